```python
import math
import jax, jax.numpy as jnp
from jax import lax
import numpy as np

D_MODEL = 1024
BATCH = 4
SEQ = 8192
DEPTH = 1

D_MIX = D_MODEL
ATTN_HEADS = 4
ATTN_QK_DIM = 64
ATTN_V_DIM = 2 * ATTN_QK_DIM
ATTN_WIDTH = ATTN_HEADS * ATTN_V_DIM
ATTN_QK_WIDTH = 2 * ATTN_HEADS * ATTN_QK_DIM
RNN_WIDTH = D_MIX - ATTN_WIDTH
RNN_BLOCKS = 8
RNN_BLOCK_W = RNN_WIDTH // RNN_BLOCKS
CONV_WIDTH = 4
RG_LRU_C = 8.0
D_FF = 2816
N_BUCKETS = 32
MAX_DISTANCE = 128
Q_BLOCK = 128
NORM_EPS = 1e-6
N_MOD = 9
IN_SPLITS = (ATTN_QK_WIDTH, 2 * ATTN_QK_WIDTH, 2 * ATTN_QK_WIDTH + ATTN_WIDTH,
             2 * ATTN_QK_WIDTH + ATTN_WIDTH + RNN_WIDTH)
D_IN = 2 * ATTN_QK_WIDTH + ATTN_WIDTH + 2 * RNN_WIDTH

kernel_name = "hymba_style_diffattn_rglru_macaron_adaln"


def rms_norm(x, g):
    xf = x.astype(jnp.float32)
    y = xf * lax.rsqrt(jnp.mean(xf * xf, axis=-1, keepdims=True) + NORM_EPS)
    return (y * g.astype(jnp.float32)).astype(x.dtype)


def modulate(h, shift, scale):
    return h * (1 + scale[:, None, :]) + shift[:, None, :]


def swiglu(h, w1, w3, w2):
    return jnp.einsum('bsf,fd->bsd', jax.nn.silu(h @ w1) * (h @ w3), w2)


def t5_bucket(rel):
    n = jnp.maximum(rel, 0)
    max_exact = N_BUCKETS // 2
    nf = jnp.maximum(n, 1).astype(jnp.float32)
    large = max_exact + (jnp.log(nf / max_exact) / math.log(MAX_DISTANCE / max_exact)
                         * (N_BUCKETS - max_exact)).astype(jnp.int32)
    large = jnp.minimum(large, N_BUCKETS - 1)
    return jnp.where(n < max_exact, n, large)


def diff_attention(q, k, v, rel_bias, lam):
    B, _, S, _ = q.shape
    scale = ATTN_QK_DIM ** -0.5
    kpos = jnp.arange(S, dtype=jnp.int32)

    def block(start):
        qb = lax.dynamic_slice_in_dim(q, start, Q_BLOCK, axis=2)
        s = jnp.einsum('bmqd,bmkd->bmqk', qb, k,
                       preferred_element_type=jnp.float32) * scale
        rel = (start + jnp.arange(Q_BLOCK, dtype=jnp.int32))[:, None] - kpos[None, :]
        bias = jnp.take(rel_bias, t5_bucket(rel), axis=0)
        s = s + jnp.transpose(bias, (2, 0, 1)).astype(jnp.float32)[None]
        s = jnp.where((rel >= 0)[None, None], s, -jnp.inf)
        p = jax.nn.softmax(s, axis=-1).reshape(B, ATTN_HEADS, 2, Q_BLOCK, S)
        a = p[:, :, 0] - lam * p[:, :, 1]
        return jnp.einsum('bhqk,bhkv->bhqv', a.astype(v.dtype), v)

    starts = jnp.arange(S // Q_BLOCK, dtype=jnp.int32) * Q_BLOCK
    o = lax.map(block, starts)
    return jnp.transpose(o, (1, 0, 3, 2, 4)).reshape(B, S, ATTN_HEADS, ATTN_V_DIM)


def causal_depthwise_conv(x, w, b):
    C = x.shape[-1]
    y = lax.conv_general_dilated(x, w[:, None, :], window_strides=(1,),
                                 padding=[(CONV_WIDTH - 1, 0)],
                                 dimension_numbers=('NWC', 'WIO', 'NWC'),
                                 feature_group_count=C)
    return y + b


def rg_lru(x, w_a, b_a, w_i, b_i, lam_L):
    B, S, C = x.shape
    xb = x.reshape(B, S, RNN_BLOCKS, RNN_BLOCK_W)
    r = jax.nn.sigmoid(jnp.einsum('bsnc,ncd->bsnd', xb, w_a).reshape(B, S, C) + b_a).astype(jnp.float32)
    i = jax.nn.sigmoid(jnp.einsum('bsnc,ncd->bsnd', xb, w_i).reshape(B, S, C) + b_i).astype(jnp.float32)
    log_a = RG_LRU_C * r * jax.nn.log_sigmoid(lam_L.astype(jnp.float32))
    a = jnp.exp(log_a)
    u = jnp.sqrt(-jnp.expm1(2.0 * log_a)) * (i * x.astype(jnp.float32))

    def combine(left, right):
        a1, b1 = left
        a2, b2 = right
        return a1 * a2, a2 * b1 + b2

    _, h = lax.associative_scan(combine, (a, u), axis=1)
    return h.astype(x.dtype)


def setup_inputs(seed: int = 0) -> dict:
    key = jax.random.key(seed)
    ks = jax.random.split(key, 32)
    f32 = jnp.float32
    nrm = lambda k, shape, s: jax.random.normal(k, shape, f32) * s
    u = jax.random.uniform(ks[21], (DEPTH, RNN_WIDTH), f32, 0.9, 0.999)
    s_a = u ** (1.0 / RG_LRU_C)
    lru_L = jnp.log(s_a) - jnp.log1p(-s_a)
    return {
        "x": nrm(ks[0], (BATCH, SEQ, D_MODEL), 1.0),
        "c": nrm(ks[1], (BATCH, D_MODEL), 1.0),
        "rel_bias": nrm(ks[2], (N_BUCKETS, 2 * ATTN_HEADS), 0.5),
        "ada_w": nrm(ks[3], (DEPTH, D_MODEL, N_MOD * D_MODEL), 0.5 * D_MODEL ** -0.5),
        "ada_b": nrm(ks[4], (DEPTH, N_MOD * D_MODEL), 0.02),
        "norm_g": 1.0 + nrm(ks[5], (DEPTH, 3, D_MODEL), 0.02),
        "ffn1_w1": nrm(ks[6], (DEPTH, D_MODEL, D_FF), D_MODEL ** -0.5),
        "ffn1_w3": nrm(ks[7], (DEPTH, D_MODEL, D_FF), D_MODEL ** -0.5),
        "ffn1_w2": nrm(ks[8], (DEPTH, D_FF, D_MODEL), D_FF ** -0.5),
        "w_in": nrm(ks[9], (DEPTH, D_MODEL, D_IN), D_MODEL ** -0.5),
        "lam_q1": nrm(ks[10], (DEPTH, ATTN_QK_DIM), 0.1),
        "lam_k1": nrm(ks[11], (DEPTH, ATTN_QK_DIM), 0.1),
        "lam_q2": nrm(ks[12], (DEPTH, ATTN_QK_DIM), 0.1),
        "lam_k2": nrm(ks[13], (DEPTH, ATTN_QK_DIM), 0.1),
        "subln_g": 1.0 + nrm(ks[14], (DEPTH, ATTN_V_DIM), 0.02),
        "conv_w": nrm(ks[15], (DEPTH, CONV_WIDTH, RNN_WIDTH), CONV_WIDTH ** -0.5),
        "conv_b": nrm(ks[16], (DEPTH, RNN_WIDTH), 0.01),
        "gate_a_w": nrm(ks[17], (DEPTH, RNN_BLOCKS, RNN_BLOCK_W, RNN_BLOCK_W), RNN_BLOCK_W ** -0.5),
        "gate_a_b": nrm(ks[18], (DEPTH, RNN_WIDTH), 0.01),
        "gate_i_w": nrm(ks[19], (DEPTH, RNN_BLOCKS, RNN_BLOCK_W, RNN_BLOCK_W), RNN_BLOCK_W ** -0.5),
        "gate_i_b": nrm(ks[20], (DEPTH, RNN_WIDTH), 0.01),
        "lru_L": lru_L,
        "w_out": nrm(ks[22], (DEPTH, D_MIX, D_MODEL), D_MIX ** -0.5),
        "ffn2_w1": nrm(ks[23], (DEPTH, D_MODEL, D_FF), D_MODEL ** -0.5),
        "ffn2_w3": nrm(ks[24], (DEPTH, D_MODEL, D_FF), D_MODEL ** -0.5),
        "ffn2_w2": nrm(ks[25], (DEPTH, D_FF, D_MODEL), D_FF ** -0.5),
        "final_g": 1.0 + nrm(ks[26], (D_MODEL,), 0.02),
    }


def reference(x, c, rel_bias, ada_w, ada_b, norm_g, ffn1_w1, ffn1_w3, ffn1_w2, w_in,
              lam_q1, lam_k1, lam_q2, lam_k2, subln_g, conv_w, conv_b,
              gate_a_w, gate_a_b, gate_i_w, gate_i_b, lru_L, w_out,
              ffn2_w1, ffn2_w3, ffn2_w2, final_g):
    B, S, _ = x.shape
    c_act = jax.nn.silu(c)
    for l in range(DEPTH):
        mod = jnp.einsum('bd,de->be', c_act, ada_w[l]) + ada_b[l]
        sh1, sc1, g1, sh2, sc2, g2, sh3, sc3, g3 = jnp.split(mod, N_MOD, axis=-1)

        h = modulate(rms_norm(x, norm_g[l, 0]), sh1, sc1)
        x = x + 0.5 * g1[:, None, :] * swiglu(h, ffn1_w1[l], ffn1_w3[l], ffn1_w2[l])

        h = modulate(rms_norm(x, norm_g[l, 1]), sh2, sc2)
        proj = h @ w_in[l]
        q, k, v, xr, gr = jnp.split(proj, IN_SPLITS, axis=-1)
        q = q.reshape(B, S, 2 * ATTN_HEADS, ATTN_QK_DIM).transpose(0, 2, 1, 3)
        k = k.reshape(B, S, 2 * ATTN_HEADS, ATTN_QK_DIM).transpose(0, 2, 1, 3)
        v = v.reshape(B, S, ATTN_HEADS, ATTN_V_DIM).transpose(0, 2, 1, 3)

        lambda_init = 0.8 - 0.6 * math.exp(-0.3 * l)
        lam = (jnp.exp(jnp.sum(lam_q1[l].astype(jnp.float32) * lam_k1[l].astype(jnp.float32)))
               - jnp.exp(jnp.sum(lam_q2[l].astype(jnp.float32) * lam_k2[l].astype(jnp.float32)))
               + lambda_init)
        o = diff_attention(q, k, v, rel_bias, lam)
        o = (rms_norm(o, subln_g[l]) * (1 - lambda_init)).reshape(B, S, ATTN_WIDTH)

        xr = causal_depthwise_conv(xr, conv_w[l], conv_b[l])
        hr = rg_lru(xr, gate_a_w[l], gate_a_b[l], gate_i_w[l], gate_i_b[l], lru_L[l])
        yr = hr * jax.nn.gelu(gr)

        mix = jnp.concatenate([o, yr], axis=-1) @ w_out[l]
        x = x + g2[:, None, :] * mix

        h = modulate(rms_norm(x, norm_g[l, 2]), sh3, sc3)
        x = x + 0.5 * g3[:, None, :] * swiglu(h, ffn2_w1[l], ffn2_w3[l], ffn2_w2[l])

    return rms_norm(x, final_g)
```

```python
import functools
import math

import jax
import jax.numpy as jnp
from jax import lax
from jax.experimental import pallas as pl
from jax.experimental.pallas import tpu as pltpu

F32 = jnp.float32
BF16 = jnp.bfloat16

ATTN_HEADS = 4
ATTN_QK_DIM = 64
ATTN_V_DIM = 2 * ATTN_QK_DIM
ATTN_WIDTH = ATTN_HEADS * ATTN_V_DIM
RNN_BLOCKS = 8
CONV_WIDTH = 4
RG_LRU_C = 8.0
N_BUCKETS = 32
MAX_DISTANCE = 128
NORM_EPS = 1e-6
N_MOD = 9

SUBLANES = 8
VMEM_LIMIT = 56 * 1024 * 1024

FFN_TM = 512
FFN_FC = 256
PROJ_TM = 512
ATTN_T = 256
RNN_TC = 1024
NEG_BIG = -1e30


def _rms(x):
    return x * lax.rsqrt(jnp.mean(x * x, axis=-1, keepdims=True) + NORM_EPS)


def _const_spec(shape):
    nd = len(shape)
    return pl.BlockSpec(shape, lambda *_: (0,) * nd, pipeline_mode=pl.Buffered(1))


def _mod_kernel(c_ref, w_ref, b_ref, o_ref):
    c = c_ref[...]
    ca = c * jax.nn.sigmoid(c)
    o_ref[0] = jnp.dot(ca, w_ref[...], preferred_element_type=F32,
                       precision=lax.Precision.HIGHEST) + b_ref[0]


def _modulation(c_pad, ada_w, ada_b):
    rows, d = c_pad.shape
    n = ada_w.shape[1] // d
    return pl.pallas_call(
        _mod_kernel,
        grid=(n,),
        in_specs=[pl.BlockSpec((rows, d), lambda j: (0, 0)),
                  pl.BlockSpec((d, d), lambda j: (0, j)),
                  pl.BlockSpec((1, 1, d), lambda j: (j, 0, 0))],
        out_specs=pl.BlockSpec((1, rows, d), lambda j: (j, 0, 0)),
        out_shape=jax.ShapeDtypeStruct((n, rows, d), F32),
        compiler_params=pltpu.CompilerParams(vmem_limit_bytes=VMEM_LIMIT),
        name="adaln_mod",
    )(c_pad, ada_w, ada_b.reshape(n, 1, d))


def _ffn_kernel(x_ref, ng_ref, sh_ref, sc_ref, g_ref, w1_ref, w3_ref, w2_ref, *rest,
                final_norm):
    if final_norm:
        fg_ref, o_ref, acc_ref = rest
    else:
        o_ref, acc_ref = rest
    x = x_ref[0]
    h = _rms(x) * ng_ref[...] * (1.0 + sc_ref[0]) + sh_ref[0]
    hb = h.astype(BF16)
    acc_ref[...] = jnp.zeros_like(acc_ref)

    def body(ci, carry):
        a = jnp.dot(hb, w1_ref[ci], preferred_element_type=F32)
        b = jnp.dot(hb, w3_ref[ci], preferred_element_type=F32)
        u = (a * jax.nn.sigmoid(a) * b).astype(BF16)
        acc_ref[...] += jnp.dot(u, w2_ref[ci], preferred_element_type=F32)
        return carry

    lax.fori_loop(0, w1_ref.shape[0], body, 0)
    y = x + 0.5 * g_ref[0] * acc_ref[...]
    if final_norm:
        y = _rms(y) * fg_ref[...]
    o_ref[0] = y


def _ffn(x, norm_g, sh, sc, g, w1c, w3c, w2c, final_g=None):
    B, S, D = x.shape
    tm = FFN_TM
    final_norm = final_g is not None
    vec = pl.BlockSpec((1, 1, D), lambda b, i: (b, 0, 0))
    in_specs = [pl.BlockSpec((1, tm, D), lambda b, i: (b, i, 0)),
                _const_spec((1, D)), vec, vec, vec,
                _const_spec(w1c.shape), _const_spec(w3c.shape), _const_spec(w2c.shape)]
    args = [x, norm_g.reshape(1, D), sh, sc, g, w1c, w3c, w2c]
    if final_norm:
        in_specs.append(_const_spec((1, D)))
        args.append(final_g.reshape(1, D))
    return pl.pallas_call(
        functools.partial(_ffn_kernel, final_norm=final_norm),
        grid=(B, S // tm),
        in_specs=in_specs,
        out_specs=pl.BlockSpec((1, tm, D), lambda b, i: (b, i, 0)),
        out_shape=jax.ShapeDtypeStruct((B, S, D), F32),
        scratch_shapes=[pltpu.VMEM((tm, D), F32)],
        compiler_params=pltpu.CompilerParams(
            dimension_semantics=("parallel", "parallel"), vmem_limit_bytes=VMEM_LIMIT),
        name="ffn_final" if final_norm else "ffn",
    )(*args)


def _inproj_kernel(x_ref, ng_ref, sh_ref, sc_ref, w_ref,
                   q_ref, k_ref, v_ref, xr_ref, gr_ref):
    x = x_ref[0]
    h = _rms(x) * ng_ref[...] * (1.0 + sc_ref[0]) + sh_ref[0]
    hb = h.astype(BF16)
    scale = ATTN_QK_DIM ** -0.5
    q_ref[0] = (jnp.dot(hb, w_ref[0], preferred_element_type=F32) * scale).astype(BF16)
    k_ref[0] = jnp.dot(hb, w_ref[1], preferred_element_type=F32).astype(BF16)
    v_ref[0] = jnp.dot(hb, w_ref[2], preferred_element_type=F32).astype(BF16)
    xr_ref[0] = jnp.dot(hb, w_ref[3], preferred_element_type=F32)
    gr_ref[0] = jnp.dot(hb, w_ref[4], preferred_element_type=F32)


def _inproj(x, norm_g, sh, sc, w_in_c):
    B, S, D = x.shape
    tm = PROJ_TM
    W = w_in_c.shape[-1]
    vec = pl.BlockSpec((1, 1, D), lambda b, i: (b, 0, 0))
    tok = pl.BlockSpec((1, tm, W), lambda b, i: (b, i, 0))
    return pl.pallas_call(
        _inproj_kernel,
        grid=(B, S // tm),
        in_specs=[pl.BlockSpec((1, tm, D), lambda b, i: (b, i, 0)),
                  _const_spec((1, D)), vec, vec, _const_spec(w_in_c.shape)],
        out_specs=[tok] * 5,
        out_shape=[jax.ShapeDtypeStruct((B, S, W), BF16)] * 3
                  + [jax.ShapeDtypeStruct((B, S, W), F32)] * 2,
        compiler_params=pltpu.CompilerParams(
            dimension_semantics=("parallel", "parallel"), vmem_limit_bytes=VMEM_LIMIT),
        name="in_proj",
    )(x, norm_g.reshape(1, D), sh, sc, w_in_c)


def _bias_kernel(rb_ref, o_ref):
    mp = pl.program_id(0)
    T = o_ref.shape[-1]
    row = lax.broadcasted_iota(jnp.int32, (T, T), 0)
    col = lax.broadcasted_iota(jnp.int32, (T, T), 1)
    max_exact = N_BUCKETS // 2
    last = rb_ref[N_BUCKETS - 1, mp]
    for which in range(2):
        rel = row - col + (T if which == 0 else 0)
        n = jnp.maximum(rel, 0)
        nf = jnp.maximum(n, 1).astype(F32)
        large = max_exact + (jnp.log(nf / max_exact) / math.log(MAX_DISTANCE / max_exact)
                             * (N_BUCKETS - max_exact)).astype(jnp.int32)
        large = jnp.minimum(large, N_BUCKETS - 1)
        bucket = jnp.where(n < max_exact, n, large)
        val = jnp.zeros((T, T), F32)
        for bk in range(N_BUCKETS - 1):
            val = jnp.where(bucket == bk, rb_ref[bk, mp] - last, val)
        if which == 1:
            val = jnp.where(rel >= 0, val, NEG_BIG)
        o_ref[0, which] = val


def _bias_tiles(rel_bias, T):
    n_maps = rel_bias.shape[1]
    return pl.pallas_call(
        _bias_kernel,
        grid=(n_maps,),
        in_specs=[pl.BlockSpec(memory_space=pltpu.SMEM)],
        out_specs=pl.BlockSpec((1, 2, T, T), lambda m: (m, 0, 0, 0)),
        out_shape=jax.ShapeDtypeStruct((n_maps, 2, T, T), F32),
        name="rel_bias_tiles",
    )(rel_bias)


def _attn_kernel(q_ref, k_ref, v_ref, bias_ref, lq1_ref, lk1_ref, lq2_ref, lk2_ref,
                 sg_ref, o_ref, qs_ref, m_ref, l_ref, acc_ref, *, lambda_init):
    T = q_ref.shape[1]
    dqk = ATTN_QK_DIM
    qi = pl.program_id(2)

    q = q_ref[0]
    lane = lax.broadcasted_iota(jnp.int32, q.shape, 1)
    zero = jnp.zeros_like(q)
    qs_ref[:T] = jnp.where(lane < dqk, q, zero)
    qs_ref[T:] = jnp.where(lane >= dqk, q, zero)
    m_ref[...] = jnp.full_like(m_ref, NEG_BIG)
    l_ref[...] = jnp.zeros_like(l_ref)
    acc_ref[...] = jnp.zeros_like(acc_ref)

    def scores(j):
        start = pl.multiple_of(j * T, T)
        kj = k_ref[0, pl.ds(start, T), :]
        vj = v_ref[0, pl.ds(start, T), :]
        s = lax.dot_general(qs_ref[...], kj, (((1,), (1,)), ((), ())),
                            preferred_element_type=F32)
        return s, vj

    def update(s, vj):
        m_old = m_ref[...]
        m_new = jnp.maximum(m_old, jnp.max(s, axis=-1, keepdims=True))
        alpha = jnp.exp(m_old - m_new)
        p = jnp.exp(s - m_new)
        l_ref[...] = alpha * l_ref[...] + jnp.sum(p, axis=-1, keepdims=True)
        acc_ref[...] = alpha * acc_ref[...] + jnp.dot(
            p.astype(BF16), vj, preferred_element_type=F32)
        m_ref[...] = m_new

    def with_bias(s, which):
        return jnp.concatenate([s[:T] + bias_ref[0, 0, which],
                                s[T:] + bias_ref[0, 1, which]], axis=0)

    def far_body(j, carry):
        update(*scores(j))
        return carry

    lax.fori_loop(0, jnp.maximum(qi - 1, 0), far_body, 0)

    @pl.when(qi >= 1)
    def _():
        s, vj = scores(qi - 1)
        update(with_bias(s, 0), vj)

    s, vj = scores(qi)
    update(with_bias(s, 1), vj)

    lam = (jnp.exp(jnp.sum(lq1_ref[...] * lk1_ref[...], axis=-1, keepdims=True))
           - jnp.exp(jnp.sum(lq2_ref[...] * lk2_ref[...], axis=-1, keepdims=True))
           + lambda_init)
    o_all = acc_ref[...] / l_ref[...]
    o = o_all[:T] - lam * o_all[T:]
    o_ref[0] = (_rms(o) * sg_ref[...] * (1.0 - lambda_init)).astype(o_ref.dtype)


def _attention(q, k, v, bias, lq1, lk1, lq2, lk2, subln_g, lambda_init):
    B, S, _ = q.shape
    T = ATTN_T
    H, dv, dqk = ATTN_HEADS, ATTN_V_DIM, ATTN_QK_DIM
    bias = bias.reshape(H, 2, 2, T, T)
    small = lambda n: _const_spec((1, n))
    return pl.pallas_call(
        functools.partial(_attn_kernel, lambda_init=lambda_init),
        grid=(B, H, S // T),
        in_specs=[pl.BlockSpec((1, T, 2 * dqk), lambda b, h, i: (b, i, h)),
                  pl.BlockSpec((1, S, 2 * dqk), lambda b, h, i: (b, 0, h)),
                  pl.BlockSpec((1, S, dv), lambda b, h, i: (b, 0, h)),
                  pl.BlockSpec((1, 2, 2, T, T), lambda b, h, i: (h, 0, 0, 0, 0)),
                  small(dqk), small(dqk), small(dqk), small(dqk), small(dv)],
        out_specs=pl.BlockSpec((1, T, dv), lambda b, h, i: (b, i, h)),
        out_shape=jax.ShapeDtypeStruct((B, S, H * dv), BF16),
        scratch_shapes=[pltpu.VMEM((2 * T, 2 * dqk), BF16),
                        pltpu.VMEM((2 * T, 1), F32),
                        pltpu.VMEM((2 * T, 1), F32),
                        pltpu.VMEM((2 * T, dv), F32)],
        compiler_params=pltpu.CompilerParams(
            dimension_semantics=("parallel", "parallel", "arbitrary"),
            vmem_limit_bytes=VMEM_LIMIT),
        name="diff_attn",
    )(q, k, v, bias, lq1.reshape(1, dqk), lk1.reshape(1, dqk), lq2.reshape(1, dqk),
      lk2.reshape(1, dqk), subln_g.reshape(1, dv))


def _rglru_kernel(xr_ref, gr_ref, cw_ref, cb_ref, wg_ref, ba_ref, bi_ref, L_ref,
                  y_ref, tail_ref, h_ref, a_buf, b_buf):
    Tc, C = xr_ref.shape[1], xr_ref.shape[2]
    G = Tc // SUBLANES

    @pl.when(pl.program_id(1) == 0)
    def _():
        tail_ref[...] = jnp.zeros_like(tail_ref)
        h_ref[...] = jnp.zeros_like(h_ref)

    x = xr_ref[0]
    tail = tail_ref[...]
    sub = lax.broadcasted_iota(jnp.int32, (SUBLANES, C), 0)
    xc = x * cw_ref[CONV_WIDTH - 1:CONV_WIDTH, :] + cb_ref[...]
    for k in range(1, CONV_WIDTH):
        xs = pltpu.roll(x, k, 0)
        head = jnp.where(sub < k, pltpu.roll(tail, k, 0), xs[:SUBLANES])
        xs = jnp.concatenate([head, xs[SUBLANES:]], axis=0)
        xc = xc + xs * cw_ref[CONV_WIDTH - 1 - k:CONV_WIDTH - k, :]
    tail_ref[...] = x[Tc - SUBLANES:]

    gates = jnp.dot(xc.astype(BF16), wg_ref[...], preferred_element_type=F32)
    r = jax.nn.sigmoid(gates[:, :C] + ba_ref[...])
    ig = jax.nn.sigmoid(gates[:, C:] + bi_ref[...])
    L = L_ref[...]
    log_sig = jnp.minimum(L, 0.0) - jnp.log1p(jnp.exp(-jnp.abs(L)))
    log_a = RG_LRU_C * r * log_sig
    a = jnp.exp(log_a)
    u = jnp.sqrt(-jnp.tanh(log_a) * (a * a + 1.0)) * (ig * xc)

    a3 = a.reshape(G, SUBLANES, C)
    u3 = u.reshape(G, SUBLANES, C)
    sub3 = lax.broadcasted_iota(jnp.int32, (G, SUBLANES, C), 1)
    for k in (1, 2, 4):
        a_sh = pltpu.roll(a3, k, 1)
        u_sh = pltpu.roll(u3, k, 1)
        ok = sub3 >= k
        u3 = jnp.where(ok, a3 * u_sh + u3, u3)
        a3 = jnp.where(ok, a3 * a_sh, a3)
    a_buf[...] = a3
    b_buf[...] = u3

    def body(g, hprev):
        hg = a_buf[g] * hprev + b_buf[g]
        b_buf[g] = hg
        return hg[SUBLANES - 1:SUBLANES, :]

    h_last = lax.fori_loop(0, G, body, h_ref[...], unroll=8)
    h_ref[...] = h_last

    h = b_buf[...].reshape(Tc, C)
    gr = gr_ref[0]
    gelu = 0.5 * gr * (1.0 + jnp.tanh(
        math.sqrt(2.0 / math.pi) * (gr + 0.044715 * (gr * gr * gr))))
    y_ref[0] = (h * gelu).astype(y_ref.dtype)


def _rglru(xr, gr, conv_w, conv_b, w_gates, b_a, b_i, lru_L):
    B, S, C = xr.shape
    Tc = RNN_TC
    tok = pl.BlockSpec((1, Tc, C), lambda b, t: (b, t, 0))
    row = lambda: _const_spec((1, C))
    return pl.pallas_call(
        _rglru_kernel,
        grid=(B, S // Tc),
        in_specs=[tok, tok, _const_spec((CONV_WIDTH, C)), row(),
                  _const_spec(w_gates.shape), row(), row(), row()],
        out_specs=tok,
        out_shape=jax.ShapeDtypeStruct((B, S, C), BF16),
        scratch_shapes=[pltpu.VMEM((SUBLANES, C), F32),
                        pltpu.VMEM((1, C), F32),
                        pltpu.VMEM((Tc // SUBLANES, SUBLANES, C), F32),
                        pltpu.VMEM((Tc // SUBLANES, SUBLANES, C), F32)],
        compiler_params=pltpu.CompilerParams(
            dimension_semantics=("parallel", "arbitrary"), vmem_limit_bytes=VMEM_LIMIT),
        name="rg_lru",
    )(xr, gr, conv_w, conv_b.reshape(1, C), w_gates, b_a.reshape(1, C),
      b_i.reshape(1, C), lru_L.reshape(1, C))


def _outproj_kernel(x_ref, o_ref, y_ref, g_ref, w_ref, out_ref):
    mix = (jnp.dot(o_ref[0], w_ref[0], preferred_element_type=F32)
           + jnp.dot(y_ref[0], w_ref[1], preferred_element_type=F32))
    out_ref[0] = x_ref[0] + g_ref[0] * mix


def _outproj(x, o, yr, g, w_out_c):
    B, S, D = x.shape
    tm = PROJ_TM
    W = o.shape[-1]
    half = pl.BlockSpec((1, tm, W), lambda b, i: (b, i, 0))
    full = pl.BlockSpec((1, tm, D), lambda b, i: (b, i, 0))
    return pl.pallas_call(
        _outproj_kernel,
        grid=(B, S // tm),
        in_specs=[full, half, half, pl.BlockSpec((1, 1, D), lambda b, i: (b, 0, 0)),
                  _const_spec(w_out_c.shape)],
        out_specs=full,
        out_shape=jax.ShapeDtypeStruct((B, S, D), F32),
        compiler_params=pltpu.CompilerParams(
            dimension_semantics=("parallel", "parallel"), vmem_limit_bytes=VMEM_LIMIT),
        name="out_proj",
    )(x, o, yr, g, w_out_c)


def _ffn_weights(w1, w3, w2):
    D, F = w1.shape
    n = F // FFN_FC
    w1c = w1.astype(BF16).reshape(D, n, FFN_FC).transpose(1, 0, 2)
    w3c = w3.astype(BF16).reshape(D, n, FFN_FC).transpose(1, 0, 2)
    w2c = w2.astype(BF16).reshape(n, FFN_FC, D)
    return w1c, w3c, w2c


def _block_diag(w):
    n, bw, _ = w.shape
    eye = jnp.eye(n, dtype=w.dtype)
    return (eye[:, None, :, None] * w[:, :, None, :]).reshape(n * bw, n * bw)


def kernel(x, c, rel_bias, ada_w, ada_b, norm_g, ffn1_w1, ffn1_w3, ffn1_w2, w_in, lam_q1, lam_k1, lam_q2, lam_k2, subln_g, conv_w, conv_b, gate_a_w, gate_a_b, gate_i_w, gate_i_b, lru_L, w_out, ffn2_w1, ffn2_w3, ffn2_w2, final_g):
    B, S, D = x.shape
    depth = ada_w.shape[0]
    c_pad = jnp.zeros((SUBLANES, D), F32).at[:B].set(c)
    bias = _bias_tiles(rel_bias, ATTN_T)
    for l in range(depth):
        mod = _modulation(c_pad, ada_w[l], ada_b[l])[:, :B]
        sh1, sc1, g1, sh2, sc2, g2, sh3, sc3, g3 = [m.reshape(B, 1, D) for m in mod]

        x = _ffn(x, norm_g[l, 0], sh1, sc1, g1,
                 *_ffn_weights(ffn1_w1[l], ffn1_w3[l], ffn1_w2[l]))

        w_in_c = w_in[l].astype(BF16).reshape(D, 5, ATTN_WIDTH).transpose(1, 0, 2)
        q, k, v, xr, gr = _inproj(x, norm_g[l, 1], sh2, sc2, w_in_c)

        lambda_init = 0.8 - 0.6 * math.exp(-0.3 * l)
        o = _attention(q, k, v, bias, lam_q1[l], lam_k1[l], lam_q2[l], lam_k2[l],
                       subln_g[l], lambda_init)

        w_gates = jnp.concatenate(
            [_block_diag(gate_a_w[l]), _block_diag(gate_i_w[l])], axis=1).astype(BF16)
        yr = _rglru(xr, gr, conv_w[l], conv_b[l], w_gates, gate_a_b[l], gate_i_b[l],
                    lru_L[l])

        w_out_c = w_out[l].astype(BF16).reshape(2, ATTN_WIDTH, D)
        x = _outproj(x, o, yr, g2, w_out_c)

        last = l == depth - 1
        x = _ffn(x, norm_g[l, 2], sh3, sc3, g3,
                 *_ffn_weights(ffn2_w1[l], ffn2_w3[l], ffn2_w2[l]),
                 final_g=final_g if last else None)
    if depth == 0:
        raise ValueError("depth must be positive")
    return x
```

```python
import functools
import math

import jax
import jax.numpy as jnp
from jax import lax
from jax.experimental import pallas as pl
from jax.experimental.pallas import tpu as pltpu

F32 = jnp.float32
BF16 = jnp.bfloat16

ATTN_HEADS = 4
ATTN_QK_DIM = 64
ATTN_V_DIM = 2 * ATTN_QK_DIM
ATTN_WIDTH = ATTN_HEADS * ATTN_V_DIM
RNN_BLOCKS = 8
CONV_WIDTH = 4
RG_LRU_C = 8.0
N_BUCKETS = 32
MAX_DISTANCE = 128
NORM_EPS = 1e-6
N_MOD = 9

SUBLANES = 8
VMEM_LIMIT = 56 * 1024 * 1024

FFN_TM = 512
FFN_FC = 256
PROJ_TM = 512
ATTN_T = 256
RNN_TC = 1024
NEG_BIG = -1e30


def _rms(x):
    return x * lax.rsqrt(jnp.mean(x * x, axis=-1, keepdims=True) + NORM_EPS)


def _const_spec(shape):
    nd = len(shape)
    return pl.BlockSpec(shape, lambda *_: (0,) * nd, pipeline_mode=pl.Buffered(1))


def _mod_kernel(c_ref, w_ref, b_ref, o_ref):
    c = c_ref[...]
    ca = c * jax.nn.sigmoid(c)
    o_ref[0] = jnp.dot(ca, w_ref[...], preferred_element_type=F32,
                       precision=lax.Precision.HIGHEST) + b_ref[0]


def _modulation(c_pad, ada_w, ada_b):
    rows, d = c_pad.shape
    n = ada_w.shape[1] // d
    return pl.pallas_call(
        _mod_kernel,
        grid=(n,),
        in_specs=[pl.BlockSpec((rows, d), lambda j: (0, 0)),
                  pl.BlockSpec((d, d), lambda j: (0, j)),
                  pl.BlockSpec((1, 1, d), lambda j: (j, 0, 0))],
        out_specs=pl.BlockSpec((1, rows, d), lambda j: (j, 0, 0)),
        out_shape=jax.ShapeDtypeStruct((n, rows, d), F32),
        compiler_params=pltpu.CompilerParams(vmem_limit_bytes=VMEM_LIMIT),
        name="adaln_mod",
    )(c_pad, ada_w, ada_b.reshape(n, 1, d))


def _ffn_kernel(x_ref, ng_ref, sh_ref, sc_ref, g_ref, w1_ref, w3_ref, w2_ref, *rest,
                final_norm):
    if final_norm:
        fg_ref, o_ref, acc_ref = rest
    else:
        o_ref, acc_ref = rest
    x = x_ref[0]
    h = _rms(x) * ng_ref[...] * (1.0 + sc_ref[0]) + sh_ref[0]
    hb = h.astype(BF16)
    acc_ref[...] = jnp.zeros_like(acc_ref)

    def body(ci, carry):
        a = jnp.dot(hb, w1_ref[ci], preferred_element_type=F32)
        b = jnp.dot(hb, w3_ref[ci], preferred_element_type=F32)
        u = (a * jax.nn.sigmoid(a) * b).astype(BF16)
        acc_ref[...] += jnp.dot(u, w2_ref[ci], preferred_element_type=F32)
        return carry

    lax.fori_loop(0, w1_ref.shape[0], body, 0)
    y = x + 0.5 * g_ref[0] * acc_ref[...]
    if final_norm:
        y = _rms(y) * fg_ref[...]
    o_ref[0] = y


def _ffn(x, norm_g, sh, sc, g, w1c, w3c, w2c, final_g=None):
    B, S, D = x.shape
    tm = FFN_TM
    final_norm = final_g is not None
    vec = pl.BlockSpec((1, 1, D), lambda b, i: (b, 0, 0))
    in_specs = [pl.BlockSpec((1, tm, D), lambda b, i: (b, i, 0)),
                _const_spec((1, D)), vec, vec, vec,
                _const_spec(w1c.shape), _const_spec(w3c.shape), _const_spec(w2c.shape)]
    args = [x, norm_g.reshape(1, D), sh, sc, g, w1c, w3c, w2c]
    if final_norm:
        in_specs.append(_const_spec((1, D)))
        args.append(final_g.reshape(1, D))
    return pl.pallas_call(
        functools.partial(_ffn_kernel, final_norm=final_norm),
        grid=(B, S // tm),
        in_specs=in_specs,
        out_specs=pl.BlockSpec((1, tm, D), lambda b, i: (b, i, 0)),
        out_shape=jax.ShapeDtypeStruct((B, S, D), F32),
        scratch_shapes=[pltpu.VMEM((tm, D), F32)],
        compiler_params=pltpu.CompilerParams(
            dimension_semantics=("parallel", "parallel"), vmem_limit_bytes=VMEM_LIMIT),
        name="ffn_final" if final_norm else "ffn",
    )(*args)


def _inproj_kernel(x_ref, ng_ref, sh_ref, sc_ref, w_ref, wvt_ref,
                   q_ref, k_ref, vt_ref, xr_ref, gr_ref):
    x = x_ref[0]
    h = _rms(x) * ng_ref[...] * (1.0 + sc_ref[0]) + sh_ref[0]
    hb = h.astype(BF16)
    scale = ATTN_QK_DIM ** -0.5
    q_ref[0] = (jnp.dot(hb, w_ref[0], preferred_element_type=F32) * scale).astype(BF16)
    k_ref[0] = jnp.dot(hb, w_ref[1], preferred_element_type=F32).astype(BF16)
    vt = lax.dot_general(wvt_ref[...], hb, (((1,), (1,)), ((), ())),
                         preferred_element_type=F32).astype(BF16)
    T = vt_ref.shape[-1]
    for ci in range(vt_ref.shape[1]):
        vt_ref[0, ci] = vt[:, ci * T:(ci + 1) * T]
    xr_ref[0] = jnp.dot(hb, w_ref[2], preferred_element_type=F32)
    gr_ref[0] = jnp.dot(hb, w_ref[3], preferred_element_type=F32)


def _inproj(x, norm_g, sh, sc, w_in_c, w_vt):
    B, S, D = x.shape
    tm, T = PROJ_TM, ATTN_T
    W = w_in_c.shape[-1]
    vec = pl.BlockSpec((1, 1, D), lambda b, i: (b, 0, 0))
    tok = pl.BlockSpec((1, tm, W), lambda b, i: (b, i, 0))
    vt_spec = pl.BlockSpec((1, tm // T, W, T), lambda b, i: (b, i, 0, 0))
    return pl.pallas_call(
        _inproj_kernel,
        grid=(B, S // tm),
        in_specs=[pl.BlockSpec((1, tm, D), lambda b, i: (b, i, 0)),
                  _const_spec((1, D)), vec, vec, _const_spec(w_in_c.shape),
                  _const_spec(w_vt.shape)],
        out_specs=[tok, tok, vt_spec, tok, tok],
        out_shape=[jax.ShapeDtypeStruct((B, S, W), BF16)] * 2
                  + [jax.ShapeDtypeStruct((B, S // T, W, T), BF16)]
                  + [jax.ShapeDtypeStruct((B, S, W), F32)] * 2,
        compiler_params=pltpu.CompilerParams(
            dimension_semantics=("parallel", "parallel"), vmem_limit_bytes=VMEM_LIMIT),
        name="in_proj",
    )(x, norm_g.reshape(1, D), sh, sc, w_in_c, w_vt)


def _bias_kernel(rb_ref, o_ref):
    mp = pl.program_id(0)
    T = o_ref.shape[-1]
    key = lax.broadcasted_iota(jnp.int32, (T, T), 0)
    qry = lax.broadcasted_iota(jnp.int32, (T, T), 1)
    max_exact = N_BUCKETS // 2
    last = rb_ref[N_BUCKETS - 1, mp]
    for which in range(2):
        rel = qry - key + (T if which == 0 else 0)
        n = jnp.maximum(rel, 0)
        nf = jnp.maximum(n, 1).astype(F32)
        large = max_exact + (jnp.log(nf / max_exact) / math.log(MAX_DISTANCE / max_exact)
                             * (N_BUCKETS - max_exact)).astype(jnp.int32)
        large = jnp.minimum(large, N_BUCKETS - 1)
        bucket = jnp.where(n < max_exact, n, large)
        val = jnp.zeros((T, T), F32)
        for bk in range(N_BUCKETS - 1):
            val = jnp.where(bucket == bk, rb_ref[bk, mp] - last, val)
        if which == 1:
            val = jnp.where(rel >= 0, val, NEG_BIG)
        o_ref[0, which] = val


def _bias_tiles(rel_bias, T):
    n_maps = rel_bias.shape[1]
    return pl.pallas_call(
        _bias_kernel,
        grid=(n_maps,),
        in_specs=[pl.BlockSpec(memory_space=pltpu.SMEM)],
        out_specs=pl.BlockSpec((1, 2, T, T), lambda m: (m, 0, 0, 0)),
        out_shape=jax.ShapeDtypeStruct((n_maps, 2, T, T), F32),
        name="rel_bias_tiles",
    )(rel_bias)


def _attn_kernel(q_ref, k_ref, vt_ref, bias_ref, lq1_ref, lk1_ref, lq2_ref, lk2_ref,
                 sg_ref, o_ref, qst_ref, m_ref, l_ref, acc_ref, *, lambda_init):
    T = q_ref.shape[1]
    dqk = ATTN_QK_DIM
    qi = pl.program_id(2)

    qt = q_ref[0].astype(F32).T
    chan = lax.broadcasted_iota(jnp.int32, qt.shape, 0)
    qst_ref[:, :T] = jnp.where(chan < dqk, qt, 0.0).astype(BF16)
    qst_ref[:, T:] = jnp.where(chan >= dqk, qt, 0.0).astype(BF16)
    m_ref[...] = jnp.full_like(m_ref, NEG_BIG)
    l_ref[...] = jnp.zeros_like(l_ref)
    acc_ref[...] = jnp.zeros_like(acc_ref)

    def scores(j):
        start = pl.multiple_of(j * T, T)
        kj = k_ref[0, pl.ds(start, T), :]
        st = jnp.dot(kj, qst_ref[...], preferred_element_type=F32)
        return st, vt_ref[0, j]

    def update(st, vt):
        m_old = m_ref[...]
        m_new = jnp.maximum(m_old, jnp.max(st, axis=0, keepdims=True))
        alpha = jnp.exp(m_old - m_new)
        p = jnp.exp(st - m_new)
        l_ref[...] = alpha * l_ref[...] + jnp.sum(p, axis=0, keepdims=True)
        acc_ref[...] = alpha * acc_ref[...] + jnp.dot(
            vt, p.astype(BF16), preferred_element_type=F32)
        m_ref[...] = m_new

    def with_bias(st, which):
        return jnp.concatenate([st[:, :T] + bias_ref[0, 0, which],
                                st[:, T:] + bias_ref[0, 1, which]], axis=1)

    def far_body(j, carry):
        update(*scores(j))
        return carry

    lax.fori_loop(0, jnp.maximum(qi - 1, 0), far_body, 0)

    @pl.when(qi >= 1)
    def _():
        st, vt = scores(qi - 1)
        update(with_bias(st, 0), vt)

    st, vt = scores(qi)
    update(with_bias(st, 1), vt)

    lam = (jnp.exp(jnp.sum(lq1_ref[...] * lk1_ref[...], axis=-1, keepdims=True))
           - jnp.exp(jnp.sum(lq2_ref[...] * lk2_ref[...], axis=-1, keepdims=True))
           + lambda_init)
    o_all = acc_ref[...] / l_ref[...]
    ot = o_all[:, :T] - lam * o_all[:, T:]
    ot = ot * lax.rsqrt(jnp.mean(ot * ot, axis=0, keepdims=True) + NORM_EPS)
    o_ref[0] = (ot.T * sg_ref[...] * (1.0 - lambda_init)).astype(o_ref.dtype)


def _attention(q, k, vt, bias, lq1, lk1, lq2, lk2, subln_g, lambda_init):
    B, S, _ = q.shape
    T = ATTN_T
    H, dv, dqk = ATTN_HEADS, ATTN_V_DIM, ATTN_QK_DIM
    bias = bias.reshape(H, 2, 2, T, T)
    small = lambda n: _const_spec((1, n))
    return pl.pallas_call(
        functools.partial(_attn_kernel, lambda_init=lambda_init),
        grid=(B, H, S // T),
        in_specs=[pl.BlockSpec((1, T, 2 * dqk), lambda b, h, i: (b, i, h)),
                  pl.BlockSpec((1, S, 2 * dqk), lambda b, h, i: (b, 0, h)),
                  pl.BlockSpec((1, S // T, dv, T), lambda b, h, i: (b, 0, h, 0)),
                  pl.BlockSpec((1, 2, 2, T, T), lambda b, h, i: (h, 0, 0, 0, 0)),
                  small(dqk), small(dqk), small(dqk), small(dqk), small(dv)],
        out_specs=pl.BlockSpec((1, T, dv), lambda b, h, i: (b, i, h)),
        out_shape=jax.ShapeDtypeStruct((B, S, H * dv), BF16),
        scratch_shapes=[pltpu.VMEM((2 * dqk, 2 * T), BF16),
                        pltpu.VMEM((1, 2 * T), F32),
                        pltpu.VMEM((1, 2 * T), F32),
                        pltpu.VMEM((dv, 2 * T), F32)],
        compiler_params=pltpu.CompilerParams(
            dimension_semantics=("parallel", "parallel", "arbitrary"),
            vmem_limit_bytes=VMEM_LIMIT),
        name="diff_attn",
    )(q, k, vt, bias, lq1.reshape(1, dqk), lk1.reshape(1, dqk), lq2.reshape(1, dqk),
      lk2.reshape(1, dqk), subln_g.reshape(1, dv))


def _rglru_kernel(xr_ref, gr_ref, cw_ref, cb_ref, wg_ref, ba_ref, bi_ref, L_ref,
                  y_ref, tail_ref, h_ref, a_buf, b_buf):
    Tc, C = xr_ref.shape[1], xr_ref.shape[2]
    G = Tc // SUBLANES

    @pl.when(pl.program_id(1) == 0)
    def _():
        tail_ref[...] = jnp.zeros_like(tail_ref)
        h_ref[...] = jnp.zeros_like(h_ref)

    x = xr_ref[0]
    tail = tail_ref[...]
    sub = lax.broadcasted_iota(jnp.int32, (SUBLANES, C), 0)
    xc = x * cw_ref[CONV_WIDTH - 1:CONV_WIDTH, :] + cb_ref[...]
    for k in range(1, CONV_WIDTH):
        xs = pltpu.roll(x, k, 0)
        head = jnp.where(sub < k, pltpu.roll(tail, k, 0), xs[:SUBLANES])
        xs = jnp.concatenate([head, xs[SUBLANES:]], axis=0)
        xc = xc + xs * cw_ref[CONV_WIDTH - 1 - k:CONV_WIDTH - k, :]
    tail_ref[...] = x[Tc - SUBLANES:]

    gates = jnp.dot(xc.astype(BF16), wg_ref[...], preferred_element_type=F32)
    r = jax.nn.sigmoid(gates[:, :C] + ba_ref[...])
    ig = jax.nn.sigmoid(gates[:, C:] + bi_ref[...])
    L = L_ref[...]
    log_sig = jnp.minimum(L, 0.0) - jnp.log1p(jnp.exp(-jnp.abs(L)))
    log_a = RG_LRU_C * r * log_sig
    a = jnp.exp(log_a)
    u = jnp.sqrt(-jnp.tanh(log_a) * (a * a + 1.0)) * (ig * xc)

    a3 = a.reshape(G, SUBLANES, C)
    u3 = u.reshape(G, SUBLANES, C)
    sub3 = lax.broadcasted_iota(jnp.int32, (G, SUBLANES, C), 1)
    for k in (1, 2, 4):
        a_sh = pltpu.roll(a3, k, 1)
        u_sh = pltpu.roll(u3, k, 1)
        ok = sub3 >= k
        u3 = jnp.where(ok, a3 * u_sh + u3, u3)
        a3 = jnp.where(ok, a3 * a_sh, a3)
    a_buf[...] = a3
    b_buf[...] = u3

    def body(g, hprev):
        hg = a_buf[g] * hprev + b_buf[g]
        b_buf[g] = hg
        return hg[SUBLANES - 1:SUBLANES, :]

    h_last = lax.fori_loop(0, G, body, h_ref[...], unroll=8)
    h_ref[...] = h_last

    h = b_buf[...].reshape(Tc, C)
    gr = gr_ref[0]
    gelu = 0.5 * gr * (1.0 + jnp.tanh(
        math.sqrt(2.0 / math.pi) * (gr + 0.044715 * (gr * gr * gr))))
    y_ref[0] = (h * gelu).astype(y_ref.dtype)


def _rglru(xr, gr, conv_w, conv_b, w_gates, b_a, b_i, lru_L):
    B, S, C = xr.shape
    Tc = RNN_TC
    tok = pl.BlockSpec((1, Tc, C), lambda b, t: (b, t, 0))
    row = lambda: _const_spec((1, C))
    return pl.pallas_call(
        _rglru_kernel,
        grid=(B, S // Tc),
        in_specs=[tok, tok, _const_spec((CONV_WIDTH, C)), row(),
                  _const_spec(w_gates.shape), row(), row(), row()],
        out_specs=tok,
        out_shape=jax.ShapeDtypeStruct((B, S, C), BF16),
        scratch_shapes=[pltpu.VMEM((SUBLANES, C), F32),
                        pltpu.VMEM((1, C), F32),
                        pltpu.VMEM((Tc // SUBLANES, SUBLANES, C), F32),
                        pltpu.VMEM((Tc // SUBLANES, SUBLANES, C), F32)],
        compiler_params=pltpu.CompilerParams(
            dimension_semantics=("parallel", "arbitrary"), vmem_limit_bytes=VMEM_LIMIT),
        name="rg_lru",
    )(xr, gr, conv_w, conv_b.reshape(1, C), w_gates, b_a.reshape(1, C),
      b_i.reshape(1, C), lru_L.reshape(1, C))


def _outproj_kernel(x_ref, o_ref, y_ref, g_ref, w_ref, out_ref):
    mix = (jnp.dot(o_ref[0], w_ref[0], preferred_element_type=F32)
           + jnp.dot(y_ref[0], w_ref[1], preferred_element_type=F32))
    out_ref[0] = x_ref[0] + g_ref[0] * mix


def _outproj(x, o, yr, g, w_out_c):
    B, S, D = x.shape
    tm = PROJ_TM
    W = o.shape[-1]
    half = pl.BlockSpec((1, tm, W), lambda b, i: (b, i, 0))
    full = pl.BlockSpec((1, tm, D), lambda b, i: (b, i, 0))
    return pl.pallas_call(
        _outproj_kernel,
        grid=(B, S // tm),
        in_specs=[full, half, half, pl.BlockSpec((1, 1, D), lambda b, i: (b, 0, 0)),
                  _const_spec(w_out_c.shape)],
        out_specs=full,
        out_shape=jax.ShapeDtypeStruct((B, S, D), F32),
        compiler_params=pltpu.CompilerParams(
            dimension_semantics=("parallel", "parallel"), vmem_limit_bytes=VMEM_LIMIT),
        name="out_proj",
    )(x, o, yr, g, w_out_c)


def _ffn_weights(w1, w3, w2):
    D, F = w1.shape
    n = F // FFN_FC
    w1c = w1.astype(BF16).reshape(D, n, FFN_FC).transpose(1, 0, 2)
    w3c = w3.astype(BF16).reshape(D, n, FFN_FC).transpose(1, 0, 2)
    w2c = w2.astype(BF16).reshape(n, FFN_FC, D)
    return w1c, w3c, w2c


def _block_diag(w):
    n, bw, _ = w.shape
    eye = jnp.eye(n, dtype=w.dtype)
    return (eye[:, None, :, None] * w[:, :, None, :]).reshape(n * bw, n * bw)


def kernel(x, c, rel_bias, ada_w, ada_b, norm_g, ffn1_w1, ffn1_w3, ffn1_w2, w_in, lam_q1, lam_k1, lam_q2, lam_k2, subln_g, conv_w, conv_b, gate_a_w, gate_a_b, gate_i_w, gate_i_b, lru_L, w_out, ffn2_w1, ffn2_w3, ffn2_w2, final_g):
    B, S, D = x.shape
    depth = ada_w.shape[0]
    c_pad = jnp.zeros((SUBLANES, D), F32).at[:B].set(c)
    bias = _bias_tiles(rel_bias, ATTN_T)
    for l in range(depth):
        mod = _modulation(c_pad, ada_w[l], ada_b[l])[:, :B]
        sh1, sc1, g1, sh2, sc2, g2, sh3, sc3, g3 = [m.reshape(B, 1, D) for m in mod]

        x = _ffn(x, norm_g[l, 0], sh1, sc1, g1,
                 *_ffn_weights(ffn1_w1[l], ffn1_w3[l], ffn1_w2[l]))

        w_in_b = w_in[l].astype(BF16).reshape(D, 5, ATTN_WIDTH)
        w_in_c = w_in_b[:, (0, 1, 3, 4), :].transpose(1, 0, 2)
        w_vt = w_in_b[:, 2, :].T
        q, k, vt, xr, gr = _inproj(x, norm_g[l, 1], sh2, sc2, w_in_c, w_vt)

        lambda_init = 0.8 - 0.6 * math.exp(-0.3 * l)
        o = _attention(q, k, vt, bias, lam_q1[l], lam_k1[l], lam_q2[l], lam_k2[l],
                       subln_g[l], lambda_init)

        w_gates = jnp.concatenate(
            [_block_diag(gate_a_w[l]), _block_diag(gate_i_w[l])], axis=1).astype(BF16)
        yr = _rglru(xr, gr, conv_w[l], conv_b[l], w_gates, gate_a_b[l], gate_i_b[l],
                    lru_L[l])

        w_out_c = w_out[l].astype(BF16).reshape(2, ATTN_WIDTH, D)
        x = _outproj(x, o, yr, g2, w_out_c)

        last = l == depth - 1
        x = _ffn(x, norm_g[l, 2], sh3, sc3, g3,
                 *_ffn_weights(ffn2_w1[l], ffn2_w3[l], ffn2_w2[l]),
                 final_g=final_g if last else None)
    return x
```

```python
import functools
import math

import jax
import jax.numpy as jnp
from jax import lax
from jax.experimental import pallas as pl
from jax.experimental.pallas import tpu as pltpu

F32 = jnp.float32
BF16 = jnp.bfloat16

ATTN_HEADS = 4
ATTN_QK_DIM = 64
ATTN_V_DIM = 2 * ATTN_QK_DIM
ATTN_WIDTH = ATTN_HEADS * ATTN_V_DIM
RNN_BLOCKS = 8
CONV_WIDTH = 4
RG_LRU_C = 8.0
N_BUCKETS = 32
MAX_DISTANCE = 128
NORM_EPS = 1e-6
N_MOD = 9

SUBLANES = 8
VMEM_LIMIT = 56 * 1024 * 1024

FFN_TM = 512
FFN_FC = 256
PROJ_TM = 512
ATTN_T = 512
ATTN_STRIP = 256
RNN_TC = 1024
NEG_BIG = -1e30
LOG2E = math.log2(math.e)


def _rms(x):
    return x * lax.rsqrt(jnp.mean(x * x, axis=-1, keepdims=True) + NORM_EPS)


def _const_spec(shape):
    nd = len(shape)
    return pl.BlockSpec(shape, lambda *_: (0,) * nd, pipeline_mode=pl.Buffered(1))


def _mod_kernel(c_ref, w_ref, b_ref, o_ref):
    c = c_ref[...]
    ca = c * jax.nn.sigmoid(c)
    o_ref[0] = jnp.dot(ca, w_ref[...], preferred_element_type=F32,
                       precision=lax.Precision.HIGHEST) + b_ref[0]


def _modulation(c_pad, ada_w, ada_b):
    rows, d = c_pad.shape
    n = ada_w.shape[1] // d
    return pl.pallas_call(
        _mod_kernel,
        grid=(n,),
        in_specs=[pl.BlockSpec((rows, d), lambda j: (0, 0)),
                  pl.BlockSpec((d, d), lambda j: (0, j)),
                  pl.BlockSpec((1, 1, d), lambda j: (j, 0, 0))],
        out_specs=pl.BlockSpec((1, rows, d), lambda j: (j, 0, 0)),
        out_shape=jax.ShapeDtypeStruct((n, rows, d), F32),
        compiler_params=pltpu.CompilerParams(vmem_limit_bytes=VMEM_LIMIT),
        name="adaln_mod",
    )(c_pad, ada_w, ada_b.reshape(n, 1, d))


def _ffn_kernel(x_ref, ng_ref, sh_ref, sc_ref, g_ref, w1_ref, w3_ref, w2_ref, *rest,
                final_norm):
    if final_norm:
        fg_ref, o_ref, acc_ref = rest
    else:
        o_ref, acc_ref = rest
    x = x_ref[0]
    h = _rms(x) * ng_ref[...] * (1.0 + sc_ref[0]) + sh_ref[0]
    hb = h.astype(BF16)
    acc_ref[...] = jnp.zeros_like(acc_ref)

    def body(ci, carry):
        a = jnp.dot(hb, w1_ref[ci], preferred_element_type=F32)
        b = jnp.dot(hb, w3_ref[ci], preferred_element_type=F32)
        u = (a * jax.nn.sigmoid(a) * b).astype(BF16)
        acc_ref[...] += jnp.dot(u, w2_ref[ci], preferred_element_type=F32)
        return carry

    lax.fori_loop(0, w1_ref.shape[0], body, 0)
    y = x + 0.5 * g_ref[0] * acc_ref[...]
    if final_norm:
        y = _rms(y) * fg_ref[...]
    o_ref[0] = y


def _ffn(x, norm_g, sh, sc, g, w1c, w3c, w2c, final_g=None):
    B, S, D = x.shape
    tm = FFN_TM
    final_norm = final_g is not None
    vec = pl.BlockSpec((1, 1, D), lambda b, i: (b, 0, 0))
    in_specs = [pl.BlockSpec((1, tm, D), lambda b, i: (b, i, 0)),
                _const_spec((1, D)), vec, vec, vec,
                _const_spec(w1c.shape), _const_spec(w3c.shape), _const_spec(w2c.shape)]
    args = [x, norm_g.reshape(1, D), sh, sc, g, w1c, w3c, w2c]
    if final_norm:
        in_specs.append(_const_spec((1, D)))
        args.append(final_g.reshape(1, D))
    return pl.pallas_call(
        functools.partial(_ffn_kernel, final_norm=final_norm),
        grid=(B, S // tm),
        in_specs=in_specs,
        out_specs=pl.BlockSpec((1, tm, D), lambda b, i: (b, i, 0)),
        out_shape=jax.ShapeDtypeStruct((B, S, D), F32),
        scratch_shapes=[pltpu.VMEM((tm, D), F32)],
        compiler_params=pltpu.CompilerParams(
            dimension_semantics=("parallel", "parallel"), vmem_limit_bytes=VMEM_LIMIT),
        name="ffn_final" if final_norm else "ffn",
    )(*args)


def _inproj_kernel(x_ref, ng_ref, sh_ref, sc_ref, w_ref, wvt_ref,
                   q_ref, k_ref, vt_ref, xr_ref, gr_ref):
    x = x_ref[0]
    h = _rms(x) * ng_ref[...] * (1.0 + sc_ref[0]) + sh_ref[0]
    hb = h.astype(BF16)
    scale = ATTN_QK_DIM ** -0.5 * LOG2E
    q_ref[0] = (jnp.dot(hb, w_ref[0], preferred_element_type=F32) * scale).astype(BF16)
    k_ref[0] = jnp.dot(hb, w_ref[1], preferred_element_type=F32).astype(BF16)
    vt = lax.dot_general(wvt_ref[...], hb, (((1,), (1,)), ((), ())),
                         preferred_element_type=F32).astype(BF16)
    T = vt_ref.shape[-1]
    for ci in range(vt_ref.shape[1]):
        vt_ref[0, ci] = vt[:, ci * T:(ci + 1) * T]
    xr_ref[0] = jnp.dot(hb, w_ref[2], preferred_element_type=F32)
    gr_ref[0] = jnp.dot(hb, w_ref[3], preferred_element_type=F32)


def _inproj(x, norm_g, sh, sc, w_in_c, w_vt):
    B, S, D = x.shape
    tm, T = PROJ_TM, ATTN_T
    W = w_in_c.shape[-1]
    vec = pl.BlockSpec((1, 1, D), lambda b, i: (b, 0, 0))
    tok = pl.BlockSpec((1, tm, W), lambda b, i: (b, i, 0))
    vt_spec = pl.BlockSpec((1, tm // T, W, T), lambda b, i: (b, i, 0, 0))
    return pl.pallas_call(
        _inproj_kernel,
        grid=(B, S // tm),
        in_specs=[pl.BlockSpec((1, tm, D), lambda b, i: (b, i, 0)),
                  _const_spec((1, D)), vec, vec, _const_spec(w_in_c.shape),
                  _const_spec(w_vt.shape)],
        out_specs=[tok, tok, vt_spec, tok, tok],
        out_shape=[jax.ShapeDtypeStruct((B, S, W), BF16)] * 2
                  + [jax.ShapeDtypeStruct((B, S // T, W, T), BF16)]
                  + [jax.ShapeDtypeStruct((B, S, W), F32)] * 2,
        compiler_params=pltpu.CompilerParams(
            dimension_semantics=("parallel", "parallel"), vmem_limit_bytes=VMEM_LIMIT),
        name="in_proj",
    )(x, norm_g.reshape(1, D), sh, sc, w_in_c, w_vt)


def _bias_kernel(rb_ref, o_ref):
    mp = pl.program_id(0)
    T = o_ref.shape[-1]
    key = lax.broadcasted_iota(jnp.int32, (T, T), 0)
    qry = lax.broadcasted_iota(jnp.int32, (T, T), 1)
    max_exact = N_BUCKETS // 2
    last = rb_ref[N_BUCKETS - 1, mp]
    for which in range(2):
        rel = qry - key + (T if which == 0 else 0)
        n = jnp.maximum(rel, 0)
        nf = jnp.maximum(n, 1).astype(F32)
        large = max_exact + (jnp.log(nf / max_exact) / math.log(MAX_DISTANCE / max_exact)
                             * (N_BUCKETS - max_exact)).astype(jnp.int32)
        large = jnp.minimum(large, N_BUCKETS - 1)
        bucket = jnp.where(n < max_exact, n, large)
        val = jnp.zeros((T, T), F32)
        for bk in range(N_BUCKETS - 1):
            val = jnp.where(bucket == bk, (rb_ref[bk, mp] - last) * LOG2E, val)
        if which == 1:
            val = jnp.where(rel >= 0, val, NEG_BIG)
        o_ref[0, which] = val


def _bias_tiles(rel_bias, T):
    n_maps = rel_bias.shape[1]
    return pl.pallas_call(
        _bias_kernel,
        grid=(n_maps,),
        in_specs=[pl.BlockSpec(memory_space=pltpu.SMEM)],
        out_specs=pl.BlockSpec((1, 2, T, T), lambda m: (m, 0, 0, 0)),
        out_shape=jax.ShapeDtypeStruct((n_maps, 2, T, T), F32),
        name="rel_bias_tiles",
    )(rel_bias)


def _attn_kernel(q_ref, k_ref, vt_ref, bias_ref, lq1_ref, lk1_ref, lq2_ref, lk2_ref,
                 sg_ref, o_ref, qst_ref, m_ref, l_ref, acc_ref, s_ref, *, lambda_init):
    T = q_ref.shape[1]
    dqk = ATTN_QK_DIM
    qi = pl.program_id(2)

    qt = q_ref[0].astype(F32).T
    chan = lax.broadcasted_iota(jnp.int32, qt.shape, 0)
    qst_ref[:, :T] = jnp.where(chan < dqk, qt, 0.0).astype(BF16)
    qst_ref[:, T:] = jnp.where(chan >= dqk, qt, 0.0).astype(BF16)
    m_ref[...] = jnp.full_like(m_ref, NEG_BIG)
    l_ref[...] = jnp.zeros_like(l_ref)
    acc_ref[...] = jnp.zeros_like(acc_ref)

    strips = [slice(c * ATTN_STRIP, (c + 1) * ATTN_STRIP)
              for c in range(2 * T // ATTN_STRIP)]

    def put_scores(buf, j, c):
        start = pl.multiple_of(j * T, T)
        kj = k_ref[0, pl.ds(start, T), :]
        s_ref[buf, :, strips[c]] = jnp.dot(kj, qst_ref[:, strips[c]],
                                           preferred_element_type=F32)

    def step(buf, j, which, j_next):
        vt = vt_ref[0, j]
        if j_next is not None:
            put_scores(1 - buf, j_next, 0)
        for c, sl in enumerate(strips):
            if j_next is not None and c + 1 < len(strips):
                put_scores(1 - buf, j_next, c + 1)
            st = s_ref[buf, :, sl]
            if which is not None:
                mp, q0 = divmod(sl.start, T)
                st = st + bias_ref[0, mp, which, :, q0:q0 + ATTN_STRIP]
            m_old = m_ref[:, sl]
            m_new = jnp.maximum(m_old, jnp.max(st, axis=0, keepdims=True))
            alpha = jnp.exp2(m_old - m_new)
            p = jnp.exp2(st - m_new)
            l_ref[:, sl] = alpha * l_ref[:, sl] + jnp.sum(p, axis=0, keepdims=True)
            acc_ref[:, sl] = alpha * acc_ref[:, sl] + jnp.dot(
                vt, p.astype(BF16), preferred_element_type=F32)
            m_ref[:, sl] = m_new

    def first_scores(j):
        for c in range(len(strips)):
            put_scores(0, j, c)

    @pl.when(qi >= 1)
    def _():
        n_far = qi - 1
        odd = n_far % 2

        @pl.when(odd == 1)
        def _():
            first_scores(0)
            step(0, 0, None, None)

        first_scores(odd)

        def pair(i, carry):
            j = odd + 2 * i
            step(0, j, None, j + 1)
            step(1, j + 1, None, j + 2)
            return carry

        lax.fori_loop(0, n_far // 2, pair, 0)
        step(0, qi - 1, 0, qi)
        step(1, qi, 1, None)

    @pl.when(qi == 0)
    def _():
        first_scores(0)
        step(0, 0, 1, None)

    lam = (jnp.exp(jnp.sum(lq1_ref[...] * lk1_ref[...], axis=-1, keepdims=True))
           - jnp.exp(jnp.sum(lq2_ref[...] * lk2_ref[...], axis=-1, keepdims=True))
           + lambda_init)
    o_all = acc_ref[...] / l_ref[...]
    ot = o_all[:, :T] - lam * o_all[:, T:]
    ot = ot * lax.rsqrt(jnp.mean(ot * ot, axis=0, keepdims=True) + NORM_EPS)
    o_ref[0] = (ot.T * sg_ref[...] * (1.0 - lambda_init)).astype(o_ref.dtype)


def _attention(q, k, vt, bias, lq1, lk1, lq2, lk2, subln_g, lambda_init):
    B, S, _ = q.shape
    T = ATTN_T
    H, dv, dqk = ATTN_HEADS, ATTN_V_DIM, ATTN_QK_DIM
    bias = bias.reshape(H, 2, 2, T, T)
    small = lambda n: _const_spec((1, n))
    return pl.pallas_call(
        functools.partial(_attn_kernel, lambda_init=lambda_init),
        grid=(B, H, S // T),
        in_specs=[pl.BlockSpec((1, T, 2 * dqk), lambda b, h, i: (b, i, h)),
                  pl.BlockSpec((1, S, 2 * dqk), lambda b, h, i: (b, 0, h)),
                  pl.BlockSpec((1, S // T, dv, T), lambda b, h, i: (b, 0, h, 0)),
                  pl.BlockSpec((1, 2, 2, T, T), lambda b, h, i: (h, 0, 0, 0, 0)),
                  small(dqk), small(dqk), small(dqk), small(dqk), small(dv)],
        out_specs=pl.BlockSpec((1, T, dv), lambda b, h, i: (b, i, h)),
        out_shape=jax.ShapeDtypeStruct((B, S, H * dv), BF16),
        scratch_shapes=[pltpu.VMEM((2 * dqk, 2 * T), BF16),
                        pltpu.VMEM((1, 2 * T), F32),
                        pltpu.VMEM((1, 2 * T), F32),
                        pltpu.VMEM((dv, 2 * T), F32),
                        pltpu.VMEM((2, T, 2 * T), F32)],
        compiler_params=pltpu.CompilerParams(
            dimension_semantics=("parallel", "parallel", "arbitrary"),
            vmem_limit_bytes=VMEM_LIMIT),
        name="diff_attn",
    )(q, k, vt, bias, lq1.reshape(1, dqk), lk1.reshape(1, dqk), lq2.reshape(1, dqk),
      lk2.reshape(1, dqk), subln_g.reshape(1, dv))


def _rglru_kernel(xr_ref, gr_ref, cw_ref, cb_ref, wg_ref, ba_ref, bi_ref, L_ref,
                  y_ref, tail_ref, h_ref, a_buf, b_buf):
    Tc, C = xr_ref.shape[1], xr_ref.shape[2]
    G = Tc // SUBLANES

    @pl.when(pl.program_id(1) == 0)
    def _():
        tail_ref[...] = jnp.zeros_like(tail_ref)
        h_ref[...] = jnp.zeros_like(h_ref)

    x = xr_ref[0]
    tail = tail_ref[...]
    sub = lax.broadcasted_iota(jnp.int32, (SUBLANES, C), 0)
    xc = x * cw_ref[CONV_WIDTH - 1:CONV_WIDTH, :] + cb_ref[...]
    for k in range(1, CONV_WIDTH):
        xs = pltpu.roll(x, k, 0)
        head = jnp.where(sub < k, pltpu.roll(tail, k, 0), xs[:SUBLANES])
        xs = jnp.concatenate([head, xs[SUBLANES:]], axis=0)
        xc = xc + xs * cw_ref[CONV_WIDTH - 1 - k:CONV_WIDTH - k, :]
    tail_ref[...] = x[Tc - SUBLANES:]

    gates = jnp.dot(xc.astype(BF16), wg_ref[...], preferred_element_type=F32)
    r = jax.nn.sigmoid(gates[:, :C] + ba_ref[...])
    ig = jax.nn.sigmoid(gates[:, C:] + bi_ref[...])
    L = L_ref[...]
    log_sig = jnp.minimum(L, 0.0) - jnp.log1p(jnp.exp(-jnp.abs(L)))
    log_a = RG_LRU_C * r * log_sig
    a = jnp.exp(log_a)
    u = jnp.sqrt(-jnp.tanh(log_a) * (a * a + 1.0)) * (ig * xc)

    a3 = a.reshape(G, SUBLANES, C)
    u3 = u.reshape(G, SUBLANES, C)
    sub3 = lax.broadcasted_iota(jnp.int32, (G, SUBLANES, C), 1)
    for k in (1, 2, 4):
        a_sh = pltpu.roll(a3, k, 1)
        u_sh = pltpu.roll(u3, k, 1)
        ok = sub3 >= k
        u3 = jnp.where(ok, a3 * u_sh + u3, u3)
        a3 = jnp.where(ok, a3 * a_sh, a3)
    a_buf[...] = a3
    b_buf[...] = u3

    def body(g, hprev):
        hg = a_buf[g] * hprev + b_buf[g]
        b_buf[g] = hg
        return hg[SUBLANES - 1:SUBLANES, :]

    h_last = lax.fori_loop(0, G, body, h_ref[...], unroll=8)
    h_ref[...] = h_last

    h = b_buf[...].reshape(Tc, C)
    gr = gr_ref[0]
    gelu = 0.5 * gr * (1.0 + jnp.tanh(
        math.sqrt(2.0 / math.pi) * (gr + 0.044715 * (gr * gr * gr))))
    y_ref[0] = (h * gelu).astype(y_ref.dtype)


def _rglru(xr, gr, conv_w, conv_b, w_gates, b_a, b_i, lru_L):
    B, S, C = xr.shape
    Tc = RNN_TC
    tok = pl.BlockSpec((1, Tc, C), lambda b, t: (b, t, 0))
    row = lambda: _const_spec((1, C))
    return pl.pallas_call(
        _rglru_kernel,
        grid=(B, S // Tc),
        in_specs=[tok, tok, _const_spec((CONV_WIDTH, C)), row(),
                  _const_spec(w_gates.shape), row(), row(), row()],
        out_specs=tok,
        out_shape=jax.ShapeDtypeStruct((B, S, C), BF16),
        scratch_shapes=[pltpu.VMEM((SUBLANES, C), F32),
                        pltpu.VMEM((1, C), F32),
                        pltpu.VMEM((Tc // SUBLANES, SUBLANES, C), F32),
                        pltpu.VMEM((Tc // SUBLANES, SUBLANES, C), F32)],
        compiler_params=pltpu.CompilerParams(
            dimension_semantics=("parallel", "arbitrary"), vmem_limit_bytes=VMEM_LIMIT),
        name="rg_lru",
    )(xr, gr, conv_w, conv_b.reshape(1, C), w_gates, b_a.reshape(1, C),
      b_i.reshape(1, C), lru_L.reshape(1, C))


def _outproj_kernel(x_ref, o_ref, y_ref, g_ref, w_ref, out_ref):
    mix = (jnp.dot(o_ref[0], w_ref[0], preferred_element_type=F32)
           + jnp.dot(y_ref[0], w_ref[1], preferred_element_type=F32))
    out_ref[0] = x_ref[0] + g_ref[0] * mix


def _outproj(x, o, yr, g, w_out_c):
    B, S, D = x.shape
    tm = PROJ_TM
    W = o.shape[-1]
    half = pl.BlockSpec((1, tm, W), lambda b, i: (b, i, 0))
    full = pl.BlockSpec((1, tm, D), lambda b, i: (b, i, 0))
    return pl.pallas_call(
        _outproj_kernel,
        grid=(B, S // tm),
        in_specs=[full, half, half, pl.BlockSpec((1, 1, D), lambda b, i: (b, 0, 0)),
                  _const_spec(w_out_c.shape)],
        out_specs=full,
        out_shape=jax.ShapeDtypeStruct((B, S, D), F32),
        compiler_params=pltpu.CompilerParams(
            dimension_semantics=("parallel", "parallel"), vmem_limit_bytes=VMEM_LIMIT),
        name="out_proj",
    )(x, o, yr, g, w_out_c)


def _ffn_weights(w1, w3, w2):
    D, F = w1.shape
    n = F // FFN_FC
    w1c = w1.astype(BF16).reshape(D, n, FFN_FC).transpose(1, 0, 2)
    w3c = w3.astype(BF16).reshape(D, n, FFN_FC).transpose(1, 0, 2)
    w2c = w2.astype(BF16).reshape(n, FFN_FC, D)
    return w1c, w3c, w2c


def _block_diag(w):
    n, bw, _ = w.shape
    eye = jnp.eye(n, dtype=w.dtype)
    return (eye[:, None, :, None] * w[:, :, None, :]).reshape(n * bw, n * bw)


def kernel(x, c, rel_bias, ada_w, ada_b, norm_g, ffn1_w1, ffn1_w3, ffn1_w2, w_in, lam_q1, lam_k1, lam_q2, lam_k2, subln_g, conv_w, conv_b, gate_a_w, gate_a_b, gate_i_w, gate_i_b, lru_L, w_out, ffn2_w1, ffn2_w3, ffn2_w2, final_g):
    B, S, D = x.shape
    depth = ada_w.shape[0]
    c_pad = jnp.zeros((SUBLANES, D), F32).at[:B].set(c)
    bias = _bias_tiles(rel_bias, ATTN_T)
    for l in range(depth):
        mod = _modulation(c_pad, ada_w[l], ada_b[l])[:, :B]
        sh1, sc1, g1, sh2, sc2, g2, sh3, sc3, g3 = [m.reshape(B, 1, D) for m in mod]

        x = _ffn(x, norm_g[l, 0], sh1, sc1, g1,
                 *_ffn_weights(ffn1_w1[l], ffn1_w3[l], ffn1_w2[l]))

        w_in_b = w_in[l].astype(BF16).reshape(D, 5, ATTN_WIDTH)
        w_in_c = w_in_b[:, (0, 1, 3, 4), :].transpose(1, 0, 2)
        w_vt = w_in_b[:, 2, :].T
        q, k, vt, xr, gr = _inproj(x, norm_g[l, 1], sh2, sc2, w_in_c, w_vt)

        lambda_init = 0.8 - 0.6 * math.exp(-0.3 * l)
        o = _attention(q, k, vt, bias, lam_q1[l], lam_k1[l], lam_q2[l], lam_k2[l],
                       subln_g[l], lambda_init)

        w_gates = jnp.concatenate(
            [_block_diag(gate_a_w[l]), _block_diag(gate_i_w[l])], axis=1).astype(BF16)
        yr = _rglru(xr, gr, conv_w[l], conv_b[l], w_gates, gate_a_b[l], gate_i_b[l],
                    lru_L[l])

        w_out_c = w_out[l].astype(BF16).reshape(2, ATTN_WIDTH, D)
        x = _outproj(x, o, yr, g2, w_out_c)

        last = l == depth - 1
        x = _ffn(x, norm_g[l, 2], sh3, sc3, g3,
                 *_ffn_weights(ffn2_w1[l], ffn2_w3[l], ffn2_w2[l]),
                 final_g=final_g if last else None)
    return x
```

```python
import functools
import math

import jax
import jax.numpy as jnp
from jax import lax
from jax.experimental import pallas as pl
from jax.experimental.pallas import tpu as pltpu

F32 = jnp.float32
BF16 = jnp.bfloat16

ATTN_HEADS = 4
ATTN_QK_DIM = 64
ATTN_V_DIM = 2 * ATTN_QK_DIM
ATTN_WIDTH = ATTN_HEADS * ATTN_V_DIM
RNN_BLOCKS = 8
CONV_WIDTH = 4
RG_LRU_C = 8.0
N_BUCKETS = 32
MAX_DISTANCE = 128
NORM_EPS = 1e-6
N_MOD = 9

SUBLANES = 8
VMEM_LIMIT = 56 * 1024 * 1024

FFN_TM = 512
FFN_FC = 256
PROJ_TM = 512
ATTN_T = 512
ATTN_STRIP = 256
BIAS_T = 128
RNN_TC = 1024
NEG_BIG = -1e30
LOG2E = math.log2(math.e)


def _rms(x):
    return x * lax.rsqrt(jnp.mean(x * x, axis=-1, keepdims=True) + NORM_EPS)


def _const_spec(shape):
    nd = len(shape)
    return pl.BlockSpec(shape, lambda *_: (0,) * nd, pipeline_mode=pl.Buffered(1))


def _mod_kernel(c_ref, w_ref, b_ref, o_ref):
    c = c_ref[...]
    ca = c * jax.nn.sigmoid(c)
    o_ref[0] = jnp.dot(ca, w_ref[...], preferred_element_type=F32,
                       precision=lax.Precision.HIGHEST) + b_ref[0]


def _modulation(c_pad, ada_w, ada_b):
    rows, d = c_pad.shape
    n = ada_w.shape[1] // d
    return pl.pallas_call(
        _mod_kernel,
        grid=(n,),
        in_specs=[pl.BlockSpec((rows, d), lambda j: (0, 0)),
                  pl.BlockSpec((d, d), lambda j: (0, j)),
                  pl.BlockSpec((1, 1, d), lambda j: (j, 0, 0))],
        out_specs=pl.BlockSpec((1, rows, d), lambda j: (j, 0, 0)),
        out_shape=jax.ShapeDtypeStruct((n, rows, d), F32),
        compiler_params=pltpu.CompilerParams(vmem_limit_bytes=VMEM_LIMIT),
        name="adaln_mod",
    )(c_pad, ada_w, ada_b.reshape(n, 1, d))


def _ffn_kernel(x_ref, ng_ref, sh_ref, sc_ref, g_ref, w1_ref, w3_ref, w2_ref, *rest,
                final_norm):
    if final_norm:
        fg_ref, o_ref, acc_ref = rest
    else:
        o_ref, acc_ref = rest
    x = x_ref[0]
    h = _rms(x) * ng_ref[...] * (1.0 + sc_ref[0]) + sh_ref[0]
    hb = h.astype(BF16)
    n_chunks = w1_ref.shape[1] // FFN_FC
    cols = lambda ci: slice(ci * FFN_FC, (ci + 1) * FFN_FC)
    up = lambda ci: (jnp.dot(hb, w1_ref[:, cols(ci)], preferred_element_type=F32),
                     jnp.dot(hb, w3_ref[:, cols(ci)], preferred_element_type=F32))
    a, b = up(0)
    for ci in range(n_chunks):
        if ci + 1 < n_chunks:
            a_next, b_next = up(ci + 1)
        u = (a * jax.nn.sigmoid(a) * b).astype(BF16)
        down = jnp.dot(u, w2_ref[cols(ci), :], preferred_element_type=F32)
        if ci == 0:
            acc_ref[...] = down
        else:
            acc_ref[...] += down
        a, b = a_next, b_next
    y = x + 0.5 * g_ref[0] * acc_ref[...]
    if final_norm:
        y = _rms(y) * fg_ref[...]
    o_ref[0] = y


def _ffn(x, norm_g, sh, sc, g, w1c, w3c, w2c, final_g=None):
    B, S, D = x.shape
    tm = FFN_TM
    final_norm = final_g is not None
    vec = pl.BlockSpec((1, 1, D), lambda b, i: (b, 0, 0))
    in_specs = [pl.BlockSpec((1, tm, D), lambda b, i: (b, i, 0)),
                _const_spec((1, D)), vec, vec, vec,
                _const_spec(w1c.shape), _const_spec(w3c.shape), _const_spec(w2c.shape)]
    args = [x, norm_g.reshape(1, D), sh, sc, g, w1c, w3c, w2c]
    if final_norm:
        in_specs.append(_const_spec((1, D)))
        args.append(final_g.reshape(1, D))
    return pl.pallas_call(
        functools.partial(_ffn_kernel, final_norm=final_norm),
        grid=(B, S // tm),
        in_specs=in_specs,
        out_specs=pl.BlockSpec((1, tm, D), lambda b, i: (b, i, 0)),
        out_shape=jax.ShapeDtypeStruct((B, S, D), F32),
        scratch_shapes=[pltpu.VMEM((tm, D), F32)],
        compiler_params=pltpu.CompilerParams(
            dimension_semantics=("parallel", "parallel"), vmem_limit_bytes=VMEM_LIMIT),
        name="ffn_final" if final_norm else "ffn",
    )(*args)


def _inproj_kernel(x_ref, ng_ref, sh_ref, sc_ref, w_ref, wvt_ref,
                   q_ref, k_ref, vt_ref, xr_ref, gr_ref):
    x = x_ref[0]
    h = _rms(x) * ng_ref[...] * (1.0 + sc_ref[0]) + sh_ref[0]
    hb = h.astype(BF16)
    scale = ATTN_QK_DIM ** -0.5 * LOG2E
    W = q_ref.shape[-1]
    proj = lambda n: jnp.dot(hb, w_ref[:, n * W:(n + 1) * W], preferred_element_type=F32)
    q_ref[0] = (proj(0) * scale).astype(BF16)
    k_ref[0] = proj(1).astype(BF16)
    vt = lax.dot_general(wvt_ref[...], hb, (((1,), (1,)), ((), ())),
                         preferred_element_type=F32).astype(BF16)
    T = vt_ref.shape[-1]
    for ci in range(vt_ref.shape[1]):
        vt_ref[0, ci] = vt[:, ci * T:(ci + 1) * T]
    xr_ref[0] = proj(3)
    gr_ref[0] = proj(4)


def _inproj(x, norm_g, sh, sc, w_in_c, w_vt):
    B, S, D = x.shape
    tm, T = PROJ_TM, ATTN_T
    W = w_vt.shape[0]
    vec = pl.BlockSpec((1, 1, D), lambda b, i: (b, 0, 0))
    tok = pl.BlockSpec((1, tm, W), lambda b, i: (b, i, 0))
    vt_spec = pl.BlockSpec((1, tm // T, W, T), lambda b, i: (b, i, 0, 0))
    return pl.pallas_call(
        _inproj_kernel,
        grid=(B, S // tm),
        in_specs=[pl.BlockSpec((1, tm, D), lambda b, i: (b, i, 0)),
                  _const_spec((1, D)), vec, vec, _const_spec(w_in_c.shape),
                  _const_spec(w_vt.shape)],
        out_specs=[tok, tok, vt_spec, tok, tok],
        out_shape=[jax.ShapeDtypeStruct((B, S, W), BF16)] * 2
                  + [jax.ShapeDtypeStruct((B, S // T, W, T), BF16)]
                  + [jax.ShapeDtypeStruct((B, S, W), F32)] * 2,
        compiler_params=pltpu.CompilerParams(
            dimension_semantics=("parallel", "parallel"), vmem_limit_bytes=VMEM_LIMIT),
        name="in_proj",
    )(x, norm_g.reshape(1, D), sh, sc, w_in_c, w_vt)


def _bias_kernel(rb_ref, o_ref):
    mp = pl.program_id(0)
    T = o_ref.shape[-1]
    key = lax.broadcasted_iota(jnp.int32, (T, T), 0)
    qry = lax.broadcasted_iota(jnp.int32, (T, T), 1)
    max_exact = N_BUCKETS // 2
    last = rb_ref[N_BUCKETS - 1, mp]
    for which in range(2):
        rel = qry - key + (T if which == 0 else 0)
        n = jnp.maximum(rel, 0)
        nf = jnp.maximum(n, 1).astype(F32)
        large = max_exact + (jnp.log(nf / max_exact) / math.log(MAX_DISTANCE / max_exact)
                             * (N_BUCKETS - max_exact)).astype(jnp.int32)
        large = jnp.minimum(large, N_BUCKETS - 1)
        bucket = jnp.where(n < max_exact, n, large)
        val = jnp.zeros((T, T), F32)
        for bk in range(N_BUCKETS - 1):
            val = jnp.where(bucket == bk, (rb_ref[bk, mp] - last) * LOG2E, val)
        if which == 1:
            val = jnp.where(rel >= 0, val, NEG_BIG)
        o_ref[0, which] = val


def _bias_tiles(rel_bias, T):
    n_maps = rel_bias.shape[1]
    return pl.pallas_call(
        _bias_kernel,
        grid=(n_maps,),
        in_specs=[pl.BlockSpec(memory_space=pltpu.SMEM)],
        out_specs=pl.BlockSpec((1, 2, T, T), lambda m: (m, 0, 0, 0)),
        out_shape=jax.ShapeDtypeStruct((n_maps, 2, T, T), F32),
        name="rel_bias_tiles",
    )(rel_bias)


def _attn_kernel(q_ref, k_ref, vt_ref, bias_ref, lq1_ref, lk1_ref, lq2_ref, lk2_ref,
                 sg_ref, o_ref, qst_ref, m_ref, l_ref, acc_ref, s_ref, *, lambda_init):
    T = q_ref.shape[1]
    dqk = ATTN_QK_DIM
    qi = pl.program_id(2)

    qt = q_ref[0].astype(F32).T
    chan = lax.broadcasted_iota(jnp.int32, qt.shape, 0)
    qst_ref[:, :T] = jnp.where(chan < dqk, qt, 0.0).astype(BF16)
    qst_ref[:, T:] = jnp.where(chan >= dqk, qt, 0.0).astype(BF16)
    m_ref[...] = jnp.full_like(m_ref, NEG_BIG)
    l_ref[...] = jnp.zeros_like(l_ref)
    acc_ref[...] = jnp.zeros_like(acc_ref)

    strips = [slice(c * ATTN_STRIP, (c + 1) * ATTN_STRIP)
              for c in range(2 * T // ATTN_STRIP)]

    def put_scores(buf, j, c):
        start = pl.multiple_of(j * T, T)
        kj = k_ref[0, pl.ds(start, T), :]
        s_ref[buf, :, strips[c]] = jnp.dot(kj, qst_ref[:, strips[c]],
                                           preferred_element_type=F32)

    def biased(st, which, mp, q0):
        n_key = T // BIAS_T
        rows = []
        for a in range(n_key):
            cols = []
            for b in range(q0 // BIAS_T, (q0 + ATTN_STRIP) // BIAS_T):
                lo = (b - q0 // BIAS_T) * BIAS_T
                piece = st[a * BIAS_T:(a + 1) * BIAS_T, lo:lo + BIAS_T]
                dist = b - a + (n_key if which == 0 else 0)
                if dist < 0:
                    piece = jnp.full_like(piece, NEG_BIG)
                elif dist <= 1:
                    piece = piece + bias_ref[0, mp, 1 - dist]
                cols.append(piece)
            rows.append(jnp.concatenate(cols, axis=1))
        return jnp.concatenate(rows, axis=0)

    def step(buf, j, which, j_next):
        vt = vt_ref[0, j]
        if j_next is not None:
            put_scores(1 - buf, j_next, 0)
        for c, sl in enumerate(strips):
            if j_next is not None and c + 1 < len(strips):
                put_scores(1 - buf, j_next, c + 1)
            st = s_ref[buf, :, sl]
            if which is not None:
                st = biased(st, which, *divmod(sl.start, T))
            m_old = m_ref[:, sl]
            m_new = jnp.maximum(m_old, jnp.max(st, axis=0, keepdims=True))
            alpha = jnp.exp2(m_old - m_new)
            p = jnp.exp2(st - m_new)
            l_ref[:, sl] = alpha * l_ref[:, sl] + jnp.sum(p, axis=0, keepdims=True)
            acc_ref[:, sl] = alpha * acc_ref[:, sl] + jnp.dot(
                vt, p.astype(BF16), preferred_element_type=F32)
            m_ref[:, sl] = m_new

    def first_scores(j):
        for c in range(len(strips)):
            put_scores(0, j, c)

    @pl.when(qi >= 1)
    def _():
        n_far = qi - 1
        odd = n_far % 2

        @pl.when(odd == 1)
        def _():
            first_scores(0)
            step(0, 0, None, None)

        first_scores(odd)

        def pair(i, carry):
            j = odd + 2 * i
            step(0, j, None, j + 1)
            step(1, j + 1, None, j + 2)
            return carry

        lax.fori_loop(0, n_far // 2, pair, 0)
        step(0, qi - 1, 0, qi)
        step(1, qi, 1, None)

    @pl.when(qi == 0)
    def _():
        first_scores(0)
        step(0, 0, 1, None)

    lam = (jnp.exp(jnp.sum(lq1_ref[...] * lk1_ref[...], axis=-1, keepdims=True))
           - jnp.exp(jnp.sum(lq2_ref[...] * lk2_ref[...], axis=-1, keepdims=True))
           + lambda_init)
    o_all = acc_ref[...] / l_ref[...]
    ot = o_all[:, :T] - lam * o_all[:, T:]
    ot = ot * lax.rsqrt(jnp.mean(ot * ot, axis=0, keepdims=True) + NORM_EPS)
    o_ref[0] = (ot.T * sg_ref[...] * (1.0 - lambda_init)).astype(o_ref.dtype)


def _attention(q, k, vt, bias, lq1, lk1, lq2, lk2, subln_g, lambda_init):
    B, S, _ = q.shape
    T = ATTN_T
    H, dv, dqk = ATTN_HEADS, ATTN_V_DIM, ATTN_QK_DIM
    bias = bias.reshape(H, 2, 2, BIAS_T, BIAS_T)
    small = lambda n: _const_spec((1, n))
    return pl.pallas_call(
        functools.partial(_attn_kernel, lambda_init=lambda_init),
        grid=(B, H, S // T),
        in_specs=[pl.BlockSpec((1, T, 2 * dqk), lambda b, h, i: (b, i, h)),
                  pl.BlockSpec((1, S, 2 * dqk), lambda b, h, i: (b, 0, h)),
                  pl.BlockSpec((1, S // T, dv, T), lambda b, h, i: (b, 0, h, 0)),
                  pl.BlockSpec((1, 2, 2, BIAS_T, BIAS_T), lambda b, h, i: (h, 0, 0, 0, 0)),
                  small(dqk), small(dqk), small(dqk), small(dqk), small(dv)],
        out_specs=pl.BlockSpec((1, T, dv), lambda b, h, i: (b, i, h)),
        out_shape=jax.ShapeDtypeStruct((B, S, H * dv), BF16),
        scratch_shapes=[pltpu.VMEM((2 * dqk, 2 * T), BF16),
                        pltpu.VMEM((1, 2 * T), F32),
                        pltpu.VMEM((1, 2 * T), F32),
                        pltpu.VMEM((dv, 2 * T), F32),
                        pltpu.VMEM((2, T, 2 * T), F32)],
        compiler_params=pltpu.CompilerParams(
            dimension_semantics=("parallel", "parallel", "arbitrary"),
            vmem_limit_bytes=VMEM_LIMIT),
        name="diff_attn",
    )(q, k, vt, bias, lq1.reshape(1, dqk), lk1.reshape(1, dqk), lq2.reshape(1, dqk),
      lk2.reshape(1, dqk), subln_g.reshape(1, dv))


def _rglru_kernel(xr_ref, gr_ref, cw_ref, cb_ref, wg_ref, ba_ref, bi_ref, L_ref,
                  y_ref, tail_ref, h_ref, a_buf, b_buf):
    Tc, C = xr_ref.shape[1], xr_ref.shape[2]
    G = Tc // SUBLANES

    @pl.when(pl.program_id(1) == 0)
    def _():
        tail_ref[...] = jnp.zeros_like(tail_ref)
        h_ref[...] = jnp.zeros_like(h_ref)

    x = xr_ref[0]
    tail = tail_ref[...]
    sub = lax.broadcasted_iota(jnp.int32, (SUBLANES, C), 0)
    xc = x * cw_ref[CONV_WIDTH - 1:CONV_WIDTH, :] + cb_ref[...]
    for k in range(1, CONV_WIDTH):
        xs = pltpu.roll(x, k, 0)
        head = jnp.where(sub < k, pltpu.roll(tail, k, 0), xs[:SUBLANES])
        xs = jnp.concatenate([head, xs[SUBLANES:]], axis=0)
        xc = xc + xs * cw_ref[CONV_WIDTH - 1 - k:CONV_WIDTH - k, :]
    tail_ref[...] = x[Tc - SUBLANES:]

    gates = jnp.dot(xc.astype(BF16), wg_ref[...], preferred_element_type=F32)
    r = jax.nn.sigmoid(gates[:, :C] + ba_ref[...])
    ig = jax.nn.sigmoid(gates[:, C:] + bi_ref[...])
    L = L_ref[...]
    log_sig = jnp.minimum(L, 0.0) - jnp.log1p(jnp.exp(-jnp.abs(L)))
    log_a = RG_LRU_C * r * log_sig
    a = jnp.exp(log_a)
    u = jnp.sqrt(-jnp.tanh(log_a) * (a * a + 1.0)) * (ig * xc)

    a3 = a.reshape(G, SUBLANES, C)
    u3 = u.reshape(G, SUBLANES, C)
    sub3 = lax.broadcasted_iota(jnp.int32, (G, SUBLANES, C), 1)
    for k in (1, 2, 4):
        a_sh = pltpu.roll(a3, k, 1)
        u_sh = pltpu.roll(u3, k, 1)
        ok = sub3 >= k
        u3 = jnp.where(ok, a3 * u_sh + u3, u3)
        a3 = jnp.where(ok, a3 * a_sh, a3)
    a_buf[...] = a3
    b_buf[...] = u3

    def body(g, hprev):
        hg = a_buf[g] * hprev + b_buf[g]
        b_buf[g] = hg
        return hg[SUBLANES - 1:SUBLANES, :]

    h_last = lax.fori_loop(0, G, body, h_ref[...], unroll=8)
    h_ref[...] = h_last

    h = b_buf[...].reshape(Tc, C)
    gr = gr_ref[0]
    gelu = 0.5 * gr * (1.0 + jnp.tanh(
        math.sqrt(2.0 / math.pi) * (gr + 0.044715 * (gr * gr * gr))))
    y_ref[0] = (h * gelu).astype(y_ref.dtype)


def _rglru(xr, gr, conv_w, conv_b, w_gates, b_a, b_i, lru_L):
    B, S, C = xr.shape
    Tc = RNN_TC
    tok = pl.BlockSpec((1, Tc, C), lambda b, t: (b, t, 0))
    row = lambda: _const_spec((1, C))
    return pl.pallas_call(
        _rglru_kernel,
        grid=(B, S // Tc),
        in_specs=[tok, tok, _const_spec((CONV_WIDTH, C)), row(),
                  _const_spec(w_gates.shape), row(), row(), row()],
        out_specs=tok,
        out_shape=jax.ShapeDtypeStruct((B, S, C), BF16),
        scratch_shapes=[pltpu.VMEM((SUBLANES, C), F32),
                        pltpu.VMEM((1, C), F32),
                        pltpu.VMEM((Tc // SUBLANES, SUBLANES, C), F32),
                        pltpu.VMEM((Tc // SUBLANES, SUBLANES, C), F32)],
        compiler_params=pltpu.CompilerParams(
            dimension_semantics=("parallel", "arbitrary"), vmem_limit_bytes=VMEM_LIMIT),
        name="rg_lru",
    )(xr, gr, conv_w, conv_b.reshape(1, C), w_gates, b_a.reshape(1, C),
      b_i.reshape(1, C), lru_L.reshape(1, C))


def _outproj_kernel(x_ref, o_ref, y_ref, g_ref, w_ref, out_ref):
    mix = (jnp.dot(o_ref[0], w_ref[0], preferred_element_type=F32)
           + jnp.dot(y_ref[0], w_ref[1], preferred_element_type=F32))
    out_ref[0] = x_ref[0] + g_ref[0] * mix


def _outproj(x, o, yr, g, w_out_c):
    B, S, D = x.shape
    tm = PROJ_TM
    W = o.shape[-1]
    half = pl.BlockSpec((1, tm, W), lambda b, i: (b, i, 0))
    full = pl.BlockSpec((1, tm, D), lambda b, i: (b, i, 0))
    return pl.pallas_call(
        _outproj_kernel,
        grid=(B, S // tm),
        in_specs=[full, half, half, pl.BlockSpec((1, 1, D), lambda b, i: (b, 0, 0)),
                  _const_spec(w_out_c.shape)],
        out_specs=full,
        out_shape=jax.ShapeDtypeStruct((B, S, D), F32),
        compiler_params=pltpu.CompilerParams(
            dimension_semantics=("parallel", "parallel"), vmem_limit_bytes=VMEM_LIMIT),
        name="out_proj",
    )(x, o, yr, g, w_out_c)


def _ffn_weights(w1, w3, w2):
    return w1.astype(BF16), w3.astype(BF16), w2.astype(BF16)


def _block_diag(w):
    n, bw, _ = w.shape
    eye = jnp.eye(n, dtype=w.dtype)
    return (eye[:, None, :, None] * w[:, :, None, :]).reshape(n * bw, n * bw)


def kernel(x, c, rel_bias, ada_w, ada_b, norm_g, ffn1_w1, ffn1_w3, ffn1_w2, w_in, lam_q1, lam_k1, lam_q2, lam_k2, subln_g, conv_w, conv_b, gate_a_w, gate_a_b, gate_i_w, gate_i_b, lru_L, w_out, ffn2_w1, ffn2_w3, ffn2_w2, final_g):
    B, S, D = x.shape
    depth = ada_w.shape[0]
    c_pad = jnp.zeros((SUBLANES, D), F32).at[:B].set(c)
    bias = _bias_tiles(rel_bias, BIAS_T)
    for l in range(depth):
        mod = _modulation(c_pad, ada_w[l], ada_b[l])[:, :B]
        sh1, sc1, g1, sh2, sc2, g2, sh3, sc3, g3 = [m.reshape(B, 1, D) for m in mod]

        x = _ffn(x, norm_g[l, 0], sh1, sc1, g1,
                 *_ffn_weights(ffn1_w1[l], ffn1_w3[l], ffn1_w2[l]))

        w_in_b = w_in[l].astype(BF16)
        w_vt = w_in_b[:, 2 * ATTN_WIDTH:3 * ATTN_WIDTH].T
        q, k, vt, xr, gr = _inproj(x, norm_g[l, 1], sh2, sc2, w_in_b, w_vt)

        lambda_init = 0.8 - 0.6 * math.exp(-0.3 * l)
        o = _attention(q, k, vt, bias, lam_q1[l], lam_k1[l], lam_q2[l], lam_k2[l],
                       subln_g[l], lambda_init)

        w_gates = jnp.concatenate(
            [_block_diag(gate_a_w[l]), _block_diag(gate_i_w[l])], axis=1).astype(BF16)
        yr = _rglru(xr, gr, conv_w[l], conv_b[l], w_gates, gate_a_b[l], gate_i_b[l],
                    lru_L[l])

        w_out_c = w_out[l].astype(BF16).reshape(2, ATTN_WIDTH, D)
        x = _outproj(x, o, yr, g2, w_out_c)

        last = l == depth - 1
        x = _ffn(x, norm_g[l, 2], sh3, sc3, g3,
                 *_ffn_weights(ffn2_w1[l], ffn2_w3[l], ffn2_w2[l]),
                 final_g=final_g if last else None)
    return x
```

```python
import functools
import math

import jax
import jax.numpy as jnp
from jax import lax
from jax.experimental import pallas as pl
from jax.experimental.pallas import tpu as pltpu

F32 = jnp.float32
BF16 = jnp.bfloat16

ATTN_HEADS = 4
ATTN_QK_DIM = 64
ATTN_V_DIM = 2 * ATTN_QK_DIM
ATTN_WIDTH = ATTN_HEADS * ATTN_V_DIM
RNN_BLOCKS = 8
CONV_WIDTH = 4
RG_LRU_C = 8.0
N_BUCKETS = 32
MAX_DISTANCE = 128
NORM_EPS = 1e-6
N_MOD = 9

SUBLANES = 8
VMEM_LIMIT = 56 * 1024 * 1024

FFN_TM = 512
FFN_FC = 256
PROJ_TM = 1024
ATTN_T = 1024
ATTN_STRIP = 256
BIAS_T = 128
RNN_TC = 1024
NEG_BIG = -1e30
LOG2E = math.log2(math.e)


def _rms(x):
    return x * lax.rsqrt(jnp.mean(x * x, axis=-1, keepdims=True) + NORM_EPS)


def _const_spec(shape):
    nd = len(shape)
    return pl.BlockSpec(shape, lambda *_: (0,) * nd, pipeline_mode=pl.Buffered(1))


def _mod_kernel(c_ref, w_ref, b_ref, o_ref):
    c = c_ref[...]
    ca = c * jax.nn.sigmoid(c)
    o_ref[0] = jnp.dot(ca, w_ref[...], preferred_element_type=F32,
                       precision=lax.Precision.HIGHEST) + b_ref[0]


def _modulation(c_pad, ada_w, ada_b):
    rows, d = c_pad.shape
    n = ada_w.shape[1] // d
    return pl.pallas_call(
        _mod_kernel,
        grid=(n,),
        in_specs=[pl.BlockSpec((rows, d), lambda j: (0, 0)),
                  pl.BlockSpec((d, d), lambda j: (0, j)),
                  pl.BlockSpec((1, 1, d), lambda j: (j, 0, 0))],
        out_specs=pl.BlockSpec((1, rows, d), lambda j: (j, 0, 0)),
        out_shape=jax.ShapeDtypeStruct((n, rows, d), F32),
        compiler_params=pltpu.CompilerParams(vmem_limit_bytes=VMEM_LIMIT),
        name="adaln_mod",
    )(c_pad, ada_w, ada_b.reshape(n, 1, d))


def _ffn_kernel(x_ref, ng_ref, sh_ref, sc_ref, g_ref, w1_ref, w3_ref, w2_ref, *rest,
                final_norm):
    if final_norm:
        fg_ref, o_ref, acc_ref = rest
    else:
        o_ref, acc_ref = rest
    x = x_ref[0]
    h = _rms(x) * ng_ref[...] * (1.0 + sc_ref[0]) + sh_ref[0]
    hb = h.astype(BF16)
    n_chunks = w1_ref.shape[1] // FFN_FC
    cols = lambda ci: slice(ci * FFN_FC, (ci + 1) * FFN_FC)
    up = lambda ci: (jnp.dot(hb, w1_ref[:, cols(ci)], preferred_element_type=F32),
                     jnp.dot(hb, w3_ref[:, cols(ci)], preferred_element_type=F32))
    a, b = up(0)
    for ci in range(n_chunks):
        if ci + 1 < n_chunks:
            a_next, b_next = up(ci + 1)
        u = (a * jax.nn.sigmoid(a) * b).astype(BF16)
        down = jnp.dot(u, w2_ref[cols(ci), :], preferred_element_type=F32)
        if ci == 0:
            acc_ref[...] = down
        else:
            acc_ref[...] += down
        a, b = a_next, b_next
    y = x + 0.5 * g_ref[0] * acc_ref[...]
    if final_norm:
        y = _rms(y) * fg_ref[...]
    o_ref[0] = y


def _ffn(x, norm_g, sh, sc, g, w1c, w3c, w2c, final_g=None):
    B, S, D = x.shape
    tm = FFN_TM
    final_norm = final_g is not None
    vec = pl.BlockSpec((1, 1, D), lambda b, i: (b, 0, 0))
    in_specs = [pl.BlockSpec((1, tm, D), lambda b, i: (b, i, 0)),
                _const_spec((1, D)), vec, vec, vec,
                _const_spec(w1c.shape), _const_spec(w3c.shape), _const_spec(w2c.shape)]
    args = [x, norm_g.reshape(1, D), sh, sc, g, w1c, w3c, w2c]
    if final_norm:
        in_specs.append(_const_spec((1, D)))
        args.append(final_g.reshape(1, D))
    return pl.pallas_call(
        functools.partial(_ffn_kernel, final_norm=final_norm),
        grid=(B, S // tm),
        in_specs=in_specs,
        out_specs=pl.BlockSpec((1, tm, D), lambda b, i: (b, i, 0)),
        out_shape=jax.ShapeDtypeStruct((B, S, D), F32),
        scratch_shapes=[pltpu.VMEM((tm, D), F32)],
        compiler_params=pltpu.CompilerParams(
            dimension_semantics=("parallel", "parallel"), vmem_limit_bytes=VMEM_LIMIT),
        name="ffn_final" if final_norm else "ffn",
    )(*args)


def _inproj_kernel(x_ref, ng_ref, sh_ref, sc_ref, w_ref, wvt_ref,
                   q_ref, k_ref, vt_ref, xr_ref, gr_ref):
    x = x_ref[0]
    h = _rms(x) * ng_ref[...] * (1.0 + sc_ref[0]) + sh_ref[0]
    hb = h.astype(BF16)
    scale = ATTN_QK_DIM ** -0.5 * LOG2E
    W = q_ref.shape[-1]
    proj = lambda n: jnp.dot(hb, w_ref[:, n * W:(n + 1) * W], preferred_element_type=F32)
    q_ref[0] = (proj(0) * scale).astype(BF16)
    k_ref[0] = proj(1).astype(BF16)
    vt = lax.dot_general(wvt_ref[...], hb, (((1,), (1,)), ((), ())),
                         preferred_element_type=F32).astype(BF16)
    T = vt_ref.shape[-1]
    for ci in range(vt_ref.shape[1]):
        vt_ref[0, ci] = vt[:, ci * T:(ci + 1) * T]
    xr_ref[0] = proj(3)
    gr_ref[0] = proj(4)


def _inproj(x, norm_g, sh, sc, w_in_c, w_vt):
    B, S, D = x.shape
    tm, T = PROJ_TM, ATTN_T
    W = w_vt.shape[0]
    vec = pl.BlockSpec((1, 1, D), lambda b, i: (b, 0, 0))
    tok = pl.BlockSpec((1, tm, W), lambda b, i: (b, i, 0))
    vt_spec = pl.BlockSpec((1, tm // T, W, T), lambda b, i: (b, i, 0, 0))
    return pl.pallas_call(
        _inproj_kernel,
        grid=(B, S // tm),
        in_specs=[pl.BlockSpec((1, tm, D), lambda b, i: (b, i, 0)),
                  _const_spec((1, D)), vec, vec, _const_spec(w_in_c.shape),
                  _const_spec(w_vt.shape)],
        out_specs=[tok, tok, vt_spec, tok, tok],
        out_shape=[jax.ShapeDtypeStruct((B, S, W), BF16)] * 2
                  + [jax.ShapeDtypeStruct((B, S // T, W, T), BF16)]
                  + [jax.ShapeDtypeStruct((B, S, W), F32)] * 2,
        compiler_params=pltpu.CompilerParams(
            dimension_semantics=("parallel", "parallel"), vmem_limit_bytes=VMEM_LIMIT),
        name="in_proj",
    )(x, norm_g.reshape(1, D), sh, sc, w_in_c, w_vt)


def _bias_kernel(rb_ref, o_ref):
    mp = pl.program_id(0)
    T = o_ref.shape[-1]
    key = lax.broadcasted_iota(jnp.int32, (T, T), 0)
    qry = lax.broadcasted_iota(jnp.int32, (T, T), 1)
    max_exact = N_BUCKETS // 2
    last = rb_ref[N_BUCKETS - 1, mp]
    for which in range(2):
        rel = qry - key + (T if which == 0 else 0)
        n = jnp.maximum(rel, 0)
        nf = jnp.maximum(n, 1).astype(F32)
        large = max_exact + (jnp.log(nf / max_exact) / math.log(MAX_DISTANCE / max_exact)
                             * (N_BUCKETS - max_exact)).astype(jnp.int32)
        large = jnp.minimum(large, N_BUCKETS - 1)
        bucket = jnp.where(n < max_exact, n, large)
        val = jnp.zeros((T, T), F32)
        for bk in range(N_BUCKETS - 1):
            val = jnp.where(bucket == bk, (rb_ref[bk, mp] - last) * LOG2E, val)
        if which == 1:
            val = jnp.where(rel >= 0, val, NEG_BIG)
        o_ref[0, which] = val


def _bias_tiles(rel_bias, T):
    n_maps = rel_bias.shape[1]
    return pl.pallas_call(
        _bias_kernel,
        grid=(n_maps,),
        in_specs=[pl.BlockSpec(memory_space=pltpu.SMEM)],
        out_specs=pl.BlockSpec((1, 2, T, T), lambda m: (m, 0, 0, 0)),
        out_shape=jax.ShapeDtypeStruct((n_maps, 2, T, T), F32),
        name="rel_bias_tiles",
    )(rel_bias)


def _attn_kernel(q_ref, k_ref, vt_ref, bias_ref, lq1_ref, lk1_ref, lq2_ref, lk2_ref,
                 sg_ref, o_ref, qst_ref, m_ref, l_ref, acc_ref, s_ref, *, lambda_init):
    T = q_ref.shape[1]
    dqk = ATTN_QK_DIM
    qi = pl.program_id(2)

    qt = q_ref[0].astype(F32).T
    chan = lax.broadcasted_iota(jnp.int32, qt.shape, 0)
    qst_ref[:, :T] = jnp.where(chan < dqk, qt, 0.0).astype(BF16)
    qst_ref[:, T:] = jnp.where(chan >= dqk, qt, 0.0).astype(BF16)
    m_ref[...] = jnp.full_like(m_ref, NEG_BIG)
    l_ref[...] = jnp.zeros_like(l_ref)
    acc_ref[...] = jnp.zeros_like(acc_ref)

    strips = [slice(c * ATTN_STRIP, (c + 1) * ATTN_STRIP)
              for c in range(2 * T // ATTN_STRIP)]

    def put_scores(buf, j, c):
        start = pl.multiple_of(j * T, T)
        kj = k_ref[0, pl.ds(start, T), :]
        s_ref[buf, :, strips[c]] = jnp.dot(kj, qst_ref[:, strips[c]],
                                           preferred_element_type=F32)

    def biased(st, which, mp, q0):
        n_key = T // BIAS_T
        rows = []
        for a in range(n_key):
            cols = []
            for b in range(q0 // BIAS_T, (q0 + ATTN_STRIP) // BIAS_T):
                lo = (b - q0 // BIAS_T) * BIAS_T
                piece = st[a * BIAS_T:(a + 1) * BIAS_T, lo:lo + BIAS_T]
                dist = b - a + (n_key if which == 0 else 0)
                if dist < 0:
                    piece = jnp.full_like(piece, NEG_BIG)
                elif dist <= 1:
                    piece = piece + bias_ref[0, mp, 1 - dist]
                cols.append(piece)
            rows.append(jnp.concatenate(cols, axis=1))
        return jnp.concatenate(rows, axis=0)

    def step(buf, j, which, j_next):
        vt = vt_ref[0, j]
        if j_next is not None:
            put_scores(1 - buf, j_next, 0)
        for c, sl in enumerate(strips):
            if j_next is not None and c + 1 < len(strips):
                put_scores(1 - buf, j_next, c + 1)
            st = s_ref[buf, :, sl]
            if which is not None:
                st = biased(st, which, *divmod(sl.start, T))
            m_old = m_ref[:, sl]
            m_new = jnp.maximum(m_old, jnp.max(st, axis=0, keepdims=True))
            alpha = jnp.exp2(m_old - m_new)
            p = jnp.exp2(st - m_new)
            l_ref[:, sl] = alpha * l_ref[:, sl] + jnp.sum(p, axis=0, keepdims=True)
            acc_ref[:, sl] = alpha * acc_ref[:, sl] + jnp.dot(
                vt, p.astype(BF16), preferred_element_type=F32)
            m_ref[:, sl] = m_new

    def first_scores(j):
        for c in range(len(strips)):
            put_scores(0, j, c)

    @pl.when(qi >= 1)
    def _():
        n_far = qi - 1
        odd = n_far % 2

        @pl.when(odd == 1)
        def _():
            first_scores(0)
            step(0, 0, None, None)

        first_scores(odd)

        def pair(i, carry):
            j = odd + 2 * i
            step(0, j, None, j + 1)
            step(1, j + 1, None, j + 2)
            return carry

        lax.fori_loop(0, n_far // 2, pair, 0)
        step(0, qi - 1, 0, qi)
        step(1, qi, 1, None)

    @pl.when(qi == 0)
    def _():
        first_scores(0)
        step(0, 0, 1, None)

    lam = (jnp.exp(jnp.sum(lq1_ref[...] * lk1_ref[...], axis=-1, keepdims=True))
           - jnp.exp(jnp.sum(lq2_ref[...] * lk2_ref[...], axis=-1, keepdims=True))
           + lambda_init)
    o_all = acc_ref[...] / l_ref[...]
    ot = o_all[:, :T] - lam * o_all[:, T:]
    ot = ot * lax.rsqrt(jnp.mean(ot * ot, axis=0, keepdims=True) + NORM_EPS)
    o_ref[0] = (ot.T * sg_ref[...] * (1.0 - lambda_init)).astype(o_ref.dtype)


def _attention(q, k, vt, bias, lq1, lk1, lq2, lk2, subln_g, lambda_init):
    B, S, _ = q.shape
    T = ATTN_T
    H, dv, dqk = ATTN_HEADS, ATTN_V_DIM, ATTN_QK_DIM
    bias = bias.reshape(H, 2, 2, BIAS_T, BIAS_T)
    small = lambda n: _const_spec((1, n))
    return pl.pallas_call(
        functools.partial(_attn_kernel, lambda_init=lambda_init),
        grid=(B, H, S // T),
        in_specs=[pl.BlockSpec((1, T, 2 * dqk), lambda b, h, i: (b, i, h)),
                  pl.BlockSpec((1, S, 2 * dqk), lambda b, h, i: (b, 0, h)),
                  pl.BlockSpec((1, S // T, dv, T), lambda b, h, i: (b, 0, h, 0)),
                  pl.BlockSpec((1, 2, 2, BIAS_T, BIAS_T), lambda b, h, i: (h, 0, 0, 0, 0)),
                  small(dqk), small(dqk), small(dqk), small(dqk), small(dv)],
        out_specs=pl.BlockSpec((1, T, dv), lambda b, h, i: (b, i, h)),
        out_shape=jax.ShapeDtypeStruct((B, S, H * dv), BF16),
        scratch_shapes=[pltpu.VMEM((2 * dqk, 2 * T), BF16),
                        pltpu.VMEM((1, 2 * T), F32),
                        pltpu.VMEM((1, 2 * T), F32),
                        pltpu.VMEM((dv, 2 * T), F32),
                        pltpu.VMEM((2, T, 2 * T), F32)],
        compiler_params=pltpu.CompilerParams(
            dimension_semantics=("parallel", "parallel", "arbitrary"),
            vmem_limit_bytes=VMEM_LIMIT),
        name="diff_attn",
    )(q, k, vt, bias, lq1.reshape(1, dqk), lk1.reshape(1, dqk), lq2.reshape(1, dqk),
      lk2.reshape(1, dqk), subln_g.reshape(1, dv))


def _rglru_kernel(xr_ref, gr_ref, cw_ref, cb_ref, wg_ref, ba_ref, bi_ref, L_ref,
                  y_ref, tail_ref, h_ref, a_buf, b_buf):
    Tc, C = xr_ref.shape[1], xr_ref.shape[2]
    G = Tc // SUBLANES

    @pl.when(pl.program_id(1) == 0)
    def _():
        tail_ref[...] = jnp.zeros_like(tail_ref)
        h_ref[...] = jnp.zeros_like(h_ref)

    x = xr_ref[0]
    tail = tail_ref[...]
    sub = lax.broadcasted_iota(jnp.int32, (SUBLANES, C), 0)
    xc = x * cw_ref[CONV_WIDTH - 1:CONV_WIDTH, :] + cb_ref[...]
    for k in range(1, CONV_WIDTH):
        xs = pltpu.roll(x, k, 0)
        head = jnp.where(sub < k, pltpu.roll(tail, k, 0), xs[:SUBLANES])
        xs = jnp.concatenate([head, xs[SUBLANES:]], axis=0)
        xc = xc + xs * cw_ref[CONV_WIDTH - 1 - k:CONV_WIDTH - k, :]
    tail_ref[...] = x[Tc - SUBLANES:]

    gates = jnp.dot(xc.astype(BF16), wg_ref[...], preferred_element_type=F32)
    r = jax.nn.sigmoid(gates[:, :C] + ba_ref[...])
    ig = jax.nn.sigmoid(gates[:, C:] + bi_ref[...])
    L = L_ref[...]
    log_sig = jnp.minimum(L, 0.0) - jnp.log1p(jnp.exp(-jnp.abs(L)))
    log_a = RG_LRU_C * r * log_sig
    a = jnp.exp(log_a)
    u = jnp.sqrt(-jnp.tanh(log_a) * (a * a + 1.0)) * (ig * xc)

    a3 = a.reshape(G, SUBLANES, C)
    u3 = u.reshape(G, SUBLANES, C)
    sub3 = lax.broadcasted_iota(jnp.int32, (G, SUBLANES, C), 1)
    for k in (1, 2, 4):
        a_sh = pltpu.roll(a3, k, 1)
        u_sh = pltpu.roll(u3, k, 1)
        ok = sub3 >= k
        u3 = jnp.where(ok, a3 * u_sh + u3, u3)
        a3 = jnp.where(ok, a3 * a_sh, a3)
    a_buf[...] = a3
    b_buf[...] = u3

    def body(g, hprev):
        hg = a_buf[g] * hprev + b_buf[g]
        b_buf[g] = hg
        return hg[SUBLANES - 1:SUBLANES, :]

    h_last = lax.fori_loop(0, G, body, h_ref[...], unroll=8)
    h_ref[...] = h_last

    h = b_buf[...].reshape(Tc, C)
    gr = gr_ref[0]
    gelu = 0.5 * gr * (1.0 + jnp.tanh(
        math.sqrt(2.0 / math.pi) * (gr + 0.044715 * (gr * gr * gr))))
    y_ref[0] = (h * gelu).astype(y_ref.dtype)


def _rglru(xr, gr, conv_w, conv_b, w_gates, b_a, b_i, lru_L):
    B, S, C = xr.shape
    Tc = RNN_TC
    tok = pl.BlockSpec((1, Tc, C), lambda b, t: (b, t, 0))
    row = lambda: _const_spec((1, C))
    return pl.pallas_call(
        _rglru_kernel,
        grid=(B, S // Tc),
        in_specs=[tok, tok, _const_spec((CONV_WIDTH, C)), row(),
                  _const_spec(w_gates.shape), row(), row(), row()],
        out_specs=tok,
        out_shape=jax.ShapeDtypeStruct((B, S, C), BF16),
        scratch_shapes=[pltpu.VMEM((SUBLANES, C), F32),
                        pltpu.VMEM((1, C), F32),
                        pltpu.VMEM((Tc // SUBLANES, SUBLANES, C), F32),
                        pltpu.VMEM((Tc // SUBLANES, SUBLANES, C), F32)],
        compiler_params=pltpu.CompilerParams(
            dimension_semantics=("parallel", "arbitrary"), vmem_limit_bytes=VMEM_LIMIT),
        name="rg_lru",
    )(xr, gr, conv_w, conv_b.reshape(1, C), w_gates, b_a.reshape(1, C),
      b_i.reshape(1, C), lru_L.reshape(1, C))


def _outproj_kernel(x_ref, o_ref, y_ref, g_ref, w_ref, out_ref):
    mix = (jnp.dot(o_ref[0], w_ref[0], preferred_element_type=F32)
           + jnp.dot(y_ref[0], w_ref[1], preferred_element_type=F32))
    out_ref[0] = x_ref[0] + g_ref[0] * mix


def _outproj(x, o, yr, g, w_out_c):
    B, S, D = x.shape
    tm = PROJ_TM
    W = o.shape[-1]
    half = pl.BlockSpec((1, tm, W), lambda b, i: (b, i, 0))
    full = pl.BlockSpec((1, tm, D), lambda b, i: (b, i, 0))
    return pl.pallas_call(
        _outproj_kernel,
        grid=(B, S // tm),
        in_specs=[full, half, half, pl.BlockSpec((1, 1, D), lambda b, i: (b, 0, 0)),
                  _const_spec(w_out_c.shape)],
        out_specs=full,
        out_shape=jax.ShapeDtypeStruct((B, S, D), F32),
        compiler_params=pltpu.CompilerParams(
            dimension_semantics=("parallel", "parallel"), vmem_limit_bytes=VMEM_LIMIT),
        name="out_proj",
    )(x, o, yr, g, w_out_c)


def _ffn_weights(w1, w3, w2):
    return w1.astype(BF16), w3.astype(BF16), w2.astype(BF16)


def _block_diag(w):
    n, bw, _ = w.shape
    eye = jnp.eye(n, dtype=w.dtype)
    return (eye[:, None, :, None] * w[:, :, None, :]).reshape(n * bw, n * bw)


def kernel(x, c, rel_bias, ada_w, ada_b, norm_g, ffn1_w1, ffn1_w3, ffn1_w2, w_in, lam_q1, lam_k1, lam_q2, lam_k2, subln_g, conv_w, conv_b, gate_a_w, gate_a_b, gate_i_w, gate_i_b, lru_L, w_out, ffn2_w1, ffn2_w3, ffn2_w2, final_g):
    B, S, D = x.shape
    depth = ada_w.shape[0]
    c_pad = jnp.zeros((SUBLANES, D), F32).at[:B].set(c)
    bias = _bias_tiles(rel_bias, BIAS_T)
    for l in range(depth):
        mod = _modulation(c_pad, ada_w[l], ada_b[l])[:, :B]
        sh1, sc1, g1, sh2, sc2, g2, sh3, sc3, g3 = [m.reshape(B, 1, D) for m in mod]

        x = _ffn(x, norm_g[l, 0], sh1, sc1, g1,
                 *_ffn_weights(ffn1_w1[l], ffn1_w3[l], ffn1_w2[l]))

        w_in_b = w_in[l].astype(BF16)
        w_vt = w_in_b[:, 2 * ATTN_WIDTH:3 * ATTN_WIDTH].T
        q, k, vt, xr, gr = _inproj(x, norm_g[l, 1], sh2, sc2, w_in_b, w_vt)

        lambda_init = 0.8 - 0.6 * math.exp(-0.3 * l)
        o = _attention(q, k, vt, bias, lam_q1[l], lam_k1[l], lam_q2[l], lam_k2[l],
                       subln_g[l], lambda_init)

        w_gates = jnp.concatenate(
            [_block_diag(gate_a_w[l]), _block_diag(gate_i_w[l])], axis=1).astype(BF16)
        yr = _rglru(xr, gr, conv_w[l], conv_b[l], w_gates, gate_a_b[l], gate_i_b[l],
                    lru_L[l])

        w_out_c = w_out[l].astype(BF16).reshape(2, ATTN_WIDTH, D)
        x = _outproj(x, o, yr, g2, w_out_c)

        last = l == depth - 1
        x = _ffn(x, norm_g[l, 2], sh3, sc3, g3,
                 *_ffn_weights(ffn2_w1[l], ffn2_w3[l], ffn2_w2[l]),
                 final_g=final_g if last else None)
    return x
```

```python
import functools
import math

import jax
import jax.numpy as jnp
from jax import lax
from jax.experimental import pallas as pl
from jax.experimental.pallas import tpu as pltpu

F32 = jnp.float32
BF16 = jnp.bfloat16

ATTN_HEADS = 4
ATTN_QK_DIM = 64
ATTN_V_DIM = 2 * ATTN_QK_DIM
ATTN_WIDTH = ATTN_HEADS * ATTN_V_DIM
RNN_BLOCKS = 8
CONV_WIDTH = 4
RG_LRU_C = 8.0
N_BUCKETS = 32
MAX_DISTANCE = 128
NORM_EPS = 1e-6
N_MOD = 9

SUBLANES = 8
VMEM_LIMIT = 56 * 1024 * 1024

FFN_TM = 512
FFN_FC = 256
PROJ_TM = 1024
ATTN_T = 1024
ATTN_STRIP = 256
BIAS_T = 128
RNN_TC = 1024
NEG_BIG = -1e30
LOG2E = math.log2(math.e)


def _rms(x):
    return x * lax.rsqrt(jnp.mean(x * x, axis=-1, keepdims=True) + NORM_EPS)


def _const_spec(shape):
    nd = len(shape)
    return pl.BlockSpec(shape, lambda *_: (0,) * nd, pipeline_mode=pl.Buffered(1))


def _mod_kernel(c_ref, w_ref, b_ref, o_ref):
    c = c_ref[...]
    ca = c * jax.nn.sigmoid(c)
    o_ref[0] = jnp.dot(ca, w_ref[...], preferred_element_type=F32,
                       precision=lax.Precision.HIGHEST) + b_ref[0]


def _modulation(c_pad, ada_w, ada_b):
    rows, d = c_pad.shape
    n = ada_w.shape[1] // d
    return pl.pallas_call(
        _mod_kernel,
        grid=(n,),
        in_specs=[pl.BlockSpec((rows, d), lambda j: (0, 0)),
                  pl.BlockSpec((d, d), lambda j: (0, j)),
                  pl.BlockSpec((1, 1, d), lambda j: (j, 0, 0))],
        out_specs=pl.BlockSpec((1, rows, d), lambda j: (j, 0, 0)),
        out_shape=jax.ShapeDtypeStruct((n, rows, d), F32),
        compiler_params=pltpu.CompilerParams(vmem_limit_bytes=VMEM_LIMIT),
        name="adaln_mod",
    )(c_pad, ada_w, ada_b.reshape(n, 1, d))


def _ffn_kernel(x_ref, ng_ref, sh_ref, sc_ref, g_ref, w1_ref, w3_ref, w2_ref, *rest,
                final_norm):
    if final_norm:
        fg_ref, o_ref, acc_ref = rest
    else:
        o_ref, acc_ref = rest
    x = x_ref[0]
    h = _rms(x) * ng_ref[...] * (1.0 + sc_ref[0]) + sh_ref[0]
    hb = h.astype(BF16)
    n_chunks = w1_ref.shape[1] // FFN_FC
    cols = lambda ci: slice(ci * FFN_FC, (ci + 1) * FFN_FC)
    up = lambda ci: (jnp.dot(hb, w1_ref[:, cols(ci)], preferred_element_type=F32),
                     jnp.dot(hb, w3_ref[:, cols(ci)], preferred_element_type=F32))
    a, b = up(0)
    for ci in range(n_chunks):
        if ci + 1 < n_chunks:
            a_next, b_next = up(ci + 1)
        u = (a * jax.nn.sigmoid(a) * b).astype(BF16)
        down = jnp.dot(u, w2_ref[cols(ci), :], preferred_element_type=F32)
        if ci == 0:
            acc_ref[...] = down
        else:
            acc_ref[...] += down
        a, b = a_next, b_next
    y = x + 0.5 * g_ref[0] * acc_ref[...]
    if final_norm:
        y = _rms(y) * fg_ref[...]
    o_ref[0] = y


def _ffn(x, norm_g, sh, sc, g, w1c, w3c, w2c, final_g=None):
    B, S, D = x.shape
    tm = FFN_TM
    final_norm = final_g is not None
    vec = pl.BlockSpec((1, 1, D), lambda b, i: (b, 0, 0))
    in_specs = [pl.BlockSpec((1, tm, D), lambda b, i: (b, i, 0)),
                _const_spec((1, D)), vec, vec, vec,
                _const_spec(w1c.shape), _const_spec(w3c.shape), _const_spec(w2c.shape)]
    args = [x, norm_g.reshape(1, D), sh, sc, g, w1c, w3c, w2c]
    if final_norm:
        in_specs.append(_const_spec((1, D)))
        args.append(final_g.reshape(1, D))
    return pl.pallas_call(
        functools.partial(_ffn_kernel, final_norm=final_norm),
        grid=(B, S // tm),
        in_specs=in_specs,
        out_specs=pl.BlockSpec((1, tm, D), lambda b, i: (b, i, 0)),
        out_shape=jax.ShapeDtypeStruct((B, S, D), F32),
        scratch_shapes=[pltpu.VMEM((tm, D), F32)],
        compiler_params=pltpu.CompilerParams(
            dimension_semantics=("parallel", "parallel"), vmem_limit_bytes=VMEM_LIMIT),
        name="ffn_final" if final_norm else "ffn",
    )(*args)


def _inproj_kernel(x_ref, ng_ref, sh_ref, sc_ref, w_ref, wvt_ref,
                   q_ref, k_ref, vt_ref, xr_ref, gr_ref):
    x = x_ref[0]
    h = _rms(x) * ng_ref[...] * (1.0 + sc_ref[0]) + sh_ref[0]
    hb = h.astype(BF16)
    scale = ATTN_QK_DIM ** -0.5 * LOG2E
    W = q_ref.shape[-1]
    proj = lambda n: jnp.dot(hb, w_ref[:, n * W:(n + 1) * W], preferred_element_type=F32)
    q_ref[0] = (proj(0) * scale).astype(BF16)
    k_ref[0] = proj(1).astype(BF16)
    vt = lax.dot_general(wvt_ref[...], hb, (((1,), (1,)), ((), ())),
                         preferred_element_type=F32).astype(BF16)
    T = vt_ref.shape[-1]
    for ci in range(vt_ref.shape[1]):
        vt_ref[0, ci] = vt[:, ci * T:(ci + 1) * T]
    xr_ref[0] = proj(3)
    gr_ref[0] = proj(4)


def _inproj(x, norm_g, sh, sc, w_in_c, w_vt):
    B, S, D = x.shape
    tm, T = PROJ_TM, ATTN_T
    W = w_vt.shape[0]
    vec = pl.BlockSpec((1, 1, D), lambda b, i: (b, 0, 0))
    tok = pl.BlockSpec((1, tm, W), lambda b, i: (b, i, 0))
    vt_spec = pl.BlockSpec((1, tm // T, W, T), lambda b, i: (b, i, 0, 0))
    return pl.pallas_call(
        _inproj_kernel,
        grid=(B, S // tm),
        in_specs=[pl.BlockSpec((1, tm, D), lambda b, i: (b, i, 0)),
                  _const_spec((1, D)), vec, vec, _const_spec(w_in_c.shape),
                  _const_spec(w_vt.shape)],
        out_specs=[tok, tok, vt_spec, tok, tok],
        out_shape=[jax.ShapeDtypeStruct((B, S, W), BF16)] * 2
                  + [jax.ShapeDtypeStruct((B, S // T, W, T), BF16)]
                  + [jax.ShapeDtypeStruct((B, S, W), F32)] * 2,
        compiler_params=pltpu.CompilerParams(
            dimension_semantics=("parallel", "parallel"), vmem_limit_bytes=VMEM_LIMIT),
        name="in_proj",
    )(x, norm_g.reshape(1, D), sh, sc, w_in_c, w_vt)


def _bias_kernel(rb_ref, o_ref):
    mp = pl.program_id(0)
    T = o_ref.shape[-1]
    key = lax.broadcasted_iota(jnp.int32, (T, T), 0)
    qry = lax.broadcasted_iota(jnp.int32, (T, T), 1)
    max_exact = N_BUCKETS // 2
    last = rb_ref[N_BUCKETS - 1, mp]
    for which in range(2):
        rel = qry - key + (T if which == 0 else 0)
        n = jnp.maximum(rel, 0)
        nf = jnp.maximum(n, 1).astype(F32)
        large = max_exact + (jnp.log(nf / max_exact) / math.log(MAX_DISTANCE / max_exact)
                             * (N_BUCKETS - max_exact)).astype(jnp.int32)
        large = jnp.minimum(large, N_BUCKETS - 1)
        bucket = jnp.where(n < max_exact, n, large)
        val = jnp.zeros((T, T), F32)
        for bk in range(N_BUCKETS - 1):
            val = jnp.where(bucket == bk, (rb_ref[bk, mp] - last) * LOG2E, val)
        if which == 1:
            val = jnp.where(rel >= 0, val, NEG_BIG)
        o_ref[0, which] = val


def _bias_tiles(rel_bias, T):
    n_maps = rel_bias.shape[1]
    return pl.pallas_call(
        _bias_kernel,
        grid=(n_maps,),
        in_specs=[pl.BlockSpec(memory_space=pltpu.SMEM)],
        out_specs=pl.BlockSpec((1, 2, T, T), lambda m: (m, 0, 0, 0)),
        out_shape=jax.ShapeDtypeStruct((n_maps, 2, T, T), F32),
        name="rel_bias_tiles",
    )(rel_bias)


def _attn_kernel(q_ref, qn_ref, k_ref, vt_ref, bias_ref, lq1_ref, lk1_ref, lq2_ref, lk2_ref,
                 sg_ref, o_ref, qst_ref, m_ref, l_ref, acc_ref, s_ref, *, lambda_init):
    T = q_ref.shape[1]
    dqk = ATTN_QK_DIM
    qi = pl.program_id(2)

    for slot, ref in enumerate((q_ref, qn_ref)):
        qt = ref[0].astype(F32).T
        chan = lax.broadcasted_iota(jnp.int32, qt.shape, 0)
        qst_ref[slot, :, :T] = jnp.where(chan < dqk, qt, 0.0).astype(BF16)
        qst_ref[slot, :, T:] = jnp.where(chan >= dqk, qt, 0.0).astype(BF16)
    m_ref[...] = jnp.full_like(m_ref, NEG_BIG)
    l_ref[...] = jnp.zeros_like(l_ref)
    acc_ref[...] = jnp.zeros_like(acc_ref)

    strips = [slice(c * ATTN_STRIP, (c + 1) * ATTN_STRIP)
              for c in range(2 * T // ATTN_STRIP)]

    def n_keys(which, c):
        return strips[c].start % T + ATTN_STRIP if which == 1 else T

    def put_scores(buf, j, which, c, slot=0):
        nk = n_keys(which, c)
        start = j * T if isinstance(j, int) else pl.multiple_of(j * T, T)
        kj = k_ref[0, pl.ds(start, nk), :]
        s_ref[buf, :nk, strips[c]] = jnp.dot(kj, qst_ref[slot, :, strips[c]],
                                             preferred_element_type=F32)

    def biased(st, which, mp, q0):
        rows = []
        for a in range(st.shape[0] // BIAS_T):
            cols = []
            for b in range(q0 // BIAS_T, (q0 + ATTN_STRIP) // BIAS_T):
                lo = (b - q0 // BIAS_T) * BIAS_T
                piece = st[a * BIAS_T:(a + 1) * BIAS_T, lo:lo + BIAS_T]
                dist = b - a + (T // BIAS_T if which == 0 else 0)
                if dist < 0:
                    piece = jnp.full_like(piece, NEG_BIG)
                elif dist <= 1:
                    piece = piece + bias_ref[0, mp, 1 - dist]
                cols.append(piece)
            rows.append(jnp.concatenate(cols, axis=1))
        return jnp.concatenate(rows, axis=0)

    def step(buf, j, which, nxt):
        put_scores(*nxt[:3], 0, nxt[3])
        for c, sl in enumerate(strips):
            if c + 1 < len(strips):
                put_scores(*nxt[:3], c + 1, nxt[3])
            nk = n_keys(which, c)
            st = s_ref[buf, :nk, sl]
            if which is not None:
                st = biased(st, which, *divmod(sl.start, T))
            m_old = m_ref[:, sl]
            m_new = jnp.maximum(m_old, jnp.max(st, axis=0, keepdims=True))
            alpha = jnp.exp2(m_old - m_new)
            p = jnp.exp2(st - m_new)
            l_ref[:, sl] = alpha * l_ref[:, sl] + jnp.sum(p, axis=0, keepdims=True)
            acc_ref[:, sl] = alpha * acc_ref[:, sl] + jnp.dot(
                vt_ref[0, j, :, :nk], p.astype(BF16), preferred_element_type=F32)
            m_ref[:, sl] = m_new

    next_q = (0, 0, None, 1)

    @pl.when(qi == 0)
    def _():
        for c in range(len(strips)):
            put_scores(2, 0, 1, c)
        step(2, 0, 1, next_q)

    @pl.when(qi >= 1)
    def _():
        n_far = qi - 1

        def pair(i, carry):
            j = 2 * i
            step(0, j, None, (1, j + 1, None, 0))
            step(1, j + 1, None, (0, j + 2, None, 0))
            return carry

        lax.fori_loop(0, n_far // 2, pair, 0)

        @pl.when(n_far % 2 == 0)
        def _():
            step(0, qi - 1, 0, (1, qi, 1, 0))
            step(1, qi, 1, next_q)

        @pl.when(n_far % 2 == 1)
        def _():
            step(0, qi - 2, None, (1, qi - 1, None, 0))
            step(1, qi - 1, 0, (2, qi, 1, 0))
            step(2, qi, 1, next_q)

    lam = (jnp.exp(jnp.sum(lq1_ref[...] * lk1_ref[...], axis=-1, keepdims=True))
           - jnp.exp(jnp.sum(lq2_ref[...] * lk2_ref[...], axis=-1, keepdims=True))
           + lambda_init)
    o_all = acc_ref[...] / l_ref[...]
    ot = o_all[:, :T] - lam * o_all[:, T:]
    ot = ot * lax.rsqrt(jnp.mean(ot * ot, axis=0, keepdims=True) + NORM_EPS)
    o_ref[0] = (ot.T * sg_ref[...] * (1.0 - lambda_init)).astype(o_ref.dtype)


def _attention(q, k, vt, bias, lq1, lk1, lq2, lk2, subln_g, lambda_init):
    B, S, _ = q.shape
    T = ATTN_T
    H, dv, dqk = ATTN_HEADS, ATTN_V_DIM, ATTN_QK_DIM
    bias = bias.reshape(H, 2, 2, BIAS_T, BIAS_T)
    small = lambda n: _const_spec((1, n))
    return pl.pallas_call(
        functools.partial(_attn_kernel, lambda_init=lambda_init),
        grid=(B, H, S // T),
        in_specs=[pl.BlockSpec((1, T, 2 * dqk), lambda b, h, i: (b, i, h)),
                  pl.BlockSpec((1, T, 2 * dqk),
                               lambda b, h, i: (b, jnp.minimum(i + 1, S // T - 1), h)),
                  pl.BlockSpec((1, S, 2 * dqk), lambda b, h, i: (b, 0, h)),
                  pl.BlockSpec((1, S // T, dv, T), lambda b, h, i: (b, 0, h, 0)),
                  pl.BlockSpec((1, 2, 2, BIAS_T, BIAS_T), lambda b, h, i: (h, 0, 0, 0, 0)),
                  small(dqk), small(dqk), small(dqk), small(dqk), small(dv)],
        out_specs=pl.BlockSpec((1, T, dv), lambda b, h, i: (b, i, h)),
        out_shape=jax.ShapeDtypeStruct((B, S, H * dv), BF16),
        scratch_shapes=[pltpu.VMEM((2, 2 * dqk, 2 * T), BF16),
                        pltpu.VMEM((1, 2 * T), F32),
                        pltpu.VMEM((1, 2 * T), F32),
                        pltpu.VMEM((dv, 2 * T), F32),
                        pltpu.VMEM((3, T, 2 * T), F32)],
        compiler_params=pltpu.CompilerParams(
            dimension_semantics=("parallel", "parallel", "arbitrary"),
            vmem_limit_bytes=VMEM_LIMIT),
        name="diff_attn",
    )(q, q, k, vt, bias, lq1.reshape(1, dqk), lk1.reshape(1, dqk), lq2.reshape(1, dqk),
      lk2.reshape(1, dqk), subln_g.reshape(1, dv))


def _rglru_kernel(xr_ref, gr_ref, cw_ref, cb_ref, wg_ref, ba_ref, bi_ref, L_ref,
                  y_ref, tail_ref, h_ref, a_buf, b_buf):
    Tc, C = xr_ref.shape[1], xr_ref.shape[2]
    G = Tc // SUBLANES

    @pl.when(pl.program_id(1) == 0)
    def _():
        tail_ref[...] = jnp.zeros_like(tail_ref)
        h_ref[...] = jnp.zeros_like(h_ref)

    x = xr_ref[0]
    tail = tail_ref[...]
    sub = lax.broadcasted_iota(jnp.int32, (SUBLANES, C), 0)
    xc = x * cw_ref[CONV_WIDTH - 1:CONV_WIDTH, :] + cb_ref[...]
    for k in range(1, CONV_WIDTH):
        xs = pltpu.roll(x, k, 0)
        head = jnp.where(sub < k, pltpu.roll(tail, k, 0), xs[:SUBLANES])
        xs = jnp.concatenate([head, xs[SUBLANES:]], axis=0)
        xc = xc + xs * cw_ref[CONV_WIDTH - 1 - k:CONV_WIDTH - k, :]
    tail_ref[...] = x[Tc - SUBLANES:]

    gates = jnp.dot(xc.astype(BF16), wg_ref[...], preferred_element_type=F32)
    r = jax.nn.sigmoid(gates[:, :C] + ba_ref[...])
    ig = jax.nn.sigmoid(gates[:, C:] + bi_ref[...])
    L = L_ref[...]
    log_sig = jnp.minimum(L, 0.0) - jnp.log1p(jnp.exp(-jnp.abs(L)))
    log_a = RG_LRU_C * r * log_sig
    a = jnp.exp(log_a)
    v = -jnp.tanh(log_a) * (a * a + 1.0)
    root = jnp.where(v > 0.0, v * lax.rsqrt(v), 0.0)
    u = root * (ig * xc)

    a3 = a.reshape(G, SUBLANES, C)
    u3 = u.reshape(G, SUBLANES, C)
    sub3 = lax.broadcasted_iota(jnp.int32, (G, SUBLANES, C), 1)
    for k in (1, 2, 4):
        a_sh = pltpu.roll(a3, k, 1)
        u_sh = pltpu.roll(u3, k, 1)
        ok = sub3 >= k
        u3 = jnp.where(ok, a3 * u_sh + u3, u3)
        a3 = jnp.where(ok, a3 * a_sh, a3)
    a_buf[...] = a3
    b_buf[...] = u3

    def body(g, hprev):
        hg = a_buf[g] * hprev + b_buf[g]
        b_buf[g] = hg
        return hg[SUBLANES - 1:SUBLANES, :]

    h_last = lax.fori_loop(0, G, body, h_ref[...], unroll=8)
    h_ref[...] = h_last

    h = b_buf[...].reshape(Tc, C)
    gr = gr_ref[0]
    gelu = 0.5 * gr * (1.0 + jnp.tanh(
        math.sqrt(2.0 / math.pi) * (gr + 0.044715 * (gr * gr * gr))))
    y_ref[0] = (h * gelu).astype(y_ref.dtype)


def _rglru(xr, gr, conv_w, conv_b, w_gates, b_a, b_i, lru_L):
    B, S, C = xr.shape
    Tc = RNN_TC
    tok = pl.BlockSpec((1, Tc, C), lambda b, t: (b, t, 0))
    row = lambda: _const_spec((1, C))
    return pl.pallas_call(
        _rglru_kernel,
        grid=(B, S // Tc),
        in_specs=[tok, tok, _const_spec((CONV_WIDTH, C)), row(),
                  _const_spec(w_gates.shape), row(), row(), row()],
        out_specs=tok,
        out_shape=jax.ShapeDtypeStruct((B, S, C), BF16),
        scratch_shapes=[pltpu.VMEM((SUBLANES, C), F32),
                        pltpu.VMEM((1, C), F32),
                        pltpu.VMEM((Tc // SUBLANES, SUBLANES, C), F32),
                        pltpu.VMEM((Tc // SUBLANES, SUBLANES, C), F32)],
        compiler_params=pltpu.CompilerParams(
            dimension_semantics=("parallel", "arbitrary"), vmem_limit_bytes=VMEM_LIMIT),
        name="rg_lru",
    )(xr, gr, conv_w, conv_b.reshape(1, C), w_gates, b_a.reshape(1, C),
      b_i.reshape(1, C), lru_L.reshape(1, C))


def _outproj_kernel(x_ref, o_ref, y_ref, g_ref, w_ref, out_ref):
    mix = (jnp.dot(o_ref[0], w_ref[0], preferred_element_type=F32)
           + jnp.dot(y_ref[0], w_ref[1], preferred_element_type=F32))
    out_ref[0] = x_ref[0] + g_ref[0] * mix


def _outproj(x, o, yr, g, w_out_c):
    B, S, D = x.shape
    tm = PROJ_TM
    W = o.shape[-1]
    half = pl.BlockSpec((1, tm, W), lambda b, i: (b, i, 0))
    full = pl.BlockSpec((1, tm, D), lambda b, i: (b, i, 0))
    return pl.pallas_call(
        _outproj_kernel,
        grid=(B, S // tm),
        in_specs=[full, half, half, pl.BlockSpec((1, 1, D), lambda b, i: (b, 0, 0)),
                  _const_spec(w_out_c.shape)],
        out_specs=full,
        out_shape=jax.ShapeDtypeStruct((B, S, D), F32),
        compiler_params=pltpu.CompilerParams(
            dimension_semantics=("parallel", "parallel"), vmem_limit_bytes=VMEM_LIMIT),
        name="out_proj",
    )(x, o, yr, g, w_out_c)


def _ffn_weights(w1, w3, w2):
    return w1.astype(BF16), w3.astype(BF16), w2.astype(BF16)


def _block_diag(w):
    n, bw, _ = w.shape
    eye = jnp.eye(n, dtype=w.dtype)
    return (eye[:, None, :, None] * w[:, :, None, :]).reshape(n * bw, n * bw)


def kernel(x, c, rel_bias, ada_w, ada_b, norm_g, ffn1_w1, ffn1_w3, ffn1_w2, w_in, lam_q1, lam_k1, lam_q2, lam_k2, subln_g, conv_w, conv_b, gate_a_w, gate_a_b, gate_i_w, gate_i_b, lru_L, w_out, ffn2_w1, ffn2_w3, ffn2_w2, final_g):
    B, S, D = x.shape
    depth = ada_w.shape[0]
    c_pad = jnp.zeros((SUBLANES, D), F32).at[:B].set(c)
    bias = _bias_tiles(rel_bias, BIAS_T)
    for l in range(depth):
        mod = _modulation(c_pad, ada_w[l], ada_b[l])[:, :B]
        sh1, sc1, g1, sh2, sc2, g2, sh3, sc3, g3 = [m.reshape(B, 1, D) for m in mod]

        x = _ffn(x, norm_g[l, 0], sh1, sc1, g1,
                 *_ffn_weights(ffn1_w1[l], ffn1_w3[l], ffn1_w2[l]))

        w_in_b = w_in[l].astype(BF16)
        w_vt = w_in_b[:, 2 * ATTN_WIDTH:3 * ATTN_WIDTH].T
        q, k, vt, xr, gr = _inproj(x, norm_g[l, 1], sh2, sc2, w_in_b, w_vt)

        lambda_init = 0.8 - 0.6 * math.exp(-0.3 * l)
        o = _attention(q, k, vt, bias, lam_q1[l], lam_k1[l], lam_q2[l], lam_k2[l],
                       subln_g[l], lambda_init)

        w_gates = jnp.concatenate(
            [_block_diag(gate_a_w[l]), _block_diag(gate_i_w[l])], axis=1).astype(BF16)
        yr = _rglru(xr, gr, conv_w[l], conv_b[l], w_gates, gate_a_b[l], gate_i_b[l],
                    lru_L[l])

        w_out_c = w_out[l].astype(BF16).reshape(2, ATTN_WIDTH, D)
        x = _outproj(x, o, yr, g2, w_out_c)

        last = l == depth - 1
        x = _ffn(x, norm_g[l, 2], sh3, sc3, g3,
                 *_ffn_weights(ffn2_w1[l], ffn2_w3[l], ffn2_w2[l]),
                 final_g=final_g if last else None)
    return x
```

```python
import functools
import math

import jax
import jax.numpy as jnp
from jax import lax
from jax.experimental import pallas as pl
from jax.experimental.pallas import tpu as pltpu

F32 = jnp.float32
BF16 = jnp.bfloat16

ATTN_HEADS = 4
ATTN_QK_DIM = 64
ATTN_V_DIM = 2 * ATTN_QK_DIM
ATTN_WIDTH = ATTN_HEADS * ATTN_V_DIM
RNN_BLOCKS = 8
CONV_WIDTH = 4
RG_LRU_C = 8.0
N_BUCKETS = 32
MAX_DISTANCE = 128
NORM_EPS = 1e-6
N_MOD = 9

SUBLANES = 8
VMEM_LIMIT = 56 * 1024 * 1024

FFN_TM = 512
FFN_FC = 256
ATTN_T = 1024
ATTN_STRIP = 256
ATTN_LOOKAHEAD = 1
BIAS_T = 128
RNN_TC = 1024
NEG_BIG = -1e30
LOG2E = math.log2(math.e)


def _rms(x):
    return x * lax.rsqrt(jnp.mean(x * x, axis=-1, keepdims=True) + NORM_EPS)


def _const_spec(shape):
    nd = len(shape)
    return pl.BlockSpec(shape, lambda *_: (0,) * nd, pipeline_mode=pl.Buffered(1))


def _mod_kernel(c_ref, w_ref, b_ref, o_ref):
    c = c_ref[...]
    ca = c * jax.nn.sigmoid(c)
    o_ref[0] = jnp.dot(ca, w_ref[...], preferred_element_type=F32,
                       precision=lax.Precision.HIGHEST) + b_ref[0]


def _modulation(c_pad, ada_w, ada_b):
    rows, d = c_pad.shape
    n = ada_w.shape[1] // d
    return pl.pallas_call(
        _mod_kernel,
        grid=(n,),
        in_specs=[pl.BlockSpec((rows, d), lambda j: (0, 0)),
                  pl.BlockSpec((d, d), lambda j: (0, j)),
                  pl.BlockSpec((1, 1, d), lambda j: (j, 0, 0))],
        out_specs=pl.BlockSpec((1, rows, d), lambda j: (j, 0, 0)),
        out_shape=jax.ShapeDtypeStruct((n, rows, d), F32),
        compiler_params=pltpu.CompilerParams(vmem_limit_bytes=VMEM_LIMIT),
        name="adaln_mod",
    )(c_pad, ada_w, ada_b.reshape(n, 1, d))


def _modulated(x, ng_ref, sh_ref, sc_ref):
    return (_rms(x) * ng_ref[...] * (1.0 + sc_ref[0]) + sh_ref[0]).astype(BF16)


def _swiglu_residual(x, hb, g_ref, w1_ref, w3_ref, w2_ref, acc_ref):
    n_chunks = w1_ref.shape[1] // FFN_FC
    cols = lambda ci: slice(ci * FFN_FC, (ci + 1) * FFN_FC)
    up = lambda ci: (jnp.dot(hb, w1_ref[:, cols(ci)], preferred_element_type=F32),
                     jnp.dot(hb, w3_ref[:, cols(ci)], preferred_element_type=F32))
    a, b = up(0)
    for ci in range(n_chunks):
        if ci + 1 < n_chunks:
            a_next, b_next = up(ci + 1)
        u = (a * jax.nn.sigmoid(a) * b).astype(BF16)
        down = jnp.dot(u, w2_ref[cols(ci), :], preferred_element_type=F32)
        if ci == 0:
            acc_ref[...] = down
        else:
            acc_ref[...] += down
        a, b = a_next, b_next
    return x + 0.5 * g_ref[0] * acc_ref[...]


def _ffn_in_kernel(x_ref, ng1_ref, sh1_ref, sc1_ref, g1_ref, w1_ref, w3_ref, w2_ref,
                   ng2_ref, sh2_ref, sc2_ref, win_ref, wvt_ref,
                   x1_ref, q_ref, k_ref, vt_ref, xr_ref, gr_ref, acc_ref):
    x = x_ref[0]
    y = _swiglu_residual(x, _modulated(x, ng1_ref, sh1_ref, sc1_ref),
                         g1_ref, w1_ref, w3_ref, w2_ref, acc_ref)
    x1_ref[0] = y
    hb = _modulated(y, ng2_ref, sh2_ref, sc2_ref)
    scale = ATTN_QK_DIM ** -0.5 * LOG2E
    W = q_ref.shape[-1]
    proj = lambda n: jnp.dot(hb, win_ref[:, n * W:(n + 1) * W], preferred_element_type=F32)
    q_ref[0] = (proj(0) * scale).astype(BF16)
    k_ref[0] = proj(1).astype(BF16)
    vt_ref[0, 0] = lax.dot_general(wvt_ref[...], hb, (((1,), (1,)), ((), ())),
                                   preferred_element_type=F32).astype(BF16)
    xr_ref[0] = proj(3)
    gr_ref[0] = proj(4)


def _ffn_in(x, ng1, sh1, sc1, g1, w1, w3, w2, ng2, sh2, sc2, w_in_b, w_vt):
    B, S, D = x.shape
    tm, T = FFN_TM, ATTN_T
    W = w_vt.shape[0]
    per_t = T // tm
    vec = pl.BlockSpec((1, 1, D), lambda b, i: (b, 0, 0))
    row = _const_spec((1, D))
    full = pl.BlockSpec((1, tm, D), lambda b, i: (b, i, 0))
    tok = pl.BlockSpec((1, tm, W), lambda b, i: (b, i, 0))
    vt_spec = pl.BlockSpec((1, 1, W, tm), lambda b, i: (b, i // per_t, 0, i % per_t))
    return pl.pallas_call(
        _ffn_in_kernel,
        grid=(B, S // tm),
        in_specs=[full, row, vec, vec, vec,
                  _const_spec(w1.shape), _const_spec(w3.shape), _const_spec(w2.shape),
                  row, vec, vec, _const_spec(w_in_b.shape), _const_spec(w_vt.shape)],
        out_specs=[full, tok, tok, vt_spec, tok, tok],
        out_shape=[jax.ShapeDtypeStruct((B, S, D), F32)]
                  + [jax.ShapeDtypeStruct((B, S, W), BF16)] * 2
                  + [jax.ShapeDtypeStruct((B, S // T, W, T), BF16)]
                  + [jax.ShapeDtypeStruct((B, S, W), F32)] * 2,
        scratch_shapes=[pltpu.VMEM((tm, D), F32)],
        compiler_params=pltpu.CompilerParams(
            dimension_semantics=("parallel", "parallel"), vmem_limit_bytes=VMEM_LIMIT),
        name="ffn_in_proj",
    )(x, ng1.reshape(1, D), sh1, sc1, g1, w1, w3, w2, ng2.reshape(1, D), sh2, sc2,
      w_in_b, w_vt)


def _out_ffn_kernel(x_ref, o_ref, yr_ref, g2_ref, wout_ref, ng_ref, sh_ref, sc_ref, g_ref,
                    w1_ref, w3_ref, w2_ref, *rest, final_norm):
    if final_norm:
        fg_ref, out_ref, acc_ref = rest
    else:
        out_ref, acc_ref = rest
    mix = (jnp.dot(o_ref[0], wout_ref[0], preferred_element_type=F32)
           + jnp.dot(yr_ref[0], wout_ref[1], preferred_element_type=F32))
    x = x_ref[0] + g2_ref[0] * mix
    y = _swiglu_residual(x, _modulated(x, ng_ref, sh_ref, sc_ref),
                         g_ref, w1_ref, w3_ref, w2_ref, acc_ref)
    if final_norm:
        y = _rms(y) * fg_ref[...]
    out_ref[0] = y


def _out_ffn(x, o, yr, g2, w_out_c, ng, sh, sc, g, w1, w3, w2, final_g=None):
    B, S, D = x.shape
    tm = FFN_TM
    W = o.shape[-1]
    final_norm = final_g is not None
    vec = pl.BlockSpec((1, 1, D), lambda b, i: (b, 0, 0))
    row = _const_spec((1, D))
    full = pl.BlockSpec((1, tm, D), lambda b, i: (b, i, 0))
    half = pl.BlockSpec((1, tm, W), lambda b, i: (b, i, 0))
    in_specs = [full, half, half, vec, _const_spec(w_out_c.shape), row, vec, vec, vec,
                _const_spec(w1.shape), _const_spec(w3.shape), _const_spec(w2.shape)]
    args = [x, o, yr, g2, w_out_c, ng.reshape(1, D), sh, sc, g, w1, w3, w2]
    if final_norm:
        in_specs.append(row)
        args.append(final_g.reshape(1, D))
    return pl.pallas_call(
        functools.partial(_out_ffn_kernel, final_norm=final_norm),
        grid=(B, S // tm),
        in_specs=in_specs,
        out_specs=full,
        out_shape=jax.ShapeDtypeStruct((B, S, D), F32),
        scratch_shapes=[pltpu.VMEM((tm, D), F32)],
        compiler_params=pltpu.CompilerParams(
            dimension_semantics=("parallel", "parallel"), vmem_limit_bytes=VMEM_LIMIT),
        name="out_proj_ffn_final" if final_norm else "out_proj_ffn",
    )(*args)


def _bias_kernel(rb_ref, o_ref):
    mp = pl.program_id(0)
    T = o_ref.shape[-1]
    key = lax.broadcasted_iota(jnp.int32, (T, T), 0)
    qry = lax.broadcasted_iota(jnp.int32, (T, T), 1)
    max_exact = N_BUCKETS // 2
    last = rb_ref[N_BUCKETS - 1, mp]
    for which in range(2):
        rel = qry - key + (T if which == 0 else 0)
        n = jnp.maximum(rel, 0)
        nf = jnp.maximum(n, 1).astype(F32)
        large = max_exact + (jnp.log(nf / max_exact) / math.log(MAX_DISTANCE / max_exact)
                             * (N_BUCKETS - max_exact)).astype(jnp.int32)
        large = jnp.minimum(large, N_BUCKETS - 1)
        bucket = jnp.where(n < max_exact, n, large)
        val = jnp.zeros((T, T), F32)
        for bk in range(N_BUCKETS - 1):
            val = jnp.where(bucket == bk, (rb_ref[bk, mp] - last) * LOG2E, val)
        if which == 1:
            val = jnp.where(rel >= 0, val, NEG_BIG)
        o_ref[0, which] = val


def _bias_tiles(rel_bias, T):
    n_maps = rel_bias.shape[1]
    return pl.pallas_call(
        _bias_kernel,
        grid=(n_maps,),
        in_specs=[pl.BlockSpec(memory_space=pltpu.SMEM)],
        out_specs=pl.BlockSpec((1, 2, T, T), lambda m: (m, 0, 0, 0)),
        out_shape=jax.ShapeDtypeStruct((n_maps, 2, T, T), F32),
        name="rel_bias_tiles",
    )(rel_bias)


def _attn_kernel(q_ref, qn_ref, k_ref, vt_ref, bias_ref, lq1_ref, lk1_ref, lq2_ref, lk2_ref,
                 sg_ref, o_ref, qst_ref, m_ref, l_ref, acc_ref, s_ref, *, lambda_init):
    T = q_ref.shape[1]
    dqk = ATTN_QK_DIM
    qi = pl.program_id(2)

    for slot, ref in enumerate((q_ref, qn_ref)):
        qt = ref[0].astype(F32).T
        chan = lax.broadcasted_iota(jnp.int32, qt.shape, 0)
        qst_ref[slot, :, :T] = jnp.where(chan < dqk, qt, 0.0).astype(BF16)
        qst_ref[slot, :, T:] = jnp.where(chan >= dqk, qt, 0.0).astype(BF16)
    m_ref[...] = jnp.full_like(m_ref, NEG_BIG)
    l_ref[...] = jnp.zeros_like(l_ref)
    acc_ref[...] = jnp.zeros_like(acc_ref)

    strips = [slice(c * ATTN_STRIP, (c + 1) * ATTN_STRIP)
              for c in range(2 * T // ATTN_STRIP)]

    def n_keys(which, c):
        return strips[c].start % T + ATTN_STRIP if which == 1 else T

    def put_scores(buf, j, which, c, slot=0):
        nk = n_keys(which, c)
        start = j * T if isinstance(j, int) else pl.multiple_of(j * T, T)
        kj = k_ref[0, pl.ds(start, nk), :]
        s_ref[buf, :nk, strips[c]] = jnp.dot(kj, qst_ref[slot, :, strips[c]],
                                             preferred_element_type=F32)

    def biased(st, which, mp, q0):
        rows = []
        for a in range(st.shape[0] // BIAS_T):
            cols = []
            for b in range(q0 // BIAS_T, (q0 + ATTN_STRIP) // BIAS_T):
                lo = (b - q0 // BIAS_T) * BIAS_T
                piece = st[a * BIAS_T:(a + 1) * BIAS_T, lo:lo + BIAS_T]
                dist = b - a + (T // BIAS_T if which == 0 else 0)
                if dist < 0:
                    piece = jnp.full_like(piece, NEG_BIG)
                elif dist <= 1:
                    piece = piece + bias_ref[0, mp, 1 - dist]
                cols.append(piece)
            rows.append(jnp.concatenate(cols, axis=1))
        return jnp.concatenate(rows, axis=0)

    def step(buf, j, which, nxt):
        for c in range(ATTN_LOOKAHEAD):
            put_scores(*nxt[:3], c, nxt[3])
        for c, sl in enumerate(strips):
            if c + ATTN_LOOKAHEAD < len(strips):
                put_scores(*nxt[:3], c + ATTN_LOOKAHEAD, nxt[3])
            nk = n_keys(which, c)
            st = s_ref[buf, :nk, sl]
            if which is not None:
                st = biased(st, which, *divmod(sl.start, T))
            m_old = m_ref[:, sl]
            m_new = jnp.maximum(m_old, jnp.max(st, axis=0, keepdims=True))
            alpha = jnp.exp2(m_old - m_new)
            p = jnp.exp2(st - m_new)
            l_ref[:, sl] = alpha * l_ref[:, sl] + jnp.sum(p, axis=0, keepdims=True)
            acc_ref[:, sl] = alpha * acc_ref[:, sl] + jnp.dot(
                vt_ref[0, j, :, :nk], p.astype(BF16), preferred_element_type=F32)
            m_ref[:, sl] = m_new

    next_q = (0, 0, None, 1)

    @pl.when(qi == 0)
    def _():
        for c in range(len(strips)):
            put_scores(2, 0, 1, c)
        step(2, 0, 1, next_q)

    @pl.when(qi >= 1)
    def _():
        n_far = qi - 1

        def pair(i, carry):
            j = 2 * i
            step(0, j, None, (1, j + 1, None, 0))
            step(1, j + 1, None, (0, j + 2, None, 0))
            return carry

        lax.fori_loop(0, n_far // 2, pair, 0)

        @pl.when(n_far % 2 == 0)
        def _():
            step(0, qi - 1, 0, (1, qi, 1, 0))
            step(1, qi, 1, next_q)

        @pl.when(n_far % 2 == 1)
        def _():
            step(0, qi - 2, None, (1, qi - 1, None, 0))
            step(1, qi - 1, 0, (2, qi, 1, 0))
            step(2, qi, 1, next_q)

    lam = (jnp.exp(jnp.sum(lq1_ref[...] * lk1_ref[...], axis=-1, keepdims=True))
           - jnp.exp(jnp.sum(lq2_ref[...] * lk2_ref[...], axis=-1, keepdims=True))
           + lambda_init)
    o_all = acc_ref[...] / l_ref[...]
    ot = o_all[:, :T] - lam * o_all[:, T:]
    ot = ot * lax.rsqrt(jnp.mean(ot * ot, axis=0, keepdims=True) + NORM_EPS)
    o_ref[0] = (ot.T * sg_ref[...] * (1.0 - lambda_init)).astype(o_ref.dtype)


def _attention(q, k, vt, bias, lq1, lk1, lq2, lk2, subln_g, lambda_init):
    B, S, _ = q.shape
    T = ATTN_T
    H, dv, dqk = ATTN_HEADS, ATTN_V_DIM, ATTN_QK_DIM
    bias = bias.reshape(H, 2, 2, BIAS_T, BIAS_T)
    small = lambda n: _const_spec((1, n))
    return pl.pallas_call(
        functools.partial(_attn_kernel, lambda_init=lambda_init),
        grid=(B, H, S // T),
        in_specs=[pl.BlockSpec((1, T, 2 * dqk), lambda b, h, i: (b, i, h)),
                  pl.BlockSpec((1, T, 2 * dqk),
                               lambda b, h, i: (b, jnp.minimum(i + 1, S // T - 1), h)),
                  pl.BlockSpec((1, S, 2 * dqk), lambda b, h, i: (b, 0, h)),
                  pl.BlockSpec((1, S // T, dv, T), lambda b, h, i: (b, 0, h, 0)),
                  pl.BlockSpec((1, 2, 2, BIAS_T, BIAS_T), lambda b, h, i: (h, 0, 0, 0, 0)),
                  small(dqk), small(dqk), small(dqk), small(dqk), small(dv)],
        out_specs=pl.BlockSpec((1, T, dv), lambda b, h, i: (b, i, h)),
        out_shape=jax.ShapeDtypeStruct((B, S, H * dv), BF16),
        scratch_shapes=[pltpu.VMEM((2, 2 * dqk, 2 * T), BF16),
                        pltpu.VMEM((1, 2 * T), F32),
                        pltpu.VMEM((1, 2 * T), F32),
                        pltpu.VMEM((dv, 2 * T), F32),
                        pltpu.VMEM((3, T, 2 * T), F32)],
        compiler_params=pltpu.CompilerParams(
            dimension_semantics=("parallel", "parallel", "arbitrary"),
            vmem_limit_bytes=VMEM_LIMIT),
        name="diff_attn",
    )(q, q, k, vt, bias, lq1.reshape(1, dqk), lk1.reshape(1, dqk), lq2.reshape(1, dqk),
      lk2.reshape(1, dqk), subln_g.reshape(1, dv))


def _rglru_kernel(xr_ref, gr_ref, cw_ref, cb_ref, wg_ref, ba_ref, bi_ref, L_ref,
                  y_ref, tail_ref, h_ref, a_buf, b_buf):
    Tc, C = xr_ref.shape[1], xr_ref.shape[2]
    G = Tc // SUBLANES

    @pl.when(pl.program_id(1) == 0)
    def _():
        tail_ref[...] = jnp.zeros_like(tail_ref)
        h_ref[...] = jnp.zeros_like(h_ref)

    x = xr_ref[0]
    tail = tail_ref[...]
    sub = lax.broadcasted_iota(jnp.int32, (SUBLANES, C), 0)
    xc = x * cw_ref[CONV_WIDTH - 1:CONV_WIDTH, :] + cb_ref[...]
    for k in range(1, CONV_WIDTH):
        xs = pltpu.roll(x, k, 0)
        head = jnp.where(sub < k, pltpu.roll(tail, k, 0), xs[:SUBLANES])
        xs = jnp.concatenate([head, xs[SUBLANES:]], axis=0)
        xc = xc + xs * cw_ref[CONV_WIDTH - 1 - k:CONV_WIDTH - k, :]
    tail_ref[...] = x[Tc - SUBLANES:]

    gates = jnp.dot(xc.astype(BF16), wg_ref[...], preferred_element_type=F32)
    r = jax.nn.sigmoid(gates[:, :C] + ba_ref[...])
    ig = jax.nn.sigmoid(gates[:, C:] + bi_ref[...])
    L = L_ref[...]
    log_sig = jnp.minimum(L, 0.0) - jnp.log1p(jnp.exp(-jnp.abs(L)))
    log_a = RG_LRU_C * r * log_sig
    a = jnp.exp(log_a)
    v = -jnp.tanh(log_a) * (a * a + 1.0)
    root = jnp.where(v > 0.0, v * lax.rsqrt(v), 0.0)
    u = root * (ig * xc)

    a3 = a.reshape(G, SUBLANES, C)
    u3 = u.reshape(G, SUBLANES, C)
    sub3 = lax.broadcasted_iota(jnp.int32, (G, SUBLANES, C), 1)
    for k in (1, 2, 4):
        a_sh = pltpu.roll(a3, k, 1)
        u_sh = pltpu.roll(u3, k, 1)
        ok = sub3 >= k
        u3 = jnp.where(ok, a3 * u_sh + u3, u3)
        a3 = jnp.where(ok, a3 * a_sh, a3)
    a_buf[...] = a3
    b_buf[...] = u3

    def body(g, hprev):
        hg = a_buf[g] * hprev + b_buf[g]
        b_buf[g] = hg
        return hg[SUBLANES - 1:SUBLANES, :]

    h_last = lax.fori_loop(0, G, body, h_ref[...], unroll=8)
    h_ref[...] = h_last

    h = b_buf[...].reshape(Tc, C)
    gr = gr_ref[0]
    gelu = 0.5 * gr * (1.0 + jnp.tanh(
        math.sqrt(2.0 / math.pi) * (gr + 0.044715 * (gr * gr * gr))))
    y_ref[0] = (h * gelu).astype(y_ref.dtype)


def _rglru(xr, gr, conv_w, conv_b, w_gates, b_a, b_i, lru_L):
    B, S, C = xr.shape
    Tc = RNN_TC
    tok = pl.BlockSpec((1, Tc, C), lambda b, t: (b, t, 0))
    row = lambda: _const_spec((1, C))
    return pl.pallas_call(
        _rglru_kernel,
        grid=(B, S // Tc),
        in_specs=[tok, tok, _const_spec((CONV_WIDTH, C)), row(),
                  _const_spec(w_gates.shape), row(), row(), row()],
        out_specs=tok,
        out_shape=jax.ShapeDtypeStruct((B, S, C), BF16),
        scratch_shapes=[pltpu.VMEM((SUBLANES, C), F32),
                        pltpu.VMEM((1, C), F32),
                        pltpu.VMEM((Tc // SUBLANES, SUBLANES, C), F32),
                        pltpu.VMEM((Tc // SUBLANES, SUBLANES, C), F32)],
        compiler_params=pltpu.CompilerParams(
            dimension_semantics=("parallel", "arbitrary"), vmem_limit_bytes=VMEM_LIMIT),
        name="rg_lru",
    )(xr, gr, conv_w, conv_b.reshape(1, C), w_gates, b_a.reshape(1, C),
      b_i.reshape(1, C), lru_L.reshape(1, C))


def _ffn_weights(w1, w3, w2):
    return w1.astype(BF16), w3.astype(BF16), w2.astype(BF16)


def _block_diag(w):
    n, bw, _ = w.shape
    eye = jnp.eye(n, dtype=w.dtype)
    return (eye[:, None, :, None] * w[:, :, None, :]).reshape(n * bw, n * bw)


def kernel(x, c, rel_bias, ada_w, ada_b, norm_g, ffn1_w1, ffn1_w3, ffn1_w2, w_in, lam_q1, lam_k1, lam_q2, lam_k2, subln_g, conv_w, conv_b, gate_a_w, gate_a_b, gate_i_w, gate_i_b, lru_L, w_out, ffn2_w1, ffn2_w3, ffn2_w2, final_g):
    B, S, D = x.shape
    depth = ada_w.shape[0]
    c_pad = jnp.zeros((SUBLANES, D), F32).at[:B].set(c)
    bias = _bias_tiles(rel_bias, BIAS_T)
    for l in range(depth):
        mod = _modulation(c_pad, ada_w[l], ada_b[l])[:, :B]
        sh1, sc1, g1, sh2, sc2, g2, sh3, sc3, g3 = [m.reshape(B, 1, D) for m in mod]

        w_in_b = w_in[l].astype(BF16)
        w_vt = w_in_b[:, 2 * ATTN_WIDTH:3 * ATTN_WIDTH].T
        x, q, k, vt, xr, gr = _ffn_in(
            x, norm_g[l, 0], sh1, sc1, g1,
            *_ffn_weights(ffn1_w1[l], ffn1_w3[l], ffn1_w2[l]),
            norm_g[l, 1], sh2, sc2, w_in_b, w_vt)

        lambda_init = 0.8 - 0.6 * math.exp(-0.3 * l)
        o = _attention(q, k, vt, bias, lam_q1[l], lam_k1[l], lam_q2[l], lam_k2[l],
                       subln_g[l], lambda_init)

        w_gates = jnp.concatenate(
            [_block_diag(gate_a_w[l]), _block_diag(gate_i_w[l])], axis=1).astype(BF16)
        yr = _rglru(xr, gr, conv_w[l], conv_b[l], w_gates, gate_a_b[l], gate_i_b[l],
                    lru_L[l])

        w_out_c = w_out[l].astype(BF16).reshape(2, ATTN_WIDTH, D)
        last = l == depth - 1
        x = _out_ffn(x, o, yr, g2, w_out_c, norm_g[l, 2], sh3, sc3, g3,
                     *_ffn_weights(ffn2_w1[l], ffn2_w3[l], ffn2_w2[l]),
                     final_g=final_g if last else None)
    return x
```

```python
import functools
import math

import jax
import jax.numpy as jnp
from jax import lax
from jax.experimental import pallas as pl
from jax.experimental.pallas import tpu as pltpu

F32 = jnp.float32
BF16 = jnp.bfloat16

ATTN_HEADS = 4
ATTN_QK_DIM = 64
ATTN_V_DIM = 2 * ATTN_QK_DIM
ATTN_WIDTH = ATTN_HEADS * ATTN_V_DIM
RNN_BLOCKS = 8
CONV_WIDTH = 4
RG_LRU_C = 8.0
N_BUCKETS = 32
MAX_DISTANCE = 128
NORM_EPS = 1e-6
N_MOD = 9

SUBLANES = 8
VMEM_LIMIT = 56 * 1024 * 1024

FFN_TM = 512
FFN_FC = 256
ATTN_T = 1024
ATTN_STRIP = 256
ATTN_LOOKAHEAD = 1
ATTN_HOLD = 3
BIAS_T = 128
RNN_TC = 1024
NEG_BIG = -1e30
LOG2E = math.log2(math.e)


def _rms(x):
    return x * lax.rsqrt(jnp.mean(x * x, axis=-1, keepdims=True) + NORM_EPS)


def _const_spec(shape):
    nd = len(shape)
    return pl.BlockSpec(shape, lambda *_: (0,) * nd, pipeline_mode=pl.Buffered(1))


def _mod_kernel(c_ref, w_ref, b_ref, o_ref):
    c = c_ref[...]
    ca = c * jax.nn.sigmoid(c)
    o_ref[0] = jnp.dot(ca, w_ref[...], preferred_element_type=F32,
                       precision=lax.Precision.HIGHEST) + b_ref[0]


def _modulation(c_pad, ada_w, ada_b):
    rows, d = c_pad.shape
    n = ada_w.shape[1] // d
    return pl.pallas_call(
        _mod_kernel,
        grid=(n,),
        in_specs=[pl.BlockSpec((rows, d), lambda j: (0, 0)),
                  pl.BlockSpec((d, d), lambda j: (0, j)),
                  pl.BlockSpec((1, 1, d), lambda j: (j, 0, 0))],
        out_specs=pl.BlockSpec((1, rows, d), lambda j: (j, 0, 0)),
        out_shape=jax.ShapeDtypeStruct((n, rows, d), F32),
        compiler_params=pltpu.CompilerParams(vmem_limit_bytes=VMEM_LIMIT),
        name="adaln_mod",
    )(c_pad, ada_w, ada_b.reshape(n, 1, d))


def _modulated(x, ng_ref, sh_ref, sc_ref):
    return (_rms(x) * ng_ref[...] * (1.0 + sc_ref[0]) + sh_ref[0]).astype(BF16)


def _swiglu_residual(x, hb, g_ref, w1_ref, w3_ref, w2_ref, acc_ref):
    n_chunks = w1_ref.shape[1] // FFN_FC
    cols = lambda ci: slice(ci * FFN_FC, (ci + 1) * FFN_FC)
    up = lambda ci: (jnp.dot(hb, w1_ref[:, cols(ci)], preferred_element_type=F32),
                     jnp.dot(hb, w3_ref[:, cols(ci)], preferred_element_type=F32))
    a, b = up(0)
    for ci in range(n_chunks):
        if ci + 1 < n_chunks:
            a_next, b_next = up(ci + 1)
        u = (a * jax.nn.sigmoid(a) * b).astype(BF16)
        down = jnp.dot(u, w2_ref[cols(ci), :], preferred_element_type=F32)
        if ci == 0:
            acc_ref[...] = down
        else:
            acc_ref[...] += down
        a, b = a_next, b_next
    return x + 0.5 * g_ref[0] * acc_ref[...]


def _ffn_in_kernel(x_ref, ng1_ref, sh1_ref, sc1_ref, g1_ref, w1_ref, w3_ref, w2_ref,
                   ng2_ref, sh2_ref, sc2_ref, win_ref, wqvt_ref,
                   x1_ref, q_ref, k_ref, vt_ref, xr_ref, gr_ref, acc_ref):
    x = x_ref[0]
    y = _swiglu_residual(x, _modulated(x, ng1_ref, sh1_ref, sc1_ref),
                         g1_ref, w1_ref, w3_ref, w2_ref, acc_ref)
    x1_ref[0] = y
    hb = _modulated(y, ng2_ref, sh2_ref, sc2_ref)
    scale = ATTN_QK_DIM ** -0.5 * LOG2E
    W = k_ref.shape[-1]
    proj = lambda n: jnp.dot(hb, win_ref[:, n * W:(n + 1) * W], preferred_element_type=F32)
    k_ref[0] = proj(1).astype(BF16)
    qvt = lax.dot_general(wqvt_ref[...], hb, (((1,), (1,)), ((), ())),
                          preferred_element_type=F32)
    vt_ref[0, 0] = qvt[W:].astype(BF16)
    qt = (qvt[:W] * scale).astype(BF16)
    dqk = ATTN_QK_DIM
    zeros = jnp.zeros((dqk, qt.shape[1]), BF16)
    for h in range(ATTN_HEADS):
        top = qt[2 * dqk * h:2 * dqk * h + dqk]
        bot = qt[2 * dqk * h + dqk:2 * dqk * (h + 1)]
        q_ref[0, h, 0, 0] = jnp.concatenate([top, zeros], axis=0)
        q_ref[0, h, 0, 1] = jnp.concatenate([zeros, bot], axis=0)
    xr_ref[0] = proj(3)
    gr_ref[0] = proj(4)


def _ffn_in(x, ng1, sh1, sc1, g1, w1, w3, w2, ng2, sh2, sc2, w_in_b, w_qvt):
    B, S, D = x.shape
    tm, T = FFN_TM, ATTN_T
    W = w_qvt.shape[0] // 2
    H, dqk = ATTN_HEADS, ATTN_QK_DIM
    per_t = T // tm
    vec = pl.BlockSpec((1, 1, D), lambda b, i: (b, 0, 0))
    row = _const_spec((1, D))
    full = pl.BlockSpec((1, tm, D), lambda b, i: (b, i, 0))
    tok = pl.BlockSpec((1, tm, W), lambda b, i: (b, i, 0))
    vt_spec = pl.BlockSpec((1, 1, W, tm), lambda b, i: (b, i // per_t, 0, i % per_t))
    q_spec = pl.BlockSpec((1, H, 1, 2, 2 * dqk, tm),
                          lambda b, i: (b, 0, i // per_t, 0, 0, i % per_t))
    return pl.pallas_call(
        _ffn_in_kernel,
        grid=(B, S // tm),
        in_specs=[full, row, vec, vec, vec,
                  _const_spec(w1.shape), _const_spec(w3.shape), _const_spec(w2.shape),
                  row, vec, vec, _const_spec(w_in_b.shape), _const_spec(w_qvt.shape)],
        out_specs=[full, q_spec, tok, vt_spec, tok, tok],
        out_shape=[jax.ShapeDtypeStruct((B, S, D), F32),
                   jax.ShapeDtypeStruct((B, H, S // T, 2, 2 * dqk, T), BF16),
                   jax.ShapeDtypeStruct((B, S, W), BF16),
                   jax.ShapeDtypeStruct((B, S // T, W, T), BF16)]
                  + [jax.ShapeDtypeStruct((B, S, W), F32)] * 2,
        scratch_shapes=[pltpu.VMEM((tm, D), F32)],
        compiler_params=pltpu.CompilerParams(
            dimension_semantics=("parallel", "parallel"), vmem_limit_bytes=VMEM_LIMIT),
        name="ffn_in_proj",
    )(x, ng1.reshape(1, D), sh1, sc1, g1, w1, w3, w2, ng2.reshape(1, D), sh2, sc2,
      w_in_b, w_qvt)


def _out_ffn_kernel(x_ref, o_ref, yr_ref, g2_ref, wout_ref, ng_ref, sh_ref, sc_ref, g_ref,
                    w1_ref, w3_ref, w2_ref, *rest, final_norm):
    if final_norm:
        fg_ref, out_ref, acc_ref = rest
    else:
        out_ref, acc_ref = rest
    mix = (jnp.dot(o_ref[0], wout_ref[0], preferred_element_type=F32)
           + jnp.dot(yr_ref[0], wout_ref[1], preferred_element_type=F32))
    x = x_ref[0] + g2_ref[0] * mix
    y = _swiglu_residual(x, _modulated(x, ng_ref, sh_ref, sc_ref),
                         g_ref, w1_ref, w3_ref, w2_ref, acc_ref)
    if final_norm:
        y = _rms(y) * fg_ref[...]
    out_ref[0] = y


def _out_ffn(x, o, yr, g2, w_out_c, ng, sh, sc, g, w1, w3, w2, final_g=None):
    B, S, D = x.shape
    tm = FFN_TM
    W = o.shape[-1]
    final_norm = final_g is not None
    vec = pl.BlockSpec((1, 1, D), lambda b, i: (b, 0, 0))
    row = _const_spec((1, D))
    full = pl.BlockSpec((1, tm, D), lambda b, i: (b, i, 0))
    half = pl.BlockSpec((1, tm, W), lambda b, i: (b, i, 0))
    in_specs = [full, half, half, vec, _const_spec(w_out_c.shape), row, vec, vec, vec,
                _const_spec(w1.shape), _const_spec(w3.shape), _const_spec(w2.shape)]
    args = [x, o, yr, g2, w_out_c, ng.reshape(1, D), sh, sc, g, w1, w3, w2]
    if final_norm:
        in_specs.append(row)
        args.append(final_g.reshape(1, D))
    return pl.pallas_call(
        functools.partial(_out_ffn_kernel, final_norm=final_norm),
        grid=(B, S // tm),
        in_specs=in_specs,
        out_specs=full,
        out_shape=jax.ShapeDtypeStruct((B, S, D), F32),
        scratch_shapes=[pltpu.VMEM((tm, D), F32)],
        compiler_params=pltpu.CompilerParams(
            dimension_semantics=("parallel", "parallel"), vmem_limit_bytes=VMEM_LIMIT),
        name="out_proj_ffn_final" if final_norm else "out_proj_ffn",
    )(*args)


def _bias_kernel(rb_ref, o_ref):
    mp = pl.program_id(0)
    T = o_ref.shape[-1]
    key = lax.broadcasted_iota(jnp.int32, (T, T), 0)
    qry = lax.broadcasted_iota(jnp.int32, (T, T), 1)
    max_exact = N_BUCKETS // 2
    last = rb_ref[N_BUCKETS - 1, mp]
    for which in range(2):
        rel = qry - key + (T if which == 0 else 0)
        n = jnp.maximum(rel, 0)
        nf = jnp.maximum(n, 1).astype(F32)
        large = max_exact + (jnp.log(nf / max_exact) / math.log(MAX_DISTANCE / max_exact)
                             * (N_BUCKETS - max_exact)).astype(jnp.int32)
        large = jnp.minimum(large, N_BUCKETS - 1)
        bucket = jnp.where(n < max_exact, n, large)
        val = jnp.zeros((T, T), F32)
        for bk in range(N_BUCKETS - 1):
            val = jnp.where(bucket == bk, (rb_ref[bk, mp] - last) * LOG2E, val)
        if which == 1:
            val = jnp.where(rel >= 0, val, NEG_BIG)
        o_ref[0, which] = val


def _bias_tiles(rel_bias, T):
    n_maps = rel_bias.shape[1]
    return pl.pallas_call(
        _bias_kernel,
        grid=(n_maps,),
        in_specs=[pl.BlockSpec(memory_space=pltpu.SMEM)],
        out_specs=pl.BlockSpec((1, 2, T, T), lambda m: (m, 0, 0, 0)),
        out_shape=jax.ShapeDtypeStruct((n_maps, 2, T, T), F32),
        name="rel_bias_tiles",
    )(rel_bias)


def _attn_kernel(q_ref, qn_ref, k_ref, vt_ref, bias_ref, lq1_ref, lk1_ref, lq2_ref, lk2_ref,
                 sg_ref, o_ref, m_ref, l_ref, acc_ref, s_ref, *, lambda_init):
    T = q_ref.shape[-1]
    qi = pl.program_id(2)
    q_refs = (q_ref, qn_ref)

    m_ref[...] = jnp.full_like(m_ref, NEG_BIG)
    l_ref[...] = jnp.zeros_like(l_ref)
    acc_ref[...] = jnp.zeros_like(acc_ref)

    strips = [slice(c * ATTN_STRIP, (c + 1) * ATTN_STRIP)
              for c in range(2 * T // ATTN_STRIP)]

    def n_keys(which, c):
        return strips[c].start % T + ATTN_STRIP if which == 1 else T

    def put_scores(buf, j, which, c, slot=0):
        nk = n_keys(which, c)
        start = j * T if isinstance(j, int) else pl.multiple_of(j * T, T)
        kj = k_ref[0, pl.ds(start, nk), :]
        mp, q0 = divmod(strips[c].start, T)
        qt = q_refs[slot][0, 0, 0, mp, :, q0:q0 + ATTN_STRIP]
        s_ref[buf, :nk, strips[c]] = jnp.dot(kj, qt, preferred_element_type=F32)

    def biased(st, which, mp, q0):
        rows = []
        for a in range(st.shape[0] // BIAS_T):
            cols = []
            for b in range(q0 // BIAS_T, (q0 + ATTN_STRIP) // BIAS_T):
                lo = (b - q0 // BIAS_T) * BIAS_T
                piece = st[a * BIAS_T:(a + 1) * BIAS_T, lo:lo + BIAS_T]
                dist = b - a + (T // BIAS_T if which == 0 else 0)
                if dist < 0:
                    piece = jnp.full_like(piece, NEG_BIG)
                elif dist <= 1:
                    piece = piece + bias_ref[0, mp, 1 - dist]
                cols.append(piece)
            rows.append(jnp.concatenate(cols, axis=1))
        return jnp.concatenate(rows, axis=0)

    def step(buf, j, which, nxt, hold=0):
        early = len(strips) - hold
        for c in range(ATTN_LOOKAHEAD):
            put_scores(*nxt[:3], c, nxt[3])
        for c, sl in enumerate(strips):
            if c + ATTN_LOOKAHEAD < early:
                put_scores(*nxt[:3], c + ATTN_LOOKAHEAD, nxt[3])
            nk = n_keys(which, c)
            st = s_ref[buf, :nk, sl]
            if which is not None:
                st = biased(st, which, *divmod(sl.start, T))
            m_old = m_ref[:, sl]
            m_new = jnp.maximum(m_old, jnp.max(st, axis=0, keepdims=True))
            alpha = jnp.exp2(m_old - m_new)
            p = jnp.exp2(st - m_new)
            l_ref[:, sl] = alpha * l_ref[:, sl] + jnp.sum(p, axis=0, keepdims=True)
            acc_ref[:, sl] = alpha * acc_ref[:, sl] + jnp.dot(
                vt_ref[0, j, :, :nk], p.astype(BF16), preferred_element_type=F32)
            m_ref[:, sl] = m_new
        for c in range(early, len(strips)):
            put_scores(*nxt[:3], c, nxt[3])

    def last_step(buf):
        step(buf, qi, 1, (0, 0, None, 1), hold=ATTN_HOLD)
        lam = (jnp.exp(jnp.sum(lq1_ref[...] * lk1_ref[...], axis=-1, keepdims=True))
               - jnp.exp(jnp.sum(lq2_ref[...] * lk2_ref[...], axis=-1, keepdims=True))
               + lambda_init)
        o_all = acc_ref[...] / l_ref[...]
        ot = o_all[:, :T] - lam * o_all[:, T:]
        ot = ot * lax.rsqrt(jnp.mean(ot * ot, axis=0, keepdims=True) + NORM_EPS)
        o_ref[0] = (ot.T * sg_ref[...] * (1.0 - lambda_init)).astype(o_ref.dtype)

    @pl.when(qi == 0)
    def _():
        for c in range(len(strips)):
            put_scores(2, 0, 1, c)
        last_step(2)

    @pl.when(qi >= 1)
    def _():
        n_far = qi - 1

        def pair(i, carry):
            j = 2 * i
            step(0, j, None, (1, j + 1, None, 0))
            step(1, j + 1, None, (0, j + 2, None, 0))
            return carry

        lax.fori_loop(0, n_far // 2, pair, 0)

        @pl.when(n_far % 2 == 0)
        def _():
            step(0, qi - 1, 0, (1, qi, 1, 0))
            last_step(1)

        @pl.when(n_far % 2 == 1)
        def _():
            step(0, qi - 2, None, (1, qi - 1, None, 0))
            step(1, qi - 1, 0, (2, qi, 1, 0))
            last_step(2)


def _attention(q, k, vt, bias, lq1, lk1, lq2, lk2, subln_g, lambda_init):
    B, S, _ = k.shape
    T = ATTN_T
    H, dv, dqk = ATTN_HEADS, ATTN_V_DIM, ATTN_QK_DIM
    bias = bias.reshape(H, 2, 2, BIAS_T, BIAS_T)
    small = lambda n: _const_spec((1, n))
    return pl.pallas_call(
        functools.partial(_attn_kernel, lambda_init=lambda_init),
        grid=(B, H, S // T),
        in_specs=[pl.BlockSpec((1, 1, 1, 2, 2 * dqk, T), lambda b, h, i: (b, h, i, 0, 0, 0)),
                  pl.BlockSpec((1, 1, 1, 2, 2 * dqk, T),
                               lambda b, h, i: (b, h, jnp.minimum(i + 1, S // T - 1), 0, 0, 0)),
                  pl.BlockSpec((1, S, 2 * dqk), lambda b, h, i: (b, 0, h)),
                  pl.BlockSpec((1, S // T, dv, T), lambda b, h, i: (b, 0, h, 0)),
                  pl.BlockSpec((1, 2, 2, BIAS_T, BIAS_T), lambda b, h, i: (h, 0, 0, 0, 0)),
                  small(dqk), small(dqk), small(dqk), small(dqk), small(dv)],
        out_specs=pl.BlockSpec((1, T, dv), lambda b, h, i: (b, i, h)),
        out_shape=jax.ShapeDtypeStruct((B, S, H * dv), BF16),
        scratch_shapes=[pltpu.VMEM((1, 2 * T), F32),
                        pltpu.VMEM((1, 2 * T), F32),
                        pltpu.VMEM((dv, 2 * T), F32),
                        pltpu.VMEM((3, T, 2 * T), F32)],
        compiler_params=pltpu.CompilerParams(
            dimension_semantics=("parallel", "parallel", "arbitrary"),
            vmem_limit_bytes=VMEM_LIMIT),
        name="diff_attn",
    )(q, q, k, vt, bias, lq1.reshape(1, dqk), lk1.reshape(1, dqk), lq2.reshape(1, dqk),
      lk2.reshape(1, dqk), subln_g.reshape(1, dv))


def _rglru_kernel(xr_ref, gr_ref, cw_ref, cb_ref, wg_ref, ba_ref, bi_ref, L_ref,
                  y_ref, tail_ref, h_ref, a_buf, b_buf):
    Tc, C = xr_ref.shape[1], xr_ref.shape[2]
    G = Tc // SUBLANES

    @pl.when(pl.program_id(1) == 0)
    def _():
        tail_ref[...] = jnp.zeros_like(tail_ref)
        h_ref[...] = jnp.zeros_like(h_ref)

    x = xr_ref[0]
    tail = tail_ref[...]
    sub = lax.broadcasted_iota(jnp.int32, (SUBLANES, C), 0)
    xc = x * cw_ref[CONV_WIDTH - 1:CONV_WIDTH, :] + cb_ref[...]
    for k in range(1, CONV_WIDTH):
        xs = pltpu.roll(x, k, 0)
        head = jnp.where(sub < k, pltpu.roll(tail, k, 0), xs[:SUBLANES])
        xs = jnp.concatenate([head, xs[SUBLANES:]], axis=0)
        xc = xc + xs * cw_ref[CONV_WIDTH - 1 - k:CONV_WIDTH - k, :]
    tail_ref[...] = x[Tc - SUBLANES:]

    gates = jnp.dot(xc.astype(BF16), wg_ref[...], preferred_element_type=F32)
    r = jax.nn.sigmoid(gates[:, :C] + ba_ref[...])
    ig = jax.nn.sigmoid(gates[:, C:] + bi_ref[...])
    L = L_ref[...]
    log_sig = jnp.minimum(L, 0.0) - jnp.log1p(jnp.exp(-jnp.abs(L)))
    log_a = RG_LRU_C * r * log_sig
    a = jnp.exp(log_a)
    v = -jnp.tanh(log_a) * (a * a + 1.0)
    root = jnp.where(v > 0.0, v * lax.rsqrt(v), 0.0)
    u = root * (ig * xc)

    a3 = a.reshape(G, SUBLANES, C)
    u3 = u.reshape(G, SUBLANES, C)
    sub3 = lax.broadcasted_iota(jnp.int32, (G, SUBLANES, C), 1)
    for k in (1, 2, 4):
        a_sh = pltpu.roll(a3, k, 1)
        u_sh = pltpu.roll(u3, k, 1)
        ok = sub3 >= k
        u3 = jnp.where(ok, a3 * u_sh + u3, u3)
        a3 = jnp.where(ok, a3 * a_sh, a3)
    a_buf[...] = a3
    b_buf[...] = u3

    def body(g, hprev):
        hg = a_buf[g] * hprev + b_buf[g]
        b_buf[g] = hg
        return hg[SUBLANES - 1:SUBLANES, :]

    h_last = lax.fori_loop(0, G, body, h_ref[...], unroll=8)
    h_ref[...] = h_last

    h = b_buf[...].reshape(Tc, C)
    gr = gr_ref[0]
    gelu = 0.5 * gr * (1.0 + jnp.tanh(
        math.sqrt(2.0 / math.pi) * (gr + 0.044715 * (gr * gr * gr))))
    y_ref[0] = (h * gelu).astype(y_ref.dtype)


def _rglru(xr, gr, conv_w, conv_b, w_gates, b_a, b_i, lru_L):
    B, S, C = xr.shape
    Tc = RNN_TC
    tok = pl.BlockSpec((1, Tc, C), lambda b, t: (b, t, 0))
    row = lambda: _const_spec((1, C))
    return pl.pallas_call(
        _rglru_kernel,
        grid=(B, S // Tc),
        in_specs=[tok, tok, _const_spec((CONV_WIDTH, C)), row(),
                  _const_spec(w_gates.shape), row(), row(), row()],
        out_specs=tok,
        out_shape=jax.ShapeDtypeStruct((B, S, C), BF16),
        scratch_shapes=[pltpu.VMEM((SUBLANES, C), F32),
                        pltpu.VMEM((1, C), F32),
                        pltpu.VMEM((Tc // SUBLANES, SUBLANES, C), F32),
                        pltpu.VMEM((Tc // SUBLANES, SUBLANES, C), F32)],
        compiler_params=pltpu.CompilerParams(
            dimension_semantics=("parallel", "arbitrary"), vmem_limit_bytes=VMEM_LIMIT),
        name="rg_lru",
    )(xr, gr, conv_w, conv_b.reshape(1, C), w_gates, b_a.reshape(1, C),
      b_i.reshape(1, C), lru_L.reshape(1, C))


def _ffn_weights(w1, w3, w2):
    return w1.astype(BF16), w3.astype(BF16), w2.astype(BF16)


def _block_diag(w):
    n, bw, _ = w.shape
    eye = jnp.eye(n, dtype=w.dtype)
    return (eye[:, None, :, None] * w[:, :, None, :]).reshape(n * bw, n * bw)


def kernel(x, c, rel_bias, ada_w, ada_b, norm_g, ffn1_w1, ffn1_w3, ffn1_w2, w_in, lam_q1, lam_k1, lam_q2, lam_k2, subln_g, conv_w, conv_b, gate_a_w, gate_a_b, gate_i_w, gate_i_b, lru_L, w_out, ffn2_w1, ffn2_w3, ffn2_w2, final_g):
    B, S, D = x.shape
    depth = ada_w.shape[0]
    c_pad = jnp.zeros((SUBLANES, D), F32).at[:B].set(c)
    bias = _bias_tiles(rel_bias, BIAS_T)
    for l in range(depth):
        mod = _modulation(c_pad, ada_w[l], ada_b[l])[:, :B]
        sh1, sc1, g1, sh2, sc2, g2, sh3, sc3, g3 = [m.reshape(B, 1, D) for m in mod]

        w_in_b = w_in[l].astype(BF16)
        w_qvt = jnp.concatenate([w_in_b[:, :ATTN_WIDTH],
                                 w_in_b[:, 2 * ATTN_WIDTH:3 * ATTN_WIDTH]], axis=1).T
        x, q, k, vt, xr, gr = _ffn_in(
            x, norm_g[l, 0], sh1, sc1, g1,
            *_ffn_weights(ffn1_w1[l], ffn1_w3[l], ffn1_w2[l]),
            norm_g[l, 1], sh2, sc2, w_in_b, w_qvt)

        lambda_init = 0.8 - 0.6 * math.exp(-0.3 * l)
        o = _attention(q, k, vt, bias, lam_q1[l], lam_k1[l], lam_q2[l], lam_k2[l],
                       subln_g[l], lambda_init)

        w_gates = jnp.concatenate(
            [_block_diag(gate_a_w[l]), _block_diag(gate_i_w[l])], axis=1).astype(BF16)
        yr = _rglru(xr, gr, conv_w[l], conv_b[l], w_gates, gate_a_b[l], gate_i_b[l],
                    lru_L[l])

        w_out_c = w_out[l].astype(BF16).reshape(2, ATTN_WIDTH, D)
        last = l == depth - 1
        x = _out_ffn(x, o, yr, g2, w_out_c, norm_g[l, 2], sh3, sc3, g3,
                     *_ffn_weights(ffn2_w1[l], ffn2_w3[l], ffn2_w2[l]),
                     final_g=final_g if last else None)
    return x
```

```python
import functools
import math

import jax
import jax.numpy as jnp
from jax import lax
from jax.experimental import pallas as pl
from jax.experimental.pallas import tpu as pltpu

F32 = jnp.float32
BF16 = jnp.bfloat16

ATTN_HEADS = 4
ATTN_QK_DIM = 64
ATTN_V_DIM = 2 * ATTN_QK_DIM
ATTN_WIDTH = ATTN_HEADS * ATTN_V_DIM
RNN_BLOCKS = 8
CONV_WIDTH = 4
RG_LRU_C = 8.0
N_BUCKETS = 32
MAX_DISTANCE = 128
NORM_EPS = 1e-6
N_MOD = 9

SUBLANES = 8
VMEM_LIMIT = 56 * 1024 * 1024

FFN_TM = 512
FFN2_TM = 1024
FFN_FC = 256
ATTN_T = 1024
ATTN_STRIP = 256
ATTN_LOOKAHEAD = 1
ATTN_HOLD = 3
BIAS_T = 128
RNN_TC = 1024
NEG_BIG = -1e30
LOG2E = math.log2(math.e)


def _rms(x):
    return x * lax.rsqrt(jnp.mean(x * x, axis=-1, keepdims=True) + NORM_EPS)


def _const_spec(shape):
    nd = len(shape)
    return pl.BlockSpec(shape, lambda *_: (0,) * nd, pipeline_mode=pl.Buffered(1))


def _mod_kernel(c_ref, w_ref, b_ref, o_ref):
    c = c_ref[...]
    ca = c * jax.nn.sigmoid(c)
    o_ref[0] = jnp.dot(ca, w_ref[...], preferred_element_type=F32,
                       precision=lax.Precision.HIGHEST) + b_ref[0]


def _modulation(c_pad, ada_w, ada_b):
    rows, d = c_pad.shape
    n = ada_w.shape[1] // d
    return pl.pallas_call(
        _mod_kernel,
        grid=(n,),
        in_specs=[pl.BlockSpec((rows, d), lambda j: (0, 0)),
                  pl.BlockSpec((d, d), lambda j: (0, j)),
                  pl.BlockSpec((1, 1, d), lambda j: (j, 0, 0))],
        out_specs=pl.BlockSpec((1, rows, d), lambda j: (j, 0, 0)),
        out_shape=jax.ShapeDtypeStruct((n, rows, d), F32),
        compiler_params=pltpu.CompilerParams(vmem_limit_bytes=VMEM_LIMIT),
        name="adaln_mod",
    )(c_pad, ada_w, ada_b.reshape(n, 1, d))


def _modulated(x, ng_ref, sh_ref, sc_ref):
    return (_rms(x) * ng_ref[...] * (1.0 + sc_ref[0]) + sh_ref[0]).astype(BF16)


def _swiglu_residual(x, hb, g_ref, w1_ref, w3_ref, w2_ref, acc_ref):
    n_chunks = w1_ref.shape[1] // FFN_FC
    cols = lambda ci: slice(ci * FFN_FC, (ci + 1) * FFN_FC)
    up = lambda ci: (jnp.dot(hb, w1_ref[:, cols(ci)], preferred_element_type=F32),
                     jnp.dot(hb, w3_ref[:, cols(ci)], preferred_element_type=F32))
    a, b = up(0)
    for ci in range(n_chunks):
        if ci + 1 < n_chunks:
            a_next, b_next = up(ci + 1)
        u = (a * jax.nn.sigmoid(a) * b).astype(BF16)
        down = jnp.dot(u, w2_ref[cols(ci), :], preferred_element_type=F32)
        if ci == 0:
            acc_ref[...] = down
        else:
            acc_ref[...] += down
        a, b = a_next, b_next
    return x + 0.5 * g_ref[0] * acc_ref[...]


def _ffn_in_kernel(x_ref, ng1_ref, sh1_ref, sc1_ref, g1_ref, w1_ref, w3_ref, w2_ref,
                   ng2_ref, sh2_ref, sc2_ref, win_ref, wqvt_ref,
                   x1_ref, q_ref, k_ref, vt_ref, xr_ref, gr_ref, acc_ref):
    x = x_ref[0]
    y = _swiglu_residual(x, _modulated(x, ng1_ref, sh1_ref, sc1_ref),
                         g1_ref, w1_ref, w3_ref, w2_ref, acc_ref)
    x1_ref[0] = y
    hb = _modulated(y, ng2_ref, sh2_ref, sc2_ref)
    scale = ATTN_QK_DIM ** -0.5 * LOG2E
    W = k_ref.shape[-1]
    proj = lambda n: jnp.dot(hb, win_ref[:, n * W:(n + 1) * W], preferred_element_type=F32)
    k_ref[0] = proj(1).astype(BF16)
    qvt = lax.dot_general(wqvt_ref[...], hb, (((1,), (1,)), ((), ())),
                          preferred_element_type=F32)
    vt_ref[0, 0] = qvt[W:].astype(BF16)
    qt = (qvt[:W] * scale).astype(BF16)
    dqk = ATTN_QK_DIM
    zeros = jnp.zeros((dqk, qt.shape[1]), BF16)
    for h in range(ATTN_HEADS):
        top = qt[2 * dqk * h:2 * dqk * h + dqk]
        bot = qt[2 * dqk * h + dqk:2 * dqk * (h + 1)]
        q_ref[0, h, 0, 0] = jnp.concatenate([top, zeros], axis=0)
        q_ref[0, h, 0, 1] = jnp.concatenate([zeros, bot], axis=0)
    xr_ref[0] = proj(3)
    gr_ref[0] = proj(4)


def _ffn_in(x, ng1, sh1, sc1, g1, w1, w3, w2, ng2, sh2, sc2, w_in_b, w_qvt):
    B, S, D = x.shape
    tm, T = FFN_TM, ATTN_T
    W = w_qvt.shape[0] // 2
    H, dqk = ATTN_HEADS, ATTN_QK_DIM
    per_t = T // tm
    vec = pl.BlockSpec((1, 1, D), lambda b, i: (b, 0, 0))
    row = _const_spec((1, D))
    full = pl.BlockSpec((1, tm, D), lambda b, i: (b, i, 0))
    tok = pl.BlockSpec((1, tm, W), lambda b, i: (b, i, 0))
    vt_spec = pl.BlockSpec((1, 1, W, tm), lambda b, i: (b, i // per_t, 0, i % per_t))
    q_spec = pl.BlockSpec((1, H, 1, 2, 2 * dqk, tm),
                          lambda b, i: (b, 0, i // per_t, 0, 0, i % per_t))
    return pl.pallas_call(
        _ffn_in_kernel,
        grid=(B, S // tm),
        in_specs=[full, row, vec, vec, vec,
                  _const_spec(w1.shape), _const_spec(w3.shape), _const_spec(w2.shape),
                  row, vec, vec, _const_spec(w_in_b.shape), _const_spec(w_qvt.shape)],
        out_specs=[full, q_spec, tok, vt_spec, tok, tok],
        out_shape=[jax.ShapeDtypeStruct((B, S, D), F32),
                   jax.ShapeDtypeStruct((B, H, S // T, 2, 2 * dqk, T), BF16),
                   jax.ShapeDtypeStruct((B, S, W), BF16),
                   jax.ShapeDtypeStruct((B, S // T, W, T), BF16)]
                  + [jax.ShapeDtypeStruct((B, S, W), F32)] * 2,
        scratch_shapes=[pltpu.VMEM((tm, D), F32)],
        compiler_params=pltpu.CompilerParams(
            dimension_semantics=("parallel", "parallel"), vmem_limit_bytes=VMEM_LIMIT),
        name="ffn_in_proj",
    )(x, ng1.reshape(1, D), sh1, sc1, g1, w1, w3, w2, ng2.reshape(1, D), sh2, sc2,
      w_in_b, w_qvt)


def _out_ffn_kernel(x_ref, o_ref, yr_ref, g2_ref, wout_ref, ng_ref, sh_ref, sc_ref, g_ref,
                    w1_ref, w3_ref, w2_ref, *rest, final_norm):
    if final_norm:
        fg_ref, out_ref, acc_ref = rest
    else:
        out_ref, acc_ref = rest
    mix = (jnp.dot(o_ref[0], wout_ref[0], preferred_element_type=F32)
           + jnp.dot(yr_ref[0], wout_ref[1], preferred_element_type=F32))
    x = x_ref[0] + g2_ref[0] * mix
    y = _swiglu_residual(x, _modulated(x, ng_ref, sh_ref, sc_ref),
                         g_ref, w1_ref, w3_ref, w2_ref, acc_ref)
    if final_norm:
        y = _rms(y) * fg_ref[...]
    out_ref[0] = y


def _out_ffn(x, o, yr, g2, w_out_c, ng, sh, sc, g, w1, w3, w2, final_g=None):
    B, S, D = x.shape
    tm = FFN2_TM
    W = o.shape[-1]
    final_norm = final_g is not None
    vec = pl.BlockSpec((1, 1, D), lambda b, i: (b, 0, 0))
    row = _const_spec((1, D))
    full = pl.BlockSpec((1, tm, D), lambda b, i: (b, i, 0))
    half = pl.BlockSpec((1, tm, W), lambda b, i: (b, i, 0))
    in_specs = [full, half, half, vec, _const_spec(w_out_c.shape), row, vec, vec, vec,
                _const_spec(w1.shape), _const_spec(w3.shape), _const_spec(w2.shape)]
    args = [x, o, yr, g2, w_out_c, ng.reshape(1, D), sh, sc, g, w1, w3, w2]
    if final_norm:
        in_specs.append(row)
        args.append(final_g.reshape(1, D))
    return pl.pallas_call(
        functools.partial(_out_ffn_kernel, final_norm=final_norm),
        grid=(B, S // tm),
        in_specs=in_specs,
        out_specs=full,
        out_shape=jax.ShapeDtypeStruct((B, S, D), F32),
        scratch_shapes=[pltpu.VMEM((tm, D), F32)],
        compiler_params=pltpu.CompilerParams(
            dimension_semantics=("parallel", "parallel"), vmem_limit_bytes=VMEM_LIMIT),
        name="out_proj_ffn_final" if final_norm else "out_proj_ffn",
    )(*args)


def _bias_kernel(rb_ref, o_ref):
    mp = pl.program_id(0)
    T = o_ref.shape[-1]
    key = lax.broadcasted_iota(jnp.int32, (T, T), 0)
    qry = lax.broadcasted_iota(jnp.int32, (T, T), 1)
    max_exact = N_BUCKETS // 2
    last = rb_ref[N_BUCKETS - 1, mp]
    for which in range(2):
        rel = qry - key + (T if which == 0 else 0)
        n = jnp.maximum(rel, 0)
        nf = jnp.maximum(n, 1).astype(F32)
        large = max_exact + (jnp.log(nf / max_exact) / math.log(MAX_DISTANCE / max_exact)
                             * (N_BUCKETS - max_exact)).astype(jnp.int32)
        large = jnp.minimum(large, N_BUCKETS - 1)
        bucket = jnp.where(n < max_exact, n, large)
        val = jnp.zeros((T, T), F32)
        for bk in range(N_BUCKETS - 1):
            val = jnp.where(bucket == bk, (rb_ref[bk, mp] - last) * LOG2E, val)
        if which == 1:
            val = jnp.where(rel >= 0, val, NEG_BIG)
        o_ref[0, which] = val


def _bias_tiles(rel_bias, T):
    n_maps = rel_bias.shape[1]
    return pl.pallas_call(
        _bias_kernel,
        grid=(n_maps,),
        in_specs=[pl.BlockSpec(memory_space=pltpu.SMEM)],
        out_specs=pl.BlockSpec((1, 2, T, T), lambda m: (m, 0, 0, 0)),
        out_shape=jax.ShapeDtypeStruct((n_maps, 2, T, T), F32),
        name="rel_bias_tiles",
    )(rel_bias)


def _attn_kernel(q_ref, qn_ref, k_ref, vt_ref, bias_ref, lq1_ref, lk1_ref, lq2_ref, lk2_ref,
                 sg_ref, o_ref, m_ref, l_ref, acc_ref, s_ref, *, lambda_init):
    T = q_ref.shape[-1]
    qi = pl.program_id(2)
    q_refs = (q_ref, qn_ref)

    m_ref[...] = jnp.full_like(m_ref, NEG_BIG)
    l_ref[...] = jnp.zeros_like(l_ref)
    acc_ref[...] = jnp.zeros_like(acc_ref)

    strips = [slice(c * ATTN_STRIP, (c + 1) * ATTN_STRIP)
              for c in range(2 * T // ATTN_STRIP)]

    def n_keys(which, c):
        return strips[c].start % T + ATTN_STRIP if which == 1 else T

    def put_scores(buf, j, which, c, slot=0):
        nk = n_keys(which, c)
        start = j * T if isinstance(j, int) else pl.multiple_of(j * T, T)
        kj = k_ref[0, pl.ds(start, nk), :]
        mp, q0 = divmod(strips[c].start, T)
        qt = q_refs[slot][0, 0, 0, mp, :, q0:q0 + ATTN_STRIP]
        s_ref[buf, :nk, strips[c]] = jnp.dot(kj, qt, preferred_element_type=F32)

    def biased(st, which, mp, q0):
        rows = []
        for a in range(st.shape[0] // BIAS_T):
            cols = []
            for b in range(q0 // BIAS_T, (q0 + ATTN_STRIP) // BIAS_T):
                lo = (b - q0 // BIAS_T) * BIAS_T
                piece = st[a * BIAS_T:(a + 1) * BIAS_T, lo:lo + BIAS_T]
                dist = b - a + (T // BIAS_T if which == 0 else 0)
                if dist < 0:
                    piece = jnp.full_like(piece, NEG_BIG)
                elif dist <= 1:
                    piece = piece + bias_ref[0, mp, 1 - dist]
                cols.append(piece)
            rows.append(jnp.concatenate(cols, axis=1))
        return jnp.concatenate(rows, axis=0)

    def step(buf, j, which, nxt, hold=0):
        early = len(strips) - hold
        for c in range(ATTN_LOOKAHEAD):
            put_scores(*nxt[:3], c, nxt[3])
        for c, sl in enumerate(strips):
            if c + ATTN_LOOKAHEAD < early:
                put_scores(*nxt[:3], c + ATTN_LOOKAHEAD, nxt[3])
            nk = n_keys(which, c)
            st = s_ref[buf, :nk, sl]
            if which is not None:
                st = biased(st, which, *divmod(sl.start, T))
            m_old = m_ref[:, sl]
            m_new = jnp.maximum(m_old, jnp.max(st, axis=0, keepdims=True))
            alpha = jnp.exp2(m_old - m_new)
            p = jnp.exp2(st - m_new)
            l_ref[:, sl] = alpha * l_ref[:, sl] + jnp.sum(p, axis=0, keepdims=True)
            acc_ref[:, sl] = alpha * acc_ref[:, sl] + jnp.dot(
                vt_ref[0, j, :, :nk], p.astype(BF16), preferred_element_type=F32)
            m_ref[:, sl] = m_new
        for c in range(early, len(strips)):
            put_scores(*nxt[:3], c, nxt[3])

    def last_step(buf):
        step(buf, qi, 1, (0, 0, None, 1), hold=ATTN_HOLD)
        lam = (jnp.exp(jnp.sum(lq1_ref[...] * lk1_ref[...], axis=-1, keepdims=True))
               - jnp.exp(jnp.sum(lq2_ref[...] * lk2_ref[...], axis=-1, keepdims=True))
               + lambda_init)
        o_all = acc_ref[...] / l_ref[...]
        ot = o_all[:, :T] - lam * o_all[:, T:]
        ot = ot * lax.rsqrt(jnp.mean(ot * ot, axis=0, keepdims=True) + NORM_EPS)
        o_ref[0] = (ot.T * sg_ref[...] * (1.0 - lambda_init)).astype(o_ref.dtype)

    @pl.when(qi == 0)
    def _():
        for c in range(len(strips)):
            put_scores(2, 0, 1, c)
        last_step(2)

    @pl.when(qi >= 1)
    def _():
        n_far = qi - 1

        def pair(i, carry):
            j = 2 * i
            step(0, j, None, (1, j + 1, None, 0))
            step(1, j + 1, None, (0, j + 2, None, 0))
            return carry

        lax.fori_loop(0, n_far // 2, pair, 0)

        @pl.when(n_far % 2 == 0)
        def _():
            step(0, qi - 1, 0, (1, qi, 1, 0))
            last_step(1)

        @pl.when(n_far % 2 == 1)
        def _():
            step(0, qi - 2, None, (1, qi - 1, None, 0))
            step(1, qi - 1, 0, (2, qi, 1, 0))
            last_step(2)


def _attention(q, k, vt, bias, lq1, lk1, lq2, lk2, subln_g, lambda_init):
    B, S, _ = k.shape
    T = ATTN_T
    H, dv, dqk = ATTN_HEADS, ATTN_V_DIM, ATTN_QK_DIM
    bias = bias.reshape(H, 2, 2, BIAS_T, BIAS_T)
    small = lambda n: _const_spec((1, n))
    return pl.pallas_call(
        functools.partial(_attn_kernel, lambda_init=lambda_init),
        grid=(B, H, S // T),
        in_specs=[pl.BlockSpec((1, 1, 1, 2, 2 * dqk, T), lambda b, h, i: (b, h, i, 0, 0, 0)),
                  pl.BlockSpec((1, 1, 1, 2, 2 * dqk, T),
                               lambda b, h, i: (b, h, jnp.minimum(i + 1, S // T - 1), 0, 0, 0)),
                  pl.BlockSpec((1, S, 2 * dqk), lambda b, h, i: (b, 0, h)),
                  pl.BlockSpec((1, S // T, dv, T), lambda b, h, i: (b, 0, h, 0)),
                  pl.BlockSpec((1, 2, 2, BIAS_T, BIAS_T), lambda b, h, i: (h, 0, 0, 0, 0)),
                  small(dqk), small(dqk), small(dqk), small(dqk), small(dv)],
        out_specs=pl.BlockSpec((1, T, dv), lambda b, h, i: (b, i, h)),
        out_shape=jax.ShapeDtypeStruct((B, S, H * dv), BF16),
        scratch_shapes=[pltpu.VMEM((1, 2 * T), F32),
                        pltpu.VMEM((1, 2 * T), F32),
                        pltpu.VMEM((dv, 2 * T), F32),
                        pltpu.VMEM((3, T, 2 * T), F32)],
        compiler_params=pltpu.CompilerParams(
            dimension_semantics=("parallel", "parallel", "arbitrary"),
            vmem_limit_bytes=VMEM_LIMIT),
        name="diff_attn",
    )(q, q, k, vt, bias, lq1.reshape(1, dqk), lk1.reshape(1, dqk), lq2.reshape(1, dqk),
      lk2.reshape(1, dqk), subln_g.reshape(1, dv))


def _rglru_kernel(xr_ref, gr_ref, cw_ref, cb_ref, wg_ref, ba_ref, bi_ref, L_ref,
                  y_ref, xpad_ref, h_ref, a_buf, b_buf):
    Tc, C = xr_ref.shape[1], xr_ref.shape[2]
    G = Tc // SUBLANES

    @pl.when(pl.program_id(1) == 0)
    def _():
        xpad_ref[:SUBLANES] = jnp.zeros((SUBLANES, C), F32)
        h_ref[...] = jnp.zeros_like(h_ref)

    x = xr_ref[0]
    xpad_ref[SUBLANES:] = x
    xc = x * cw_ref[CONV_WIDTH - 1:CONV_WIDTH, :] + cb_ref[...]
    for k in range(1, CONV_WIDTH):
        xc = xc + (xpad_ref[pl.ds(SUBLANES - k, Tc), :]
                   * cw_ref[CONV_WIDTH - 1 - k:CONV_WIDTH - k, :])
    xpad_ref[:SUBLANES] = x[Tc - SUBLANES:]

    gates = jnp.dot(xc.astype(BF16), wg_ref[...], preferred_element_type=F32)
    r = jax.nn.sigmoid(gates[:, :C] + ba_ref[...])
    ig = jax.nn.sigmoid(gates[:, C:] + bi_ref[...])
    L = L_ref[...]
    log_sig = jnp.minimum(L, 0.0) - jnp.log1p(jnp.exp(-jnp.abs(L)))
    log_a = r * (RG_LRU_C * log_sig)
    a = jnp.exp(log_a)
    v = -jnp.tanh(log_a) * (a * a + 1.0)
    root = jnp.where(v > 0.0, v * lax.rsqrt(v), 0.0)
    u = root * (ig * xc)

    a3 = a.reshape(G, SUBLANES, C)
    u3 = u.reshape(G, SUBLANES, C)
    sub3 = lax.broadcasted_iota(jnp.int32, (G, SUBLANES, C), 1)
    for k in (1, 2, 4):
        a_sh = pltpu.roll(a3, k, 1)
        u_sh = pltpu.roll(u3, k, 1)
        ok = sub3 >= k
        u3 = jnp.where(ok, a3 * u_sh + u3, u3)
        a3 = jnp.where(ok, a3 * a_sh, a3)
    a_buf[...] = a3
    b_buf[...] = u3

    def body(g, hprev):
        hg = a_buf[g] * hprev + b_buf[g]
        b_buf[g] = hg
        return hg[SUBLANES - 1:SUBLANES, :]

    h_last = lax.fori_loop(0, G, body, h_ref[...], unroll=8)
    h_ref[...] = h_last

    h = b_buf[...].reshape(Tc, C)
    gr = gr_ref[0]
    c0 = math.sqrt(2.0 / math.pi)
    t = jnp.tanh(gr * (c0 + (c0 * 0.044715) * (gr * gr)))
    y_ref[0] = (h * (gr * (0.5 + 0.5 * t))).astype(y_ref.dtype)


def _rglru(xr, gr, conv_w, conv_b, w_gates, b_a, b_i, lru_L):
    B, S, C = xr.shape
    Tc = RNN_TC
    tok = pl.BlockSpec((1, Tc, C), lambda b, t: (b, t, 0))
    row = lambda: _const_spec((1, C))
    return pl.pallas_call(
        _rglru_kernel,
        grid=(B, S // Tc),
        in_specs=[tok, tok, _const_spec((CONV_WIDTH, C)), row(),
                  _const_spec(w_gates.shape), row(), row(), row()],
        out_specs=tok,
        out_shape=jax.ShapeDtypeStruct((B, S, C), BF16),
        scratch_shapes=[pltpu.VMEM((Tc + SUBLANES, C), F32),
                        pltpu.VMEM((1, C), F32),
                        pltpu.VMEM((Tc // SUBLANES, SUBLANES, C), F32),
                        pltpu.VMEM((Tc // SUBLANES, SUBLANES, C), F32)],
        compiler_params=pltpu.CompilerParams(
            dimension_semantics=("parallel", "arbitrary"), vmem_limit_bytes=VMEM_LIMIT),
        name="rg_lru",
    )(xr, gr, conv_w, conv_b.reshape(1, C), w_gates, b_a.reshape(1, C),
      b_i.reshape(1, C), lru_L.reshape(1, C))


def _ffn_weights(w1, w3, w2):
    return w1.astype(BF16), w3.astype(BF16), w2.astype(BF16)


def _block_diag(w):
    n, bw, _ = w.shape
    eye = jnp.eye(n, dtype=w.dtype)
    return (eye[:, None, :, None] * w[:, :, None, :]).reshape(n * bw, n * bw)


def kernel(x, c, rel_bias, ada_w, ada_b, norm_g, ffn1_w1, ffn1_w3, ffn1_w2, w_in, lam_q1, lam_k1, lam_q2, lam_k2, subln_g, conv_w, conv_b, gate_a_w, gate_a_b, gate_i_w, gate_i_b, lru_L, w_out, ffn2_w1, ffn2_w3, ffn2_w2, final_g):
    B, S, D = x.shape
    depth = ada_w.shape[0]
    c_pad = jnp.zeros((SUBLANES, D), F32).at[:B].set(c)
    bias = _bias_tiles(rel_bias, BIAS_T)
    for l in range(depth):
        mod = _modulation(c_pad, ada_w[l], ada_b[l])[:, :B]
        sh1, sc1, g1, sh2, sc2, g2, sh3, sc3, g3 = [m.reshape(B, 1, D) for m in mod]

        w_in_b = w_in[l].astype(BF16)
        w_qvt = jnp.concatenate([w_in_b[:, :ATTN_WIDTH],
                                 w_in_b[:, 2 * ATTN_WIDTH:3 * ATTN_WIDTH]], axis=1).T
        x, q, k, vt, xr, gr = _ffn_in(
            x, norm_g[l, 0], sh1, sc1, g1,
            *_ffn_weights(ffn1_w1[l], ffn1_w3[l], ffn1_w2[l]),
            norm_g[l, 1], sh2, sc2, w_in_b, w_qvt)

        lambda_init = 0.8 - 0.6 * math.exp(-0.3 * l)
        o = _attention(q, k, vt, bias, lam_q1[l], lam_k1[l], lam_q2[l], lam_k2[l],
                       subln_g[l], lambda_init)

        w_gates = jnp.concatenate(
            [_block_diag(gate_a_w[l]), _block_diag(gate_i_w[l])], axis=1).astype(BF16)
        yr = _rglru(xr, gr, conv_w[l], conv_b[l], w_gates, gate_a_b[l], gate_i_b[l],
                    lru_L[l])

        w_out_c = w_out[l].astype(BF16).reshape(2, ATTN_WIDTH, D)
        last = l == depth - 1
        x = _out_ffn(x, o, yr, g2, w_out_c, norm_g[l, 2], sh3, sc3, g3,
                     *_ffn_weights(ffn2_w1[l], ffn2_w3[l], ffn2_w2[l]),
                     final_g=final_g if last else None)
    return x
```

```python
import functools
import math

import jax
import jax.numpy as jnp
from jax import lax
from jax.experimental import pallas as pl
from jax.experimental.pallas import tpu as pltpu

F32 = jnp.float32
BF16 = jnp.bfloat16

ATTN_HEADS = 4
ATTN_QK_DIM = 64
ATTN_V_DIM = 2 * ATTN_QK_DIM
ATTN_WIDTH = ATTN_HEADS * ATTN_V_DIM
RNN_BLOCKS = 8
CONV_WIDTH = 4
RG_LRU_C = 8.0
N_BUCKETS = 32
MAX_DISTANCE = 128
NORM_EPS = 1e-6
N_MOD = 9

SUBLANES = 8
VMEM_LIMIT = 56 * 1024 * 1024

FFN_TM = 512
FFN_FC = 256
ATTN_T = 1024
ATTN_STRIP = 256
ATTN_LOOKAHEAD = 1
ATTN_HOLD = 3
BIAS_T = 128
RNN_TC = 1024
CAST_ROWS = 256
NEG_BIG = -1e30
LOG2E = math.log2(math.e)


def _rms(x):
    return x * lax.rsqrt(jnp.mean(x * x, axis=-1, keepdims=True) + NORM_EPS)


def _const_spec(shape):
    nd = len(shape)
    return pl.BlockSpec(shape, lambda *_: (0,) * nd, pipeline_mode=pl.Buffered(1))


def _mod_kernel(c_ref, w_ref, b_ref, o_ref):
    c = c_ref[...]
    ca = c * jax.nn.sigmoid(c)
    o_ref[0] = jnp.dot(ca, w_ref[...], preferred_element_type=F32,
                       precision=lax.Precision.HIGHEST) + b_ref[0]


def _modulation(c_pad, ada_w, ada_b):
    rows, d = c_pad.shape
    n = ada_w.shape[1] // d
    return pl.pallas_call(
        _mod_kernel,
        grid=(n,),
        in_specs=[pl.BlockSpec((rows, d), lambda j: (0, 0)),
                  pl.BlockSpec((d, d), lambda j: (0, j)),
                  pl.BlockSpec((1, 1, d), lambda j: (j, 0, 0))],
        out_specs=pl.BlockSpec((1, rows, d), lambda j: (j, 0, 0)),
        out_shape=jax.ShapeDtypeStruct((n, rows, d), F32),
        compiler_params=pltpu.CompilerParams(vmem_limit_bytes=VMEM_LIMIT),
        name="adaln_mod",
    )(c_pad, ada_w, ada_b.reshape(n, 1, d))


def _modulated(x, ng_ref, sh_ref, sc_ref):
    return (_rms(x) * ng_ref[...] * (1.0 + sc_ref[0]) + sh_ref[0]).astype(BF16)


def _swiglu_residual(x, hb, g_ref, w1_ref, w3_ref, w2_ref, acc_ref):
    n_chunks = w1_ref.shape[1] // FFN_FC
    cols = lambda ci: slice(ci * FFN_FC, (ci + 1) * FFN_FC)
    up = lambda ci: (jnp.dot(hb, w1_ref[:, cols(ci)], preferred_element_type=F32),
                     jnp.dot(hb, w3_ref[:, cols(ci)], preferred_element_type=F32))
    a, b = up(0)
    for ci in range(n_chunks):
        if ci + 1 < n_chunks:
            a_next, b_next = up(ci + 1)
        u = (a * jax.nn.sigmoid(a) * b).astype(BF16)
        down = jnp.dot(u, w2_ref[cols(ci), :], preferred_element_type=F32)
        if ci == 0:
            acc_ref[...] = down
        else:
            acc_ref[...] += down
        a, b = a_next, b_next
    return x + 0.5 * g_ref[0] * acc_ref[...]


def _ffn_in_kernel(x_ref, ng1_ref, sh1_ref, sc1_ref, g1_ref, w1_ref, w3_ref, w2_ref,
                   ng2_ref, sh2_ref, sc2_ref, win_ref, wqvt_ref,
                   x1_ref, q_ref, k_ref, vt_ref, xr_ref, gr_ref, acc_ref):
    x = x_ref[0]
    y = _swiglu_residual(x, _modulated(x, ng1_ref, sh1_ref, sc1_ref),
                         g1_ref, w1_ref, w3_ref, w2_ref, acc_ref)
    x1_ref[0] = y
    hb = _modulated(y, ng2_ref, sh2_ref, sc2_ref)
    scale = ATTN_QK_DIM ** -0.5 * LOG2E
    W = k_ref.shape[-1]
    proj = lambda n: jnp.dot(hb, win_ref[:, n * W:(n + 1) * W], preferred_element_type=F32)
    k_ref[0] = proj(1).astype(BF16)
    qvt = lax.dot_general(wqvt_ref[...], hb, (((1,), (1,)), ((), ())),
                          preferred_element_type=F32)
    vt_ref[0, 0] = qvt[W:].astype(BF16)
    qt = (qvt[:W] * scale).astype(BF16)
    dqk = ATTN_QK_DIM
    zeros = jnp.zeros((dqk, qt.shape[1]), BF16)
    for h in range(ATTN_HEADS):
        top = qt[2 * dqk * h:2 * dqk * h + dqk]
        bot = qt[2 * dqk * h + dqk:2 * dqk * (h + 1)]
        q_ref[0, h, 0, 0] = jnp.concatenate([top, zeros], axis=0)
        q_ref[0, h, 0, 1] = jnp.concatenate([zeros, bot], axis=0)
    xr_ref[0] = proj(3)
    gr_ref[0] = proj(4)


def _ffn_in(x, ng1, sh1, sc1, g1, w1, w3, w2, ng2, sh2, sc2, w_in_b, w_qvt):
    B, S, D = x.shape
    tm, T = FFN_TM, ATTN_T
    W = w_qvt.shape[0] // 2
    H, dqk = ATTN_HEADS, ATTN_QK_DIM
    per_t = T // tm
    vec = pl.BlockSpec((1, 1, D), lambda b, i: (b, 0, 0))
    row = _const_spec((1, D))
    full = pl.BlockSpec((1, tm, D), lambda b, i: (b, i, 0))
    tok = pl.BlockSpec((1, tm, W), lambda b, i: (b, i, 0))
    vt_spec = pl.BlockSpec((1, 1, W, tm), lambda b, i: (b, i // per_t, 0, i % per_t))
    q_spec = pl.BlockSpec((1, H, 1, 2, 2 * dqk, tm),
                          lambda b, i: (b, 0, i // per_t, 0, 0, i % per_t))
    return pl.pallas_call(
        _ffn_in_kernel,
        grid=(B, S // tm),
        in_specs=[full, row, vec, vec, vec,
                  _const_spec(w1.shape), _const_spec(w3.shape), _const_spec(w2.shape),
                  row, vec, vec, _const_spec(w_in_b.shape), _const_spec(w_qvt.shape)],
        out_specs=[full, q_spec, tok, vt_spec, tok, tok],
        out_shape=[jax.ShapeDtypeStruct((B, S, D), F32),
                   jax.ShapeDtypeStruct((B, H, S // T, 2, 2 * dqk, T), BF16),
                   jax.ShapeDtypeStruct((B, S, W), BF16),
                   jax.ShapeDtypeStruct((B, S // T, W, T), BF16)]
                  + [jax.ShapeDtypeStruct((B, S, W), F32)] * 2,
        scratch_shapes=[pltpu.VMEM((tm, D), F32)],
        compiler_params=pltpu.CompilerParams(
            dimension_semantics=("parallel", "parallel"), vmem_limit_bytes=VMEM_LIMIT),
        name="ffn_in_proj",
    )(x, ng1.reshape(1, D), sh1, sc1, g1, w1, w3, w2, ng2.reshape(1, D), sh2, sc2,
      w_in_b, w_qvt)


def _out_ffn_kernel(x_ref, o_ref, yr_ref, g2_ref, wout_ref, ng_ref, sh_ref, sc_ref, g_ref,
                    w1_ref, w3_ref, w2_ref, *rest, final_norm):
    if final_norm:
        fg_ref, out_ref, acc_ref = rest
    else:
        out_ref, acc_ref = rest
    mix = (jnp.dot(o_ref[0], wout_ref[0], preferred_element_type=F32)
           + jnp.dot(yr_ref[0], wout_ref[1], preferred_element_type=F32))
    x = x_ref[0] + g2_ref[0] * mix
    y = _swiglu_residual(x, _modulated(x, ng_ref, sh_ref, sc_ref),
                         g_ref, w1_ref, w3_ref, w2_ref, acc_ref)
    if final_norm:
        y = _rms(y) * fg_ref[...]
    out_ref[0] = y


def _out_ffn(x, o, yr, g2, w_out_c, ng, sh, sc, g, w1, w3, w2, final_g=None):
    B, S, D = x.shape
    tm = FFN_TM
    W = o.shape[-1]
    final_norm = final_g is not None
    vec = pl.BlockSpec((1, 1, D), lambda b, i: (b, 0, 0))
    row = _const_spec((1, D))
    full = pl.BlockSpec((1, tm, D), lambda b, i: (b, i, 0))
    half = pl.BlockSpec((1, tm, W), lambda b, i: (b, i, 0))
    in_specs = [full, half, half, vec, _const_spec(w_out_c.shape), row, vec, vec, vec,
                _const_spec(w1.shape), _const_spec(w3.shape), _const_spec(w2.shape)]
    args = [x, o, yr, g2, w_out_c, ng.reshape(1, D), sh, sc, g, w1, w3, w2]
    if final_norm:
        in_specs.append(row)
        args.append(final_g.reshape(1, D))
    return pl.pallas_call(
        functools.partial(_out_ffn_kernel, final_norm=final_norm),
        grid=(B, S // tm),
        in_specs=in_specs,
        out_specs=full,
        out_shape=jax.ShapeDtypeStruct((B, S, D), F32),
        scratch_shapes=[pltpu.VMEM((tm, D), F32)],
        compiler_params=pltpu.CompilerParams(
            dimension_semantics=("parallel", "parallel"), vmem_limit_bytes=VMEM_LIMIT),
        name="out_proj_ffn_final" if final_norm else "out_proj_ffn",
    )(*args)


def _bias_kernel(rb_ref, o_ref):
    mp = pl.program_id(0)
    T = o_ref.shape[-1]
    key = lax.broadcasted_iota(jnp.int32, (T, T), 0)
    qry = lax.broadcasted_iota(jnp.int32, (T, T), 1)
    max_exact = N_BUCKETS // 2
    last = rb_ref[N_BUCKETS - 1, mp]
    for which in range(2):
        rel = qry - key + (T if which == 0 else 0)
        n = jnp.maximum(rel, 0)
        nf = jnp.maximum(n, 1).astype(F32)
        large = max_exact + (jnp.log(nf / max_exact) / math.log(MAX_DISTANCE / max_exact)
                             * (N_BUCKETS - max_exact)).astype(jnp.int32)
        large = jnp.minimum(large, N_BUCKETS - 1)
        bucket = jnp.where(n < max_exact, n, large)
        val = jnp.zeros((T, T), F32)
        for bk in range(N_BUCKETS - 1):
            val = jnp.where(bucket == bk, (rb_ref[bk, mp] - last) * LOG2E, val)
        if which == 1:
            val = jnp.where(rel >= 0, val, NEG_BIG)
        o_ref[0, which] = val


def _bias_tiles(rel_bias, T):
    n_maps = rel_bias.shape[1]
    return pl.pallas_call(
        _bias_kernel,
        grid=(n_maps,),
        in_specs=[pl.BlockSpec(memory_space=pltpu.SMEM)],
        out_specs=pl.BlockSpec((1, 2, T, T), lambda m: (m, 0, 0, 0)),
        out_shape=jax.ShapeDtypeStruct((n_maps, 2, T, T), F32),
        name="rel_bias_tiles",
    )(rel_bias)


def _attn_kernel(q_ref, qn_ref, k_ref, vt_ref, bias_ref, lq1_ref, lk1_ref, lq2_ref, lk2_ref,
                 sg_ref, o_ref, m_ref, l_ref, acc_ref, s_ref, *, lambda_init):
    T = q_ref.shape[-1]
    qi = pl.program_id(2)
    q_refs = (q_ref, qn_ref)

    m_ref[...] = jnp.full_like(m_ref, NEG_BIG)
    l_ref[...] = jnp.zeros_like(l_ref)
    acc_ref[...] = jnp.zeros_like(acc_ref)

    strips = [slice(c * ATTN_STRIP, (c + 1) * ATTN_STRIP)
              for c in range(2 * T // ATTN_STRIP)]

    def n_keys(which, c):
        return strips[c].start % T + ATTN_STRIP if which == 1 else T

    def put_scores(buf, j, which, c, slot=0):
        nk = n_keys(which, c)
        start = j * T if isinstance(j, int) else pl.multiple_of(j * T, T)
        kj = k_ref[0, pl.ds(start, nk), :]
        mp, q0 = divmod(strips[c].start, T)
        qt = q_refs[slot][0, 0, 0, mp, :, q0:q0 + ATTN_STRIP]
        s_ref[buf, :nk, strips[c]] = jnp.dot(kj, qt, preferred_element_type=F32)

    def biased(st, which, mp, q0):
        rows = []
        for a in range(st.shape[0] // BIAS_T):
            cols = []
            for b in range(q0 // BIAS_T, (q0 + ATTN_STRIP) // BIAS_T):
                lo = (b - q0 // BIAS_T) * BIAS_T
                piece = st[a * BIAS_T:(a + 1) * BIAS_T, lo:lo + BIAS_T]
                dist = b - a + (T // BIAS_T if which == 0 else 0)
                if dist < 0:
                    piece = jnp.full_like(piece, NEG_BIG)
                elif dist <= 1:
                    piece = piece + bias_ref[0, mp, 1 - dist]
                cols.append(piece)
            rows.append(jnp.concatenate(cols, axis=1))
        return jnp.concatenate(rows, axis=0)

    def step(buf, j, which, nxt, hold=0):
        early = len(strips) - hold
        for c in range(ATTN_LOOKAHEAD):
            put_scores(*nxt[:3], c, nxt[3])
        for c, sl in enumerate(strips):
            if c + ATTN_LOOKAHEAD < early:
                put_scores(*nxt[:3], c + ATTN_LOOKAHEAD, nxt[3])
            nk = n_keys(which, c)
            st = s_ref[buf, :nk, sl]
            if which is not None:
                st = biased(st, which, *divmod(sl.start, T))
            m_old = m_ref[:, sl]
            m_new = jnp.maximum(m_old, jnp.max(st, axis=0, keepdims=True))
            alpha = jnp.exp2(m_old - m_new)
            p = jnp.exp2(st - m_new)
            l_ref[:, sl] = alpha * l_ref[:, sl] + jnp.sum(p, axis=0, keepdims=True)
            acc_ref[:, sl] = alpha * acc_ref[:, sl] + jnp.dot(
                vt_ref[0, j, :, :nk], p.astype(BF16), preferred_element_type=F32)
            m_ref[:, sl] = m_new
        for c in range(early, len(strips)):
            put_scores(*nxt[:3], c, nxt[3])

    def last_step(buf):
        step(buf, qi, 1, (0, 0, None, 1), hold=ATTN_HOLD)
        lam = (jnp.exp(jnp.sum(lq1_ref[...] * lk1_ref[...], axis=-1, keepdims=True))
               - jnp.exp(jnp.sum(lq2_ref[...] * lk2_ref[...], axis=-1, keepdims=True))
               + lambda_init)
        o_all = acc_ref[...] / l_ref[...]
        ot = o_all[:, :T] - lam * o_all[:, T:]
        ot = ot * lax.rsqrt(jnp.mean(ot * ot, axis=0, keepdims=True) + NORM_EPS)
        o_ref[0] = (ot.T * sg_ref[...] * (1.0 - lambda_init)).astype(o_ref.dtype)

    @pl.when(qi == 0)
    def _():
        for c in range(len(strips)):
            put_scores(2, 0, 1, c)
        last_step(2)

    @pl.when(qi >= 1)
    def _():
        n_far = qi - 1

        def pair(i, carry):
            j = 2 * i
            step(0, j, None, (1, j + 1, None, 0))
            step(1, j + 1, None, (0, j + 2, None, 0))
            return carry

        lax.fori_loop(0, n_far // 2, pair, 0)

        @pl.when(n_far % 2 == 0)
        def _():
            step(0, qi - 1, 0, (1, qi, 1, 0))
            last_step(1)

        @pl.when(n_far % 2 == 1)
        def _():
            step(0, qi - 2, None, (1, qi - 1, None, 0))
            step(1, qi - 1, 0, (2, qi, 1, 0))
            last_step(2)


def _attention(q, k, vt, bias, lq1, lk1, lq2, lk2, subln_g, lambda_init):
    B, S, _ = k.shape
    T = ATTN_T
    H, dv, dqk = ATTN_HEADS, ATTN_V_DIM, ATTN_QK_DIM
    bias = bias.reshape(H, 2, 2, BIAS_T, BIAS_T)
    small = lambda n: _const_spec((1, n))
    return pl.pallas_call(
        functools.partial(_attn_kernel, lambda_init=lambda_init),
        grid=(B, H, S // T),
        in_specs=[pl.BlockSpec((1, 1, 1, 2, 2 * dqk, T), lambda b, h, i: (b, h, i, 0, 0, 0)),
                  pl.BlockSpec((1, 1, 1, 2, 2 * dqk, T),
                               lambda b, h, i: (b, h, jnp.minimum(i + 1, S // T - 1), 0, 0, 0)),
                  pl.BlockSpec((1, S, 2 * dqk), lambda b, h, i: (b, 0, h)),
                  pl.BlockSpec((1, S // T, dv, T), lambda b, h, i: (b, 0, h, 0)),
                  pl.BlockSpec((1, 2, 2, BIAS_T, BIAS_T), lambda b, h, i: (h, 0, 0, 0, 0)),
                  small(dqk), small(dqk), small(dqk), small(dqk), small(dv)],
        out_specs=pl.BlockSpec((1, T, dv), lambda b, h, i: (b, i, h)),
        out_shape=jax.ShapeDtypeStruct((B, S, H * dv), BF16),
        scratch_shapes=[pltpu.VMEM((1, 2 * T), F32),
                        pltpu.VMEM((1, 2 * T), F32),
                        pltpu.VMEM((dv, 2 * T), F32),
                        pltpu.VMEM((3, T, 2 * T), F32)],
        compiler_params=pltpu.CompilerParams(
            dimension_semantics=("parallel", "parallel", "arbitrary"),
            vmem_limit_bytes=VMEM_LIMIT),
        name="diff_attn",
    )(q, q, k, vt, bias, lq1.reshape(1, dqk), lk1.reshape(1, dqk), lq2.reshape(1, dqk),
      lk2.reshape(1, dqk), subln_g.reshape(1, dv))


def _rglru_kernel(xr_ref, gr_ref, cw_ref, cb_ref, wg_ref, ba_ref, bi_ref, L_ref,
                  y_ref, xpad_ref, h_ref, a_buf, b_buf):
    Tc, C = xr_ref.shape[1], xr_ref.shape[2]
    G = Tc // SUBLANES

    @pl.when(pl.program_id(1) == 0)
    def _():
        xpad_ref[:SUBLANES] = jnp.zeros((SUBLANES, C), F32)
        h_ref[...] = jnp.zeros_like(h_ref)

    x = xr_ref[0]
    xpad_ref[SUBLANES:] = x
    xc = x * cw_ref[CONV_WIDTH - 1:CONV_WIDTH, :] + cb_ref[...]
    for k in range(1, CONV_WIDTH):
        xc = xc + (xpad_ref[pl.ds(SUBLANES - k, Tc), :]
                   * cw_ref[CONV_WIDTH - 1 - k:CONV_WIDTH - k, :])
    xpad_ref[:SUBLANES] = x[Tc - SUBLANES:]

    gates = jnp.dot(xc.astype(BF16), wg_ref[...], preferred_element_type=F32)
    r = jax.nn.sigmoid(gates[:, :C] + ba_ref[...])
    ig = jax.nn.sigmoid(gates[:, C:] + bi_ref[...])
    L = L_ref[...]
    log_sig = jnp.minimum(L, 0.0) - jnp.log1p(jnp.exp(-jnp.abs(L)))
    log_a = r * (RG_LRU_C * log_sig)
    a = jnp.exp(log_a)
    v = -jnp.tanh(log_a) * (a * a + 1.0)
    root = jnp.where(v > 0.0, v * lax.rsqrt(v), 0.0)
    u = root * (ig * xc)

    a3 = a.reshape(G, SUBLANES, C)
    u3 = u.reshape(G, SUBLANES, C)
    sub3 = lax.broadcasted_iota(jnp.int32, (G, SUBLANES, C), 1)
    for k in (1, 2, 4):
        a_sh = pltpu.roll(a3, k, 1)
        u_sh = pltpu.roll(u3, k, 1)
        ok = sub3 >= k
        u3 = jnp.where(ok, a3 * u_sh + u3, u3)
        a3 = jnp.where(ok, a3 * a_sh, a3)
    a_buf[...] = a3
    b_buf[...] = u3

    def body(g, hprev):
        hg = a_buf[g] * hprev + b_buf[g]
        b_buf[g] = hg
        return hg[SUBLANES - 1:SUBLANES, :]

    h_last = lax.fori_loop(0, G, body, h_ref[...], unroll=8)
    h_ref[...] = h_last

    h = b_buf[...].reshape(Tc, C)
    gr = gr_ref[0]
    c0 = math.sqrt(2.0 / math.pi)
    t = jnp.tanh(gr * (c0 + (c0 * 0.044715) * (gr * gr)))
    y_ref[0] = (h * (gr * (0.5 + 0.5 * t))).astype(y_ref.dtype)


def _rglru(xr, gr, conv_w, conv_b, w_gates, b_a, b_i, lru_L):
    B, S, C = xr.shape
    Tc = RNN_TC
    tok = pl.BlockSpec((1, Tc, C), lambda b, t: (b, t, 0))
    row = lambda: _const_spec((1, C))
    return pl.pallas_call(
        _rglru_kernel,
        grid=(B, S // Tc),
        in_specs=[tok, tok, _const_spec((CONV_WIDTH, C)), row(),
                  _const_spec(w_gates.shape), row(), row(), row()],
        out_specs=tok,
        out_shape=jax.ShapeDtypeStruct((B, S, C), BF16),
        scratch_shapes=[pltpu.VMEM((Tc + SUBLANES, C), F32),
                        pltpu.VMEM((1, C), F32),
                        pltpu.VMEM((Tc // SUBLANES, SUBLANES, C), F32),
                        pltpu.VMEM((Tc // SUBLANES, SUBLANES, C), F32)],
        compiler_params=pltpu.CompilerParams(
            dimension_semantics=("parallel", "arbitrary"), vmem_limit_bytes=VMEM_LIMIT),
        name="rg_lru",
    )(xr, gr, conv_w, conv_b.reshape(1, C), w_gates, b_a.reshape(1, C),
      b_i.reshape(1, C), lru_L.reshape(1, C))


def _cast_kernel(w_ref, o_ref):
    o_ref[...] = w_ref[0].astype(BF16)


def _to_bf16(w, l):
    _, R, Cn = w.shape
    return pl.pallas_call(
        _cast_kernel,
        grid=(R // CAST_ROWS,),
        in_specs=[pl.BlockSpec((1, CAST_ROWS, Cn), lambda i: (l, i, 0))],
        out_specs=pl.BlockSpec((CAST_ROWS, Cn), lambda i: (i, 0)),
        out_shape=jax.ShapeDtypeStruct((R, Cn), BF16),
        compiler_params=pltpu.CompilerParams(vmem_limit_bytes=VMEM_LIMIT),
        name="to_bf16",
    )(w)


def _block_diag(w):
    n, bw, _ = w.shape
    eye = jnp.eye(n, dtype=w.dtype)
    return (eye[:, None, :, None] * w[:, :, None, :]).reshape(n * bw, n * bw)


def kernel(x, c, rel_bias, ada_w, ada_b, norm_g, ffn1_w1, ffn1_w3, ffn1_w2, w_in, lam_q1, lam_k1, lam_q2, lam_k2, subln_g, conv_w, conv_b, gate_a_w, gate_a_b, gate_i_w, gate_i_b, lru_L, w_out, ffn2_w1, ffn2_w3, ffn2_w2, final_g):
    B, S, D = x.shape
    depth = ada_w.shape[0]
    c_pad = jnp.zeros((SUBLANES, D), F32).at[:B].set(c)
    bias = _bias_tiles(rel_bias, BIAS_T)
    for l in range(depth):
        mod = _modulation(c_pad, ada_w[l], ada_b[l])[:, :B]
        sh1, sc1, g1, sh2, sc2, g2, sh3, sc3, g3 = [m.reshape(B, 1, D) for m in mod]

        w_in_b = _to_bf16(w_in, l)
        w_qvt = jnp.concatenate([w_in_b[:, :ATTN_WIDTH],
                                 w_in_b[:, 2 * ATTN_WIDTH:3 * ATTN_WIDTH]], axis=1).T
        x, q, k, vt, xr, gr = _ffn_in(
            x, norm_g[l, 0], sh1, sc1, g1,
            _to_bf16(ffn1_w1, l), _to_bf16(ffn1_w3, l), _to_bf16(ffn1_w2, l),
            norm_g[l, 1], sh2, sc2, w_in_b, w_qvt)

        lambda_init = 0.8 - 0.6 * math.exp(-0.3 * l)
        o = _attention(q, k, vt, bias, lam_q1[l], lam_k1[l], lam_q2[l], lam_k2[l],
                       subln_g[l], lambda_init)

        w_gates = jnp.concatenate(
            [_block_diag(gate_a_w[l]), _block_diag(gate_i_w[l])], axis=1).astype(BF16)
        yr = _rglru(xr, gr, conv_w[l], conv_b[l], w_gates, gate_a_b[l], gate_i_b[l],
                    lru_L[l])

        w_out_c = _to_bf16(w_out, l).reshape(2, ATTN_WIDTH, D)
        last = l == depth - 1
        x = _out_ffn(x, o, yr, g2, w_out_c, norm_g[l, 2], sh3, sc3, g3,
                     _to_bf16(ffn2_w1, l), _to_bf16(ffn2_w3, l), _to_bf16(ffn2_w2, l),
                     final_g=final_g if last else None)
    return x
```

```python
import functools
import math

import jax
import jax.numpy as jnp
from jax import lax
from jax.experimental import pallas as pl
from jax.experimental.pallas import tpu as pltpu

F32 = jnp.float32
BF16 = jnp.bfloat16

ATTN_HEADS = 4
ATTN_QK_DIM = 64
ATTN_V_DIM = 2 * ATTN_QK_DIM
ATTN_WIDTH = ATTN_HEADS * ATTN_V_DIM
RNN_BLOCKS = 8
CONV_WIDTH = 4
RG_LRU_C = 8.0
N_BUCKETS = 32
MAX_DISTANCE = 128
NORM_EPS = 1e-6
N_MOD = 9

SUBLANES = 8
VMEM_LIMIT = 56 * 1024 * 1024

FFN_TM = 512
FFN_FC = 256
ATTN_T = 1024
ATTN_STRIP = 256
ATTN_LOOKAHEAD = 1
ATTN_HOLD = 3
BIAS_T = 128
RNN_TC = 1024
NEG_BIG = -1e30
LOG2E = math.log2(math.e)


def _rms(x):
    return x * lax.rsqrt(jnp.mean(x * x, axis=-1, keepdims=True) + NORM_EPS)


def _const_spec(shape):
    nd = len(shape)
    return pl.BlockSpec(shape, lambda *_: (0,) * nd, pipeline_mode=pl.Buffered(1))


def _mod_kernel(c_ref, w_ref, b_ref, o_ref):
    c = c_ref[...]
    ca = c * jax.nn.sigmoid(c)
    o_ref[0] = jnp.dot(ca, w_ref[...], preferred_element_type=F32,
                       precision=lax.Precision.HIGHEST) + b_ref[0]


def _modulation(c_pad, ada_w, ada_b):
    rows, d = c_pad.shape
    n = ada_w.shape[1] // d
    return pl.pallas_call(
        _mod_kernel,
        grid=(n,),
        in_specs=[pl.BlockSpec((rows, d), lambda j: (0, 0)),
                  pl.BlockSpec((d, d), lambda j: (0, j)),
                  pl.BlockSpec((1, 1, d), lambda j: (j, 0, 0))],
        out_specs=pl.BlockSpec((1, rows, d), lambda j: (j, 0, 0)),
        out_shape=jax.ShapeDtypeStruct((n, rows, d), F32),
        compiler_params=pltpu.CompilerParams(vmem_limit_bytes=VMEM_LIMIT),
        name="adaln_mod",
    )(c_pad, ada_w, ada_b.reshape(n, 1, d))


def _modulated(x, ng_ref, sh_ref, sc_ref):
    return (_rms(x) * ng_ref[...] * (1.0 + sc_ref[0]) + sh_ref[0]).astype(BF16)


def _swiglu_residual(x, hb, g_ref, w1_ref, w3_ref, w2_ref, acc_ref):
    n_chunks = w1_ref.shape[1] // FFN_FC
    cols = lambda ci: slice(ci * FFN_FC, (ci + 1) * FFN_FC)
    up = lambda ci: (jnp.dot(hb, w1_ref[:, cols(ci)], preferred_element_type=F32),
                     jnp.dot(hb, w3_ref[:, cols(ci)], preferred_element_type=F32))
    a, b = up(0)
    for ci in range(n_chunks):
        if ci + 1 < n_chunks:
            a_next, b_next = up(ci + 1)
        u = (a * jax.nn.sigmoid(a) * b).astype(BF16)
        down = jnp.dot(u, w2_ref[cols(ci), :], preferred_element_type=F32)
        if ci == 0:
            acc_ref[...] = down
        else:
            acc_ref[...] += down
        a, b = a_next, b_next
    return x + 0.5 * g_ref[0] * acc_ref[...]


def _ffn_in_kernel(x_ref, ng1_ref, sh1_ref, sc1_ref, g1_ref, w1_ref, w3_ref, w2_ref,
                   ng2_ref, sh2_ref, sc2_ref, win_ref, wqvt_ref,
                   x1_ref, q_ref, k_ref, vt_ref, xr_ref, gr_ref, acc_ref):
    x = x_ref[0]
    y = _swiglu_residual(x, _modulated(x, ng1_ref, sh1_ref, sc1_ref),
                         g1_ref, w1_ref, w3_ref, w2_ref, acc_ref)
    x1_ref[0] = y
    hb = _modulated(y, ng2_ref, sh2_ref, sc2_ref)
    scale = ATTN_QK_DIM ** -0.5 * LOG2E
    W = k_ref.shape[-1]
    proj = lambda n: jnp.dot(hb, win_ref[:, n * W:(n + 1) * W], preferred_element_type=F32)
    k_ref[0] = proj(1).astype(BF16)
    qvt = lax.dot_general(wqvt_ref[...], hb, (((1,), (1,)), ((), ())),
                          preferred_element_type=F32)
    vt_ref[0, 0] = qvt[W:].astype(BF16)
    qt = (qvt[:W] * scale).astype(BF16)
    dqk = ATTN_QK_DIM
    zeros = jnp.zeros((dqk, qt.shape[1]), BF16)
    for h in range(ATTN_HEADS):
        top = qt[2 * dqk * h:2 * dqk * h + dqk]
        bot = qt[2 * dqk * h + dqk:2 * dqk * (h + 1)]
        q_ref[0, h, 0, 0] = jnp.concatenate([top, zeros], axis=0)
        q_ref[0, h, 0, 1] = jnp.concatenate([zeros, bot], axis=0)
    xr_ref[0] = proj(3)
    gr_ref[0] = proj(4)


def _ffn_in(x, ng1, sh1, sc1, g1, w1, w3, w2, ng2, sh2, sc2, w_in_b, w_qvt):
    B, S, D = x.shape
    tm, T = FFN_TM, ATTN_T
    W = w_qvt.shape[0] // 2
    H, dqk = ATTN_HEADS, ATTN_QK_DIM
    per_t = T // tm
    vec = pl.BlockSpec((1, 1, D), lambda b, i: (b, 0, 0))
    row = _const_spec((1, D))
    full = pl.BlockSpec((1, tm, D), lambda b, i: (b, i, 0))
    tok = pl.BlockSpec((1, tm, W), lambda b, i: (b, i, 0))
    vt_spec = pl.BlockSpec((1, 1, W, tm), lambda b, i: (b, i // per_t, 0, i % per_t))
    q_spec = pl.BlockSpec((1, H, 1, 2, 2 * dqk, tm),
                          lambda b, i: (b, 0, i // per_t, 0, 0, i % per_t))
    return pl.pallas_call(
        _ffn_in_kernel,
        grid=(B, S // tm),
        in_specs=[full, row, vec, vec, vec,
                  _const_spec(w1.shape), _const_spec(w3.shape), _const_spec(w2.shape),
                  row, vec, vec, _const_spec(w_in_b.shape), _const_spec(w_qvt.shape)],
        out_specs=[full, q_spec, tok, vt_spec, tok, tok],
        out_shape=[jax.ShapeDtypeStruct((B, S, D), F32),
                   jax.ShapeDtypeStruct((B, H, S // T, 2, 2 * dqk, T), BF16),
                   jax.ShapeDtypeStruct((B, S, W), BF16),
                   jax.ShapeDtypeStruct((B, S // T, W, T), BF16)]
                  + [jax.ShapeDtypeStruct((B, S, W), F32)] * 2,
        scratch_shapes=[pltpu.VMEM((tm, D), F32)],
        compiler_params=pltpu.CompilerParams(
            dimension_semantics=("parallel", "parallel"), vmem_limit_bytes=VMEM_LIMIT),
        name="ffn_in_proj",
    )(x, ng1.reshape(1, D), sh1, sc1, g1, w1, w3, w2, ng2.reshape(1, D), sh2, sc2,
      w_in_b, w_qvt)


def _out_ffn_kernel(x_ref, o_ref, yr_ref, g2_ref, wout_ref, ng_ref, sh_ref, sc_ref, g_ref,
                    w1_ref, w3_ref, w2_ref, *rest, final_norm):
    if final_norm:
        fg_ref, out_ref, acc_ref = rest
    else:
        out_ref, acc_ref = rest
    mix = (jnp.dot(o_ref[0], wout_ref[0], preferred_element_type=F32)
           + jnp.dot(yr_ref[0], wout_ref[1], preferred_element_type=F32))
    x = x_ref[0] + g2_ref[0] * mix
    y = _swiglu_residual(x, _modulated(x, ng_ref, sh_ref, sc_ref),
                         g_ref, w1_ref, w3_ref, w2_ref, acc_ref)
    if final_norm:
        y = _rms(y) * fg_ref[...]
    out_ref[0] = y


def _out_ffn(x, o, yr, g2, w_out_c, ng, sh, sc, g, w1, w3, w2, final_g=None):
    B, S, D = x.shape
    tm = FFN_TM
    W = o.shape[-1]
    final_norm = final_g is not None
    vec = pl.BlockSpec((1, 1, D), lambda b, i: (b, 0, 0))
    row = _const_spec((1, D))
    full = pl.BlockSpec((1, tm, D), lambda b, i: (b, i, 0))
    half = pl.BlockSpec((1, tm, W), lambda b, i: (b, i, 0))
    in_specs = [full, half, half, vec, _const_spec(w_out_c.shape), row, vec, vec, vec,
                _const_spec(w1.shape), _const_spec(w3.shape), _const_spec(w2.shape)]
    args = [x, o, yr, g2, w_out_c, ng.reshape(1, D), sh, sc, g, w1, w3, w2]
    if final_norm:
        in_specs.append(row)
        args.append(final_g.reshape(1, D))
    return pl.pallas_call(
        functools.partial(_out_ffn_kernel, final_norm=final_norm),
        grid=(B, S // tm),
        in_specs=in_specs,
        out_specs=full,
        out_shape=jax.ShapeDtypeStruct((B, S, D), F32),
        scratch_shapes=[pltpu.VMEM((tm, D), F32)],
        compiler_params=pltpu.CompilerParams(
            dimension_semantics=("parallel", "parallel"), vmem_limit_bytes=VMEM_LIMIT),
        name="out_proj_ffn_final" if final_norm else "out_proj_ffn",
    )(*args)


def _bias_kernel(rb_ref, o_ref):
    mp = pl.program_id(0)
    T = o_ref.shape[-1]
    key = lax.broadcasted_iota(jnp.int32, (T, T), 0)
    qry = lax.broadcasted_iota(jnp.int32, (T, T), 1)
    max_exact = N_BUCKETS // 2
    last = rb_ref[N_BUCKETS - 1, mp]
    for which in range(2):
        rel = qry - key + (T if which == 0 else 0)
        n = jnp.maximum(rel, 0)
        nf = jnp.maximum(n, 1).astype(F32)
        large = max_exact + (jnp.log(nf / max_exact) / math.log(MAX_DISTANCE / max_exact)
                             * (N_BUCKETS - max_exact)).astype(jnp.int32)
        large = jnp.minimum(large, N_BUCKETS - 1)
        bucket = jnp.where(n < max_exact, n, large)
        val = jnp.zeros((T, T), F32)
        for bk in range(N_BUCKETS - 1):
            val = jnp.where(bucket == bk, (rb_ref[bk, mp] - last) * LOG2E, val)
        if which == 1:
            val = jnp.where(rel >= 0, val, NEG_BIG)
        o_ref[0, which] = val


def _bias_tiles(rel_bias, T):
    n_maps = rel_bias.shape[1]
    return pl.pallas_call(
        _bias_kernel,
        grid=(n_maps,),
        in_specs=[pl.BlockSpec(memory_space=pltpu.SMEM)],
        out_specs=pl.BlockSpec((1, 2, T, T), lambda m: (m, 0, 0, 0)),
        out_shape=jax.ShapeDtypeStruct((n_maps, 2, T, T), F32),
        name="rel_bias_tiles",
    )(rel_bias)


def _attn_kernel(q_ref, qn_ref, k_ref, vt_ref, bias_ref, lq1_ref, lk1_ref, lq2_ref, lk2_ref,
                 sg_ref, o_ref, m_ref, l_ref, acc_ref, s_ref, *, lambda_init):
    T = q_ref.shape[-1]
    qi = pl.program_id(2)
    q_refs = (q_ref, qn_ref)

    m_ref[...] = jnp.full_like(m_ref, NEG_BIG)
    l_ref[...] = jnp.zeros_like(l_ref)
    acc_ref[...] = jnp.zeros_like(acc_ref)

    strips = [slice(c * ATTN_STRIP, (c + 1) * ATTN_STRIP)
              for c in range(2 * T // ATTN_STRIP)]

    def n_keys(which, c):
        return strips[c].start % T + ATTN_STRIP if which == 1 else T

    def put_scores(buf, j, which, c, slot=0):
        nk = n_keys(which, c)
        start = j * T if isinstance(j, int) else pl.multiple_of(j * T, T)
        kj = k_ref[0, pl.ds(start, nk), :]
        mp, q0 = divmod(strips[c].start, T)
        qt = q_refs[slot][0, 0, 0, mp, :, q0:q0 + ATTN_STRIP]
        s_ref[buf, :nk, strips[c]] = jnp.dot(kj, qt, preferred_element_type=F32)

    def biased(st, which, mp, q0):
        rows = []
        for a in range(st.shape[0] // BIAS_T):
            cols = []
            for b in range(q0 // BIAS_T, (q0 + ATTN_STRIP) // BIAS_T):
                lo = (b - q0 // BIAS_T) * BIAS_T
                piece = st[a * BIAS_T:(a + 1) * BIAS_T, lo:lo + BIAS_T]
                dist = b - a + (T // BIAS_T if which == 0 else 0)
                if dist < 0:
                    piece = jnp.full_like(piece, NEG_BIG)
                elif dist <= 1:
                    piece = piece + bias_ref[0, mp, 1 - dist]
                cols.append(piece)
            rows.append(jnp.concatenate(cols, axis=1))
        return jnp.concatenate(rows, axis=0)

    def step(buf, j, which, nxt, hold=0):
        early = len(strips) - hold
        for c in range(ATTN_LOOKAHEAD):
            put_scores(*nxt[:3], c, nxt[3])
        for c, sl in enumerate(strips):
            if c + ATTN_LOOKAHEAD < early:
                put_scores(*nxt[:3], c + ATTN_LOOKAHEAD, nxt[3])
            nk = n_keys(which, c)
            st = s_ref[buf, :nk, sl]
            if which is not None:
                st = biased(st, which, *divmod(sl.start, T))
            m_old = m_ref[:, sl]
            m_new = jnp.maximum(m_old, jnp.max(st, axis=0, keepdims=True))
            alpha = jnp.exp2(m_old - m_new)
            p = jnp.exp2(st - m_new)
            l_ref[:, sl] = alpha * l_ref[:, sl] + jnp.sum(p, axis=0, keepdims=True)
            acc_ref[:, sl] = alpha * acc_ref[:, sl] + jnp.dot(
                vt_ref[0, j, :, :nk], p.astype(BF16), preferred_element_type=F32)
            m_ref[:, sl] = m_new
        for c in range(early, len(strips)):
            put_scores(*nxt[:3], c, nxt[3])

    def last_step(buf):
        step(buf, qi, 1, (0, 0, None, 1), hold=ATTN_HOLD)
        lam = (jnp.exp(jnp.sum(lq1_ref[...] * lk1_ref[...], axis=-1, keepdims=True))
               - jnp.exp(jnp.sum(lq2_ref[...] * lk2_ref[...], axis=-1, keepdims=True))
               + lambda_init)
        o_all = acc_ref[...] / l_ref[...]
        ot = o_all[:, :T] - lam * o_all[:, T:]
        ot = ot * lax.rsqrt(jnp.mean(ot * ot, axis=0, keepdims=True) + NORM_EPS)
        o_ref[0] = (ot.T * sg_ref[...] * (1.0 - lambda_init)).astype(o_ref.dtype)

    @pl.when(qi == 0)
    def _():
        for c in range(len(strips)):
            put_scores(2, 0, 1, c)
        last_step(2)

    @pl.when(qi >= 1)
    def _():
        n_far = qi - 1

        def pair(i, carry):
            j = 2 * i
            step(0, j, None, (1, j + 1, None, 0))
            step(1, j + 1, None, (0, j + 2, None, 0))
            return carry

        lax.fori_loop(0, n_far // 2, pair, 0)

        @pl.when(n_far % 2 == 0)
        def _():
            step(0, qi - 1, 0, (1, qi, 1, 0))
            last_step(1)

        @pl.when(n_far % 2 == 1)
        def _():
            step(0, qi - 2, None, (1, qi - 1, None, 0))
            step(1, qi - 1, 0, (2, qi, 1, 0))
            last_step(2)


def _attention(q, k, vt, bias, lq1, lk1, lq2, lk2, subln_g, lambda_init):
    B, S, _ = k.shape
    T = ATTN_T
    H, dv, dqk = ATTN_HEADS, ATTN_V_DIM, ATTN_QK_DIM
    bias = bias.reshape(H, 2, 2, BIAS_T, BIAS_T)
    small = lambda n: _const_spec((1, n))
    return pl.pallas_call(
        functools.partial(_attn_kernel, lambda_init=lambda_init),
        grid=(B, H, S // T),
        in_specs=[pl.BlockSpec((1, 1, 1, 2, 2 * dqk, T), lambda b, h, i: (b, h, i, 0, 0, 0)),
                  pl.BlockSpec((1, 1, 1, 2, 2 * dqk, T),
                               lambda b, h, i: (b, h, jnp.minimum(i + 1, S // T - 1), 0, 0, 0)),
                  pl.BlockSpec((1, S, 2 * dqk), lambda b, h, i: (b, 0, h)),
                  pl.BlockSpec((1, S // T, dv, T), lambda b, h, i: (b, 0, h, 0)),
                  pl.BlockSpec((1, 2, 2, BIAS_T, BIAS_T), lambda b, h, i: (h, 0, 0, 0, 0)),
                  small(dqk), small(dqk), small(dqk), small(dqk), small(dv)],
        out_specs=pl.BlockSpec((1, T, dv), lambda b, h, i: (b, i, h)),
        out_shape=jax.ShapeDtypeStruct((B, S, H * dv), BF16),
        scratch_shapes=[pltpu.VMEM((1, 2 * T), F32),
                        pltpu.VMEM((1, 2 * T), F32),
                        pltpu.VMEM((dv, 2 * T), F32),
                        pltpu.VMEM((3, T, 2 * T), F32)],
        compiler_params=pltpu.CompilerParams(
            dimension_semantics=("parallel", "parallel", "arbitrary"),
            vmem_limit_bytes=VMEM_LIMIT),
        name="diff_attn",
    )(q, q, k, vt, bias, lq1.reshape(1, dqk), lk1.reshape(1, dqk), lq2.reshape(1, dqk),
      lk2.reshape(1, dqk), subln_g.reshape(1, dv))


def _rglru_kernel(xr_ref, gr_ref, cw_ref, cb_ref, wg_ref, ba_ref, bi_ref, L_ref,
                  y_ref, xpad_ref, h_ref, a_buf, b_buf):
    Tc, C = xr_ref.shape[1], xr_ref.shape[2]
    G = Tc // SUBLANES

    @pl.when(pl.program_id(1) == 0)
    def _():
        xpad_ref[:SUBLANES] = jnp.zeros((SUBLANES, C), F32)
        h_ref[...] = jnp.zeros_like(h_ref)

    x = xr_ref[0]
    xpad_ref[SUBLANES:] = x
    xc = x * cw_ref[CONV_WIDTH - 1:CONV_WIDTH, :] + cb_ref[...]
    for k in range(1, CONV_WIDTH):
        xc = xc + (xpad_ref[pl.ds(SUBLANES - k, Tc), :]
                   * cw_ref[CONV_WIDTH - 1 - k:CONV_WIDTH - k, :])
    xpad_ref[:SUBLANES] = x[Tc - SUBLANES:]

    gates = jnp.dot(xc.astype(BF16), wg_ref[...], preferred_element_type=F32)
    r = jax.nn.sigmoid(gates[:, :C] + ba_ref[...])
    ig = jax.nn.sigmoid(gates[:, C:] + bi_ref[...])
    L = L_ref[...]
    log_sig = jnp.minimum(L, 0.0) - jnp.log1p(jnp.exp(-jnp.abs(L)))
    log_a = r * (RG_LRU_C * log_sig)
    a = jnp.exp(log_a)
    v = -jnp.tanh(log_a) * (a * a + 1.0)
    root = jnp.where(v > 0.0, v * lax.rsqrt(v), 0.0)
    u = root * (ig * xc)

    a3 = a.reshape(G, SUBLANES, C)
    u3 = u.reshape(G, SUBLANES, C)
    sub3 = lax.broadcasted_iota(jnp.int32, (G, SUBLANES, C), 1)
    for k in (1, 2, 4):
        a_sh = pltpu.roll(a3, k, 1)
        u_sh = pltpu.roll(u3, k, 1)
        ok = sub3 >= k
        u3 = jnp.where(ok, a3 * u_sh + u3, u3)
        a3 = jnp.where(ok, a3 * a_sh, a3)
    a_buf[...] = a3
    b_buf[...] = u3

    def body(g, hprev):
        hg = a_buf[g] * hprev + b_buf[g]
        b_buf[g] = hg
        return hg[SUBLANES - 1:SUBLANES, :]

    h_last = lax.fori_loop(0, G, body, h_ref[...], unroll=8)
    h_ref[...] = h_last

    h = b_buf[...].reshape(Tc, C)
    gr = gr_ref[0]
    c0 = math.sqrt(2.0 / math.pi)
    t = jnp.tanh(gr * (c0 + (c0 * 0.044715) * (gr * gr)))
    y_ref[0] = (h * (gr * (0.5 + 0.5 * t))).astype(y_ref.dtype)


def _rglru(xr, gr, conv_w, conv_b, w_gates, b_a, b_i, lru_L):
    B, S, C = xr.shape
    Tc = RNN_TC
    tok = pl.BlockSpec((1, Tc, C), lambda b, t: (b, t, 0))
    row = lambda: _const_spec((1, C))
    return pl.pallas_call(
        _rglru_kernel,
        grid=(B, S // Tc),
        in_specs=[tok, tok, _const_spec((CONV_WIDTH, C)), row(),
                  _const_spec(w_gates.shape), row(), row(), row()],
        out_specs=tok,
        out_shape=jax.ShapeDtypeStruct((B, S, C), BF16),
        scratch_shapes=[pltpu.VMEM((Tc + SUBLANES, C), F32),
                        pltpu.VMEM((1, C), F32),
                        pltpu.VMEM((Tc // SUBLANES, SUBLANES, C), F32),
                        pltpu.VMEM((Tc // SUBLANES, SUBLANES, C), F32)],
        compiler_params=pltpu.CompilerParams(
            dimension_semantics=("parallel", "arbitrary"), vmem_limit_bytes=VMEM_LIMIT),
        name="rg_lru",
    )(xr, gr, conv_w, conv_b.reshape(1, C), w_gates, b_a.reshape(1, C),
      b_i.reshape(1, C), lru_L.reshape(1, C))


def _to_bf16(w, l):
    return w[l].astype(BF16)


def _block_diag(w):
    n, bw, _ = w.shape
    eye = jnp.eye(n, dtype=w.dtype)
    return (eye[:, None, :, None] * w[:, :, None, :]).reshape(n * bw, n * bw)


def kernel(x, c, rel_bias, ada_w, ada_b, norm_g, ffn1_w1, ffn1_w3, ffn1_w2, w_in, lam_q1, lam_k1, lam_q2, lam_k2, subln_g, conv_w, conv_b, gate_a_w, gate_a_b, gate_i_w, gate_i_b, lru_L, w_out, ffn2_w1, ffn2_w3, ffn2_w2, final_g):
    B, S, D = x.shape
    depth = ada_w.shape[0]
    c_pad = jnp.zeros((SUBLANES, D), F32).at[:B].set(c)
    bias = _bias_tiles(rel_bias, BIAS_T)
    for l in range(depth):
        mod = _modulation(c_pad, ada_w[l], ada_b[l])[:, :B]
        sh1, sc1, g1, sh2, sc2, g2, sh3, sc3, g3 = [m.reshape(B, 1, D) for m in mod]

        w_in_b = _to_bf16(w_in, l)
        w_qvt = jnp.concatenate([w_in_b[:, :ATTN_WIDTH],
                                 w_in_b[:, 2 * ATTN_WIDTH:3 * ATTN_WIDTH]], axis=1).T
        x, q, k, vt, xr, gr = _ffn_in(
            x, norm_g[l, 0], sh1, sc1, g1,
            _to_bf16(ffn1_w1, l), _to_bf16(ffn1_w3, l), _to_bf16(ffn1_w2, l),
            norm_g[l, 1], sh2, sc2, w_in_b, w_qvt)

        lambda_init = 0.8 - 0.6 * math.exp(-0.3 * l)
        o = _attention(q, k, vt, bias, lam_q1[l], lam_k1[l], lam_q2[l], lam_k2[l],
                       subln_g[l], lambda_init)

        w_gates = jnp.concatenate(
            [_block_diag(gate_a_w[l]), _block_diag(gate_i_w[l])], axis=1).astype(BF16)
        yr = _rglru(xr, gr, conv_w[l], conv_b[l], w_gates, gate_a_b[l], gate_i_b[l],
                    lru_L[l])

        w_out_c = _to_bf16(w_out, l).reshape(2, ATTN_WIDTH, D)
        last = l == depth - 1
        x = _out_ffn(x, o, yr, g2, w_out_c, norm_g[l, 2], sh3, sc3, g3,
                     _to_bf16(ffn2_w1, l), _to_bf16(ffn2_w3, l), _to_bf16(ffn2_w2, l),
                     final_g=final_g if last else None)
    return x
```

```python
import functools
import math

import jax
import jax.numpy as jnp
from jax import lax
from jax.experimental import pallas as pl
from jax.experimental.pallas import tpu as pltpu

F32 = jnp.float32
BF16 = jnp.bfloat16

ATTN_HEADS = 4
ATTN_QK_DIM = 64
ATTN_V_DIM = 2 * ATTN_QK_DIM
ATTN_WIDTH = ATTN_HEADS * ATTN_V_DIM
RNN_BLOCKS = 8
CONV_WIDTH = 4
RG_LRU_C = 8.0
N_BUCKETS = 32
MAX_DISTANCE = 128
NORM_EPS = 1e-6
N_MOD = 9

SUBLANES = 8
VMEM_LIMIT = 56 * 1024 * 1024

FFN_TM = 512
FFN_FC = 256
ATTN_T = 1024
ATTN_STRIP = 256
ATTN_LOOKAHEAD = 1
ATTN_HOLD = 3
BIAS_T = 128
RNN_TC = 1024
NEG_BIG = -1e30
LOG2E = math.log2(math.e)


def _rms(x):
    return x * lax.rsqrt(jnp.mean(x * x, axis=-1, keepdims=True) + NORM_EPS)


def _const_spec(shape):
    nd = len(shape)
    return pl.BlockSpec(shape, lambda *_: (0,) * nd, pipeline_mode=pl.Buffered(1))


def _mod_kernel(c_ref, w_ref, b_ref, o_ref):
    c = c_ref[...]
    ca = c * jax.nn.sigmoid(c)
    o_ref[0] = jnp.dot(ca, w_ref[...], preferred_element_type=F32,
                       precision=lax.Precision.HIGHEST) + b_ref[0]


def _modulation(c_pad, ada_w, ada_b):
    rows, d = c_pad.shape
    n = ada_w.shape[1] // d
    return pl.pallas_call(
        _mod_kernel,
        grid=(n,),
        in_specs=[pl.BlockSpec((rows, d), lambda j: (0, 0)),
                  pl.BlockSpec((d, d), lambda j: (0, j)),
                  pl.BlockSpec((1, 1, d), lambda j: (j, 0, 0))],
        out_specs=pl.BlockSpec((1, rows, d), lambda j: (j, 0, 0)),
        out_shape=jax.ShapeDtypeStruct((n, rows, d), F32),
        compiler_params=pltpu.CompilerParams(vmem_limit_bytes=VMEM_LIMIT),
        name="adaln_mod",
    )(c_pad, ada_w, ada_b.reshape(n, 1, d))


def _modulated(x, ng_ref, sh_ref, sc_ref):
    return (_rms(x) * ng_ref[...] * (1.0 + sc_ref[0]) + sh_ref[0]).astype(BF16)


def _swiglu_residual(x, hb, g_ref, w1_ref, w3_ref, w2_ref, acc_ref):
    n_chunks = w1_ref.shape[1] // FFN_FC
    cols = lambda ci: slice(ci * FFN_FC, (ci + 1) * FFN_FC)
    up = lambda ci: (jnp.dot(hb, w1_ref[:, cols(ci)], preferred_element_type=F32),
                     jnp.dot(hb, w3_ref[:, cols(ci)], preferred_element_type=F32))
    a, b = up(0)
    for ci in range(n_chunks):
        if ci + 1 < n_chunks:
            a_next, b_next = up(ci + 1)
        u = (a * jax.nn.sigmoid(a) * b).astype(BF16)
        down = jnp.dot(u, w2_ref[cols(ci), :], preferred_element_type=F32)
        if ci == 0:
            acc_ref[...] = down
        else:
            acc_ref[...] += down
        a, b = a_next, b_next
    return x + 0.5 * g_ref[0] * acc_ref[...]


def _ffn_in_kernel(x_ref, ng1_ref, sh1_ref, sc1_ref, g1_ref, w1_ref, w3_ref, w2_ref,
                   ng2_ref, sh2_ref, sc2_ref, win_ref, wqvt_ref,
                   x1_ref, q_ref, k_ref, vt_ref, xr_ref, gr_ref, acc_ref):
    x = x_ref[0]
    y = _swiglu_residual(x, _modulated(x, ng1_ref, sh1_ref, sc1_ref),
                         g1_ref, w1_ref, w3_ref, w2_ref, acc_ref)
    x1_ref[0] = y
    hb = _modulated(y, ng2_ref, sh2_ref, sc2_ref)
    scale = ATTN_QK_DIM ** -0.5 * LOG2E
    W = k_ref.shape[-1]
    proj = lambda n: jnp.dot(hb, win_ref[:, n * W:(n + 1) * W], preferred_element_type=F32)
    k_ref[0] = proj(1).astype(BF16)
    qvt = lax.dot_general(wqvt_ref[...], hb, (((1,), (1,)), ((), ())),
                          preferred_element_type=F32)
    vt_ref[0, 0] = qvt[W:].astype(BF16)
    qt = (qvt[:W] * scale).astype(BF16)
    dqk = ATTN_QK_DIM
    zeros = jnp.zeros((dqk, qt.shape[1]), BF16)
    for h in range(ATTN_HEADS):
        top = qt[2 * dqk * h:2 * dqk * h + dqk]
        bot = qt[2 * dqk * h + dqk:2 * dqk * (h + 1)]
        q_ref[0, h, 0, 0] = jnp.concatenate([top, zeros], axis=0)
        q_ref[0, h, 0, 1] = jnp.concatenate([zeros, bot], axis=0)
    xr_ref[0] = proj(3)
    gr_ref[0] = proj(4)


def _ffn_in(x, ng1, sh1, sc1, g1, w1, w3, w2, ng2, sh2, sc2, w_in_b, w_qvt):
    B, S, D = x.shape
    tm, T = FFN_TM, ATTN_T
    W = w_qvt.shape[0] // 2
    H, dqk = ATTN_HEADS, ATTN_QK_DIM
    per_t = T // tm
    vec = pl.BlockSpec((1, 1, D), lambda b, i: (b, 0, 0))
    row = _const_spec((1, D))
    full = pl.BlockSpec((1, tm, D), lambda b, i: (b, i, 0))
    tok = pl.BlockSpec((1, tm, W), lambda b, i: (b, i, 0))
    vt_spec = pl.BlockSpec((1, 1, W, tm), lambda b, i: (b, i // per_t, 0, i % per_t))
    q_spec = pl.BlockSpec((1, H, 1, 2, 2 * dqk, tm),
                          lambda b, i: (b, 0, i // per_t, 0, 0, i % per_t))
    return pl.pallas_call(
        _ffn_in_kernel,
        grid=(B, S // tm),
        in_specs=[full, row, vec, vec, vec,
                  _const_spec(w1.shape), _const_spec(w3.shape), _const_spec(w2.shape),
                  row, vec, vec, _const_spec(w_in_b.shape), _const_spec(w_qvt.shape)],
        out_specs=[full, q_spec, tok, vt_spec, tok, tok],
        out_shape=[jax.ShapeDtypeStruct((B, S, D), F32),
                   jax.ShapeDtypeStruct((B, H, S // T, 2, 2 * dqk, T), BF16),
                   jax.ShapeDtypeStruct((B, S, W), BF16),
                   jax.ShapeDtypeStruct((B, S // T, W, T), BF16)]
                  + [jax.ShapeDtypeStruct((B, S, W), F32)] * 2,
        scratch_shapes=[pltpu.VMEM((tm, D), F32)],
        compiler_params=pltpu.CompilerParams(
            dimension_semantics=("parallel", "parallel"), vmem_limit_bytes=VMEM_LIMIT),
        name="ffn_in_proj",
    )(x, ng1.reshape(1, D), sh1, sc1, g1, w1, w3, w2, ng2.reshape(1, D), sh2, sc2,
      w_in_b, w_qvt)


def _out_ffn_kernel(x_ref, o_ref, yr_ref, g2_ref, wout_ref, ng_ref, sh_ref, sc_ref, g_ref,
                    w1_ref, w3_ref, w2_ref, *rest, final_norm):
    if final_norm:
        fg_ref, out_ref, acc_ref = rest
    else:
        out_ref, acc_ref = rest
    mix = (jnp.dot(o_ref[0], wout_ref[0], preferred_element_type=F32)
           + jnp.dot(yr_ref[0], wout_ref[1], preferred_element_type=F32))
    x = x_ref[0] + g2_ref[0] * mix
    y = _swiglu_residual(x, _modulated(x, ng_ref, sh_ref, sc_ref),
                         g_ref, w1_ref, w3_ref, w2_ref, acc_ref)
    if final_norm:
        y = _rms(y) * fg_ref[...]
    out_ref[0] = y


def _out_ffn(x, o, yr, g2, w_out_c, ng, sh, sc, g, w1, w3, w2, final_g=None):
    B, S, D = x.shape
    tm = FFN_TM
    W = o.shape[-1]
    final_norm = final_g is not None
    vec = pl.BlockSpec((1, 1, D), lambda b, i: (b, 0, 0))
    row = _const_spec((1, D))
    full = pl.BlockSpec((1, tm, D), lambda b, i: (b, i, 0))
    half = pl.BlockSpec((1, tm, W), lambda b, i: (b, i, 0))
    in_specs = [full, half, half, vec, _const_spec(w_out_c.shape), row, vec, vec, vec,
                _const_spec(w1.shape), _const_spec(w3.shape), _const_spec(w2.shape)]
    args = [x, o, yr, g2, w_out_c, ng.reshape(1, D), sh, sc, g, w1, w3, w2]
    if final_norm:
        in_specs.append(row)
        args.append(final_g.reshape(1, D))
    return pl.pallas_call(
        functools.partial(_out_ffn_kernel, final_norm=final_norm),
        grid=(B, S // tm),
        in_specs=in_specs,
        out_specs=full,
        out_shape=jax.ShapeDtypeStruct((B, S, D), F32),
        scratch_shapes=[pltpu.VMEM((tm, D), F32)],
        compiler_params=pltpu.CompilerParams(
            dimension_semantics=("parallel", "parallel"), vmem_limit_bytes=VMEM_LIMIT),
        name="out_proj_ffn_final" if final_norm else "out_proj_ffn",
    )(*args)


def _bias_kernel(rb_ref, o_ref):
    mp = pl.program_id(0)
    T = o_ref.shape[-1]
    key = lax.broadcasted_iota(jnp.int32, (T, T), 0)
    qry = lax.broadcasted_iota(jnp.int32, (T, T), 1)
    max_exact = N_BUCKETS // 2
    last = rb_ref[N_BUCKETS - 1, mp]
    for which in range(2):
        rel = qry - key + (T if which == 0 else 0)
        n = jnp.maximum(rel, 0)
        nf = jnp.maximum(n, 1).astype(F32)
        large = max_exact + (jnp.log(nf / max_exact) / math.log(MAX_DISTANCE / max_exact)
                             * (N_BUCKETS - max_exact)).astype(jnp.int32)
        large = jnp.minimum(large, N_BUCKETS - 1)
        bucket = jnp.where(n < max_exact, n, large)
        val = jnp.zeros((T, T), F32)
        for bk in range(N_BUCKETS - 1):
            val = jnp.where(bucket == bk, (rb_ref[bk, mp] - last) * LOG2E, val)
        if which == 1:
            val = jnp.where(rel >= 0, val, NEG_BIG)
        o_ref[0, which] = val


def _bias_tiles(rel_bias, T):
    n_maps = rel_bias.shape[1]
    return pl.pallas_call(
        _bias_kernel,
        grid=(n_maps,),
        in_specs=[pl.BlockSpec(memory_space=pltpu.SMEM)],
        out_specs=pl.BlockSpec((1, 2, T, T), lambda m: (m, 0, 0, 0)),
        out_shape=jax.ShapeDtypeStruct((n_maps, 2, T, T), F32),
        name="rel_bias_tiles",
    )(rel_bias)


def _attn_kernel(q_ref, qn_ref, k_ref, vt_ref, bias_ref, lq1_ref, lk1_ref, lq2_ref, lk2_ref,
                 sg_ref, o_ref, m_ref, l_ref, acc_ref, s_ref, smax_ref, *, lambda_init):
    T = q_ref.shape[-1]
    qi = pl.program_id(2)
    q_refs = (q_ref, qn_ref)

    m_ref[...] = jnp.full_like(m_ref, NEG_BIG)
    l_ref[...] = jnp.zeros_like(l_ref)
    acc_ref[...] = jnp.zeros_like(acc_ref)

    strips = [slice(c * ATTN_STRIP, (c + 1) * ATTN_STRIP)
              for c in range(2 * T // ATTN_STRIP)]

    def n_keys(which, c):
        return strips[c].start % T + ATTN_STRIP if which == 1 else T

    def put_scores(buf, j, which, c, slot=0):
        nk = n_keys(which, c)
        start = j * T if isinstance(j, int) else pl.multiple_of(j * T, T)
        kj = k_ref[0, pl.ds(start, nk), :]
        mp, q0 = divmod(strips[c].start, T)
        qt = q_refs[slot][0, 0, 0, mp, :, q0:q0 + ATTN_STRIP]
        st = jnp.dot(kj, qt, preferred_element_type=F32)
        if which is not None:
            st = biased(st, which, mp, q0)
        s_ref[buf, :nk, strips[c]] = st
        smax_ref[buf, :, strips[c]] = jnp.max(st, axis=0, keepdims=True)

    def biased(st, which, mp, q0):
        rows = []
        for a in range(st.shape[0] // BIAS_T):
            cols = []
            for b in range(q0 // BIAS_T, (q0 + ATTN_STRIP) // BIAS_T):
                lo = (b - q0 // BIAS_T) * BIAS_T
                piece = st[a * BIAS_T:(a + 1) * BIAS_T, lo:lo + BIAS_T]
                dist = b - a + (T // BIAS_T if which == 0 else 0)
                if dist < 0:
                    piece = jnp.full_like(piece, NEG_BIG)
                elif dist <= 1:
                    piece = piece + bias_ref[0, mp, 1 - dist]
                cols.append(piece)
            rows.append(jnp.concatenate(cols, axis=1))
        return jnp.concatenate(rows, axis=0)

    def step(buf, j, which, nxt, hold=0):
        early = len(strips) - hold
        for c in range(ATTN_LOOKAHEAD):
            put_scores(*nxt[:3], c, nxt[3])
        for c, sl in enumerate(strips):
            if c + ATTN_LOOKAHEAD < early:
                put_scores(*nxt[:3], c + ATTN_LOOKAHEAD, nxt[3])
            nk = n_keys(which, c)
            st = s_ref[buf, :nk, sl]
            m_old = m_ref[:, sl]
            m_new = jnp.maximum(m_old, smax_ref[buf, :, sl])
            alpha = jnp.exp2(m_old - m_new)
            p = jnp.exp2(st - m_new)
            l_ref[:, sl] = alpha * l_ref[:, sl] + jnp.sum(p, axis=0, keepdims=True)
            acc_ref[:, sl] = alpha * acc_ref[:, sl] + jnp.dot(
                vt_ref[0, j, :, :nk], p.astype(BF16), preferred_element_type=F32)
            m_ref[:, sl] = m_new
        for c in range(early, len(strips)):
            put_scores(*nxt[:3], c, nxt[3])

    def last_step(buf, next_which=None):
        step(buf, qi, 1, (0, 0, next_which, 1), hold=ATTN_HOLD)
        lam = (jnp.exp(jnp.sum(lq1_ref[...] * lk1_ref[...], axis=-1, keepdims=True))
               - jnp.exp(jnp.sum(lq2_ref[...] * lk2_ref[...], axis=-1, keepdims=True))
               + lambda_init)
        o_all = acc_ref[...] / l_ref[...]
        ot = o_all[:, :T] - lam * o_all[:, T:]
        ot = ot * lax.rsqrt(jnp.mean(ot * ot, axis=0, keepdims=True) + NORM_EPS)
        o_ref[0] = (ot.T * sg_ref[...] * (1.0 - lambda_init)).astype(o_ref.dtype)

    @pl.when(qi == 0)
    def _():
        for c in range(len(strips)):
            put_scores(2, 0, 1, c)
        last_step(2, next_which=0)

    @pl.when(qi >= 1)
    def _():
        n_far = qi - 1

        def pair(i, carry):
            j = 2 * i
            step(0, j, None, (1, j + 1, None, 0))
            step(1, j + 1, None, (0, j + 2, None, 0))
            return carry

        lax.fori_loop(0, jnp.maximum((n_far - 1) // 2, 0), pair, 0)

        @pl.when(n_far == 0)
        def _():
            step(0, qi - 1, 0, (1, qi, 1, 0))
            last_step(1)

        @pl.when((n_far >= 2) & (n_far % 2 == 0))
        def _():
            step(0, qi - 3, None, (1, qi - 2, None, 0))
            step(1, qi - 2, None, (0, qi - 1, 0, 0))
            step(0, qi - 1, 0, (1, qi, 1, 0))
            last_step(1)

        @pl.when(n_far % 2 == 1)
        def _():
            step(0, qi - 2, None, (1, qi - 1, 0, 0))
            step(1, qi - 1, 0, (2, qi, 1, 0))
            last_step(2)


def _attention(q, k, vt, bias, lq1, lk1, lq2, lk2, subln_g, lambda_init):
    B, S, _ = k.shape
    T = ATTN_T
    H, dv, dqk = ATTN_HEADS, ATTN_V_DIM, ATTN_QK_DIM
    bias = bias.reshape(H, 2, 2, BIAS_T, BIAS_T)
    small = lambda n: _const_spec((1, n))
    return pl.pallas_call(
        functools.partial(_attn_kernel, lambda_init=lambda_init),
        grid=(B, H, S // T),
        in_specs=[pl.BlockSpec((1, 1, 1, 2, 2 * dqk, T), lambda b, h, i: (b, h, i, 0, 0, 0)),
                  pl.BlockSpec((1, 1, 1, 2, 2 * dqk, T),
                               lambda b, h, i: (b, h, jnp.minimum(i + 1, S // T - 1), 0, 0, 0)),
                  pl.BlockSpec((1, S, 2 * dqk), lambda b, h, i: (b, 0, h)),
                  pl.BlockSpec((1, S // T, dv, T), lambda b, h, i: (b, 0, h, 0)),
                  pl.BlockSpec((1, 2, 2, BIAS_T, BIAS_T), lambda b, h, i: (h, 0, 0, 0, 0)),
                  small(dqk), small(dqk), small(dqk), small(dqk), small(dv)],
        out_specs=pl.BlockSpec((1, T, dv), lambda b, h, i: (b, i, h)),
        out_shape=jax.ShapeDtypeStruct((B, S, H * dv), BF16),
        scratch_shapes=[pltpu.VMEM((1, 2 * T), F32),
                        pltpu.VMEM((1, 2 * T), F32),
                        pltpu.VMEM((dv, 2 * T), F32),
                        pltpu.VMEM((3, T, 2 * T), F32),
                        pltpu.VMEM((3, 1, 2 * T), F32)],
        compiler_params=pltpu.CompilerParams(
            dimension_semantics=("parallel", "parallel", "arbitrary"),
            vmem_limit_bytes=VMEM_LIMIT),
        name="diff_attn",
    )(q, q, k, vt, bias, lq1.reshape(1, dqk), lk1.reshape(1, dqk), lq2.reshape(1, dqk),
      lk2.reshape(1, dqk), subln_g.reshape(1, dv))


def _rglru_kernel(xr_ref, gr_ref, cw_ref, cb_ref, wg_ref, ba_ref, bi_ref, L_ref,
                  y_ref, xpad_ref, h_ref, a_buf, b_buf):
    Tc, C = xr_ref.shape[1], xr_ref.shape[2]
    G = Tc // SUBLANES

    @pl.when(pl.program_id(1) == 0)
    def _():
        xpad_ref[:SUBLANES] = jnp.zeros((SUBLANES, C), F32)
        h_ref[...] = jnp.zeros_like(h_ref)

    x = xr_ref[0]
    xpad_ref[SUBLANES:] = x
    xc = x * cw_ref[CONV_WIDTH - 1:CONV_WIDTH, :] + cb_ref[...]
    for k in range(1, CONV_WIDTH):
        xc = xc + (xpad_ref[pl.ds(SUBLANES - k, Tc), :]
                   * cw_ref[CONV_WIDTH - 1 - k:CONV_WIDTH - k, :])
    xpad_ref[:SUBLANES] = x[Tc - SUBLANES:]

    gates = jnp.dot(xc.astype(BF16), wg_ref[...], preferred_element_type=F32)
    r = jax.nn.sigmoid(gates[:, :C] + ba_ref[...])
    ig = jax.nn.sigmoid(gates[:, C:] + bi_ref[...])
    L = L_ref[...]
    log_sig = jnp.minimum(L, 0.0) - jnp.log1p(jnp.exp(-jnp.abs(L)))
    log_a = r * (RG_LRU_C * log_sig)
    a = jnp.exp(log_a)
    v = -jnp.tanh(log_a) * (a * a + 1.0)
    root = jnp.where(v > 0.0, v * lax.rsqrt(v), 0.0)
    u = root * (ig * xc)

    a3 = a.reshape(G, SUBLANES, C)
    u3 = u.reshape(G, SUBLANES, C)
    sub3 = lax.broadcasted_iota(jnp.int32, (G, SUBLANES, C), 1)
    for k in (1, 2, 4):
        a_sh = pltpu.roll(a3, k, 1)
        u_sh = pltpu.roll(u3, k, 1)
        ok = sub3 >= k
        u3 = jnp.where(ok, a3 * u_sh + u3, u3)
        a3 = jnp.where(ok, a3 * a_sh, a3)
    a_buf[...] = a3
    b_buf[...] = u3

    def body(g, hprev):
        hg = a_buf[g] * hprev + b_buf[g]
        b_buf[g] = hg
        return hg[SUBLANES - 1:SUBLANES, :]

    h_last = lax.fori_loop(0, G, body, h_ref[...], unroll=8)
    h_ref[...] = h_last

    h = b_buf[...].reshape(Tc, C)
    gr = gr_ref[0]
    c0 = math.sqrt(2.0 / math.pi)
    t = jnp.tanh(gr * (c0 + (c0 * 0.044715) * (gr * gr)))
    y_ref[0] = (h * (gr * (0.5 + 0.5 * t))).astype(y_ref.dtype)


def _rglru(xr, gr, conv_w, conv_b, w_gates, b_a, b_i, lru_L):
    B, S, C = xr.shape
    Tc = RNN_TC
    tok = pl.BlockSpec((1, Tc, C), lambda b, t: (b, t, 0))
    row = lambda: _const_spec((1, C))
    return pl.pallas_call(
        _rglru_kernel,
        grid=(B, S // Tc),
        in_specs=[tok, tok, _const_spec((CONV_WIDTH, C)), row(),
                  _const_spec(w_gates.shape), row(), row(), row()],
        out_specs=tok,
        out_shape=jax.ShapeDtypeStruct((B, S, C), BF16),
        scratch_shapes=[pltpu.VMEM((Tc + SUBLANES, C), F32),
                        pltpu.VMEM((1, C), F32),
                        pltpu.VMEM((Tc // SUBLANES, SUBLANES, C), F32),
                        pltpu.VMEM((Tc // SUBLANES, SUBLANES, C), F32)],
        compiler_params=pltpu.CompilerParams(
            dimension_semantics=("parallel", "arbitrary"), vmem_limit_bytes=VMEM_LIMIT),
        name="rg_lru",
    )(xr, gr, conv_w, conv_b.reshape(1, C), w_gates, b_a.reshape(1, C),
      b_i.reshape(1, C), lru_L.reshape(1, C))


def _to_bf16(w, l):
    return w[l].astype(BF16)


def _block_diag(w):
    n, bw, _ = w.shape
    eye = jnp.eye(n, dtype=w.dtype)
    return (eye[:, None, :, None] * w[:, :, None, :]).reshape(n * bw, n * bw)


def kernel(x, c, rel_bias, ada_w, ada_b, norm_g, ffn1_w1, ffn1_w3, ffn1_w2, w_in, lam_q1, lam_k1, lam_q2, lam_k2, subln_g, conv_w, conv_b, gate_a_w, gate_a_b, gate_i_w, gate_i_b, lru_L, w_out, ffn2_w1, ffn2_w3, ffn2_w2, final_g):
    B, S, D = x.shape
    depth = ada_w.shape[0]
    c_pad = jnp.zeros((SUBLANES, D), F32).at[:B].set(c)
    bias = _bias_tiles(rel_bias, BIAS_T)
    for l in range(depth):
        mod = _modulation(c_pad, ada_w[l], ada_b[l])[:, :B]
        sh1, sc1, g1, sh2, sc2, g2, sh3, sc3, g3 = [m.reshape(B, 1, D) for m in mod]

        w_in_b = _to_bf16(w_in, l)
        w_qvt = jnp.concatenate([w_in_b[:, :ATTN_WIDTH],
                                 w_in_b[:, 2 * ATTN_WIDTH:3 * ATTN_WIDTH]], axis=1).T
        x, q, k, vt, xr, gr = _ffn_in(
            x, norm_g[l, 0], sh1, sc1, g1,
            _to_bf16(ffn1_w1, l), _to_bf16(ffn1_w3, l), _to_bf16(ffn1_w2, l),
            norm_g[l, 1], sh2, sc2, w_in_b, w_qvt)

        lambda_init = 0.8 - 0.6 * math.exp(-0.3 * l)
        o = _attention(q, k, vt, bias, lam_q1[l], lam_k1[l], lam_q2[l], lam_k2[l],
                       subln_g[l], lambda_init)

        w_gates = jnp.concatenate(
            [_block_diag(gate_a_w[l]), _block_diag(gate_i_w[l])], axis=1).astype(BF16)
        yr = _rglru(xr, gr, conv_w[l], conv_b[l], w_gates, gate_a_b[l], gate_i_b[l],
                    lru_L[l])

        w_out_c = _to_bf16(w_out, l).reshape(2, ATTN_WIDTH, D)
        last = l == depth - 1
        x = _out_ffn(x, o, yr, g2, w_out_c, norm_g[l, 2], sh3, sc3, g3,
                     _to_bf16(ffn2_w1, l), _to_bf16(ffn2_w3, l), _to_bf16(ffn2_w2, l),
                     final_g=final_g if last else None)
    return x
```

```python
import functools
import math

import jax
import jax.numpy as jnp
from jax import lax
from jax.experimental import pallas as pl
from jax.experimental.pallas import tpu as pltpu

F32 = jnp.float32
BF16 = jnp.bfloat16

ATTN_HEADS = 4
ATTN_QK_DIM = 64
ATTN_V_DIM = 2 * ATTN_QK_DIM
ATTN_WIDTH = ATTN_HEADS * ATTN_V_DIM
RNN_BLOCKS = 8
CONV_WIDTH = 4
RG_LRU_C = 8.0
N_BUCKETS = 32
MAX_DISTANCE = 128
NORM_EPS = 1e-6
N_MOD = 9

SUBLANES = 8
VMEM_LIMIT = 56 * 1024 * 1024

FFN_TM = 512
FFN_FC = 256
ATTN_T = 1024
ATTN_STRIP = 256
ATTN_KCHUNK = 256
ATTN_LOOKAHEAD = 1
ATTN_HOLD = 3
BIAS_T = 128
RNN_TC = 1024
NEG_BIG = -1e30
LOG2E = math.log2(math.e)


def _rms(x):
    return x * lax.rsqrt(jnp.mean(x * x, axis=-1, keepdims=True) + NORM_EPS)


def _const_spec(shape):
    nd = len(shape)
    return pl.BlockSpec(shape, lambda *_: (0,) * nd, pipeline_mode=pl.Buffered(1))


def _mod_kernel(c_ref, w_ref, b_ref, o_ref):
    c = c_ref[...]
    ca = c * jax.nn.sigmoid(c)
    o_ref[0] = jnp.dot(ca, w_ref[...], preferred_element_type=F32,
                       precision=lax.Precision.HIGHEST) + b_ref[0]


def _modulation(c_pad, ada_w, ada_b):
    rows, d = c_pad.shape
    n = ada_w.shape[1] // d
    return pl.pallas_call(
        _mod_kernel,
        grid=(n,),
        in_specs=[pl.BlockSpec((rows, d), lambda j: (0, 0)),
                  pl.BlockSpec((d, d), lambda j: (0, j)),
                  pl.BlockSpec((1, 1, d), lambda j: (j, 0, 0))],
        out_specs=pl.BlockSpec((1, rows, d), lambda j: (j, 0, 0)),
        out_shape=jax.ShapeDtypeStruct((n, rows, d), F32),
        compiler_params=pltpu.CompilerParams(vmem_limit_bytes=VMEM_LIMIT),
        name="adaln_mod",
    )(c_pad, ada_w, ada_b.reshape(n, 1, d))


def _modulated(x, ng_ref, sh_ref, sc_ref):
    return (_rms(x) * ng_ref[...] * (1.0 + sc_ref[0]) + sh_ref[0]).astype(BF16)


def _swiglu_residual(x, hb, g_ref, w1_ref, w3_ref, w2_ref, acc_ref):
    n_chunks = w1_ref.shape[1] // FFN_FC
    cols = lambda ci: slice(ci * FFN_FC, (ci + 1) * FFN_FC)
    up = lambda ci: (jnp.dot(hb, w1_ref[:, cols(ci)], preferred_element_type=F32),
                     jnp.dot(hb, w3_ref[:, cols(ci)], preferred_element_type=F32))
    a, b = up(0)
    for ci in range(n_chunks):
        if ci + 1 < n_chunks:
            a_next, b_next = up(ci + 1)
        u = (a * jax.nn.sigmoid(a) * b).astype(BF16)
        down = jnp.dot(u, w2_ref[cols(ci), :], preferred_element_type=F32)
        if ci == 0:
            acc_ref[...] = down
        else:
            acc_ref[...] += down
        a, b = a_next, b_next
    return x + 0.5 * g_ref[0] * acc_ref[...]


def _ffn_in_kernel(x_ref, ng1_ref, sh1_ref, sc1_ref, g1_ref, w1_ref, w3_ref, w2_ref,
                   ng2_ref, sh2_ref, sc2_ref, win_ref, wqvt_ref,
                   x1_ref, q_ref, k_ref, vt_ref, xr_ref, gr_ref, acc_ref):
    x = x_ref[0]
    y = _swiglu_residual(x, _modulated(x, ng1_ref, sh1_ref, sc1_ref),
                         g1_ref, w1_ref, w3_ref, w2_ref, acc_ref)
    x1_ref[0] = y
    hb = _modulated(y, ng2_ref, sh2_ref, sc2_ref)
    scale = ATTN_QK_DIM ** -0.5 * LOG2E
    W = k_ref.shape[-1]
    proj = lambda n: jnp.dot(hb, win_ref[:, n * W:(n + 1) * W], preferred_element_type=F32)
    k_ref[0] = proj(1).astype(BF16)
    qvt = lax.dot_general(wqvt_ref[...], hb, (((1,), (1,)), ((), ())),
                          preferred_element_type=F32)
    vt_ref[0, 0] = qvt[W:].astype(BF16)
    qt = (qvt[:W] * scale).astype(BF16)
    dqk = ATTN_QK_DIM
    zeros = jnp.zeros((dqk, qt.shape[1]), BF16)
    for h in range(ATTN_HEADS):
        top = qt[2 * dqk * h:2 * dqk * h + dqk]
        bot = qt[2 * dqk * h + dqk:2 * dqk * (h + 1)]
        q_ref[0, h, 0, 0] = jnp.concatenate([top, zeros], axis=0)
        q_ref[0, h, 0, 1] = jnp.concatenate([zeros, bot], axis=0)
    xr_ref[0] = proj(3)
    gr_ref[0] = proj(4)


def _ffn_in(x, ng1, sh1, sc1, g1, w1, w3, w2, ng2, sh2, sc2, w_in_b, w_qvt):
    B, S, D = x.shape
    tm, T = FFN_TM, ATTN_T
    W = w_qvt.shape[0] // 2
    H, dqk = ATTN_HEADS, ATTN_QK_DIM
    per_t = T // tm
    vec = pl.BlockSpec((1, 1, D), lambda b, i: (b, 0, 0))
    row = _const_spec((1, D))
    full = pl.BlockSpec((1, tm, D), lambda b, i: (b, i, 0))
    tok = pl.BlockSpec((1, tm, W), lambda b, i: (b, i, 0))
    vt_spec = pl.BlockSpec((1, 1, W, tm), lambda b, i: (b, i // per_t, 0, i % per_t))
    q_spec = pl.BlockSpec((1, H, 1, 2, 2 * dqk, tm),
                          lambda b, i: (b, 0, i // per_t, 0, 0, i % per_t))
    return pl.pallas_call(
        _ffn_in_kernel,
        grid=(B, S // tm),
        in_specs=[full, row, vec, vec, vec,
                  _const_spec(w1.shape), _const_spec(w3.shape), _const_spec(w2.shape),
                  row, vec, vec, _const_spec(w_in_b.shape), _const_spec(w_qvt.shape)],
        out_specs=[full, q_spec, tok, vt_spec, tok, tok],
        out_shape=[jax.ShapeDtypeStruct((B, S, D), F32),
                   jax.ShapeDtypeStruct((B, H, S // T, 2, 2 * dqk, T), BF16),
                   jax.ShapeDtypeStruct((B, S, W), BF16),
                   jax.ShapeDtypeStruct((B, S // T, W, T), BF16)]
                  + [jax.ShapeDtypeStruct((B, S, W), F32)] * 2,
        scratch_shapes=[pltpu.VMEM((tm, D), F32)],
        compiler_params=pltpu.CompilerParams(
            dimension_semantics=("parallel", "parallel"), vmem_limit_bytes=VMEM_LIMIT),
        name="ffn_in_proj",
    )(x, ng1.reshape(1, D), sh1, sc1, g1, w1, w3, w2, ng2.reshape(1, D), sh2, sc2,
      w_in_b, w_qvt)


def _out_ffn_kernel(x_ref, o_ref, yr_ref, g2_ref, wout_ref, ng_ref, sh_ref, sc_ref, g_ref,
                    w1_ref, w3_ref, w2_ref, *rest, final_norm):
    if final_norm:
        fg_ref, out_ref, acc_ref = rest
    else:
        out_ref, acc_ref = rest
    mix = (jnp.dot(o_ref[0], wout_ref[0], preferred_element_type=F32)
           + jnp.dot(yr_ref[0], wout_ref[1], preferred_element_type=F32))
    x = x_ref[0] + g2_ref[0] * mix
    y = _swiglu_residual(x, _modulated(x, ng_ref, sh_ref, sc_ref),
                         g_ref, w1_ref, w3_ref, w2_ref, acc_ref)
    if final_norm:
        y = _rms(y) * fg_ref[...]
    out_ref[0] = y


def _out_ffn(x, o, yr, g2, w_out_c, ng, sh, sc, g, w1, w3, w2, final_g=None):
    B, S, D = x.shape
    tm = FFN_TM
    W = o.shape[-1]
    final_norm = final_g is not None
    vec = pl.BlockSpec((1, 1, D), lambda b, i: (b, 0, 0))
    row = _const_spec((1, D))
    full = pl.BlockSpec((1, tm, D), lambda b, i: (b, i, 0))
    half = pl.BlockSpec((1, tm, W), lambda b, i: (b, i, 0))
    in_specs = [full, half, half, vec, _const_spec(w_out_c.shape), row, vec, vec, vec,
                _const_spec(w1.shape), _const_spec(w3.shape), _const_spec(w2.shape)]
    args = [x, o, yr, g2, w_out_c, ng.reshape(1, D), sh, sc, g, w1, w3, w2]
    if final_norm:
        in_specs.append(row)
        args.append(final_g.reshape(1, D))
    return pl.pallas_call(
        functools.partial(_out_ffn_kernel, final_norm=final_norm),
        grid=(B, S // tm),
        in_specs=in_specs,
        out_specs=full,
        out_shape=jax.ShapeDtypeStruct((B, S, D), F32),
        scratch_shapes=[pltpu.VMEM((tm, D), F32)],
        compiler_params=pltpu.CompilerParams(
            dimension_semantics=("parallel", "parallel"), vmem_limit_bytes=VMEM_LIMIT),
        name="out_proj_ffn_final" if final_norm else "out_proj_ffn",
    )(*args)


def _bias_kernel(rb_ref, o_ref):
    mp = pl.program_id(0)
    T = o_ref.shape[-1]
    key = lax.broadcasted_iota(jnp.int32, (T, T), 0)
    qry = lax.broadcasted_iota(jnp.int32, (T, T), 1)
    max_exact = N_BUCKETS // 2
    last = rb_ref[N_BUCKETS - 1, mp]
    for which in range(2):
        rel = qry - key + (T if which == 0 else 0)
        n = jnp.maximum(rel, 0)
        nf = jnp.maximum(n, 1).astype(F32)
        large = max_exact + (jnp.log(nf / max_exact) / math.log(MAX_DISTANCE / max_exact)
                             * (N_BUCKETS - max_exact)).astype(jnp.int32)
        large = jnp.minimum(large, N_BUCKETS - 1)
        bucket = jnp.where(n < max_exact, n, large)
        val = jnp.zeros((T, T), F32)
        for bk in range(N_BUCKETS - 1):
            val = jnp.where(bucket == bk, (rb_ref[bk, mp] - last) * LOG2E, val)
        if which == 1:
            val = jnp.where(rel >= 0, val, NEG_BIG)
        o_ref[0, which] = val


def _bias_tiles(rel_bias, T):
    n_maps = rel_bias.shape[1]
    return pl.pallas_call(
        _bias_kernel,
        grid=(n_maps,),
        in_specs=[pl.BlockSpec(memory_space=pltpu.SMEM)],
        out_specs=pl.BlockSpec((1, 2, T, T), lambda m: (m, 0, 0, 0)),
        out_shape=jax.ShapeDtypeStruct((n_maps, 2, T, T), F32),
        name="rel_bias_tiles",
    )(rel_bias)


def _attn_kernel(q_ref, qn_ref, k_ref, vt_ref, bias_ref, lq1_ref, lk1_ref, lq2_ref, lk2_ref,
                 sg_ref, o_ref, m_ref, l_ref, acc_ref, s_ref, smax_ref, *, lambda_init):
    T = q_ref.shape[-1]
    qi = pl.program_id(2)
    q_refs = (q_ref, qn_ref)

    m_ref[...] = jnp.full_like(m_ref, NEG_BIG)
    l_ref[...] = jnp.zeros_like(l_ref)
    acc_ref[...] = jnp.zeros_like(acc_ref)

    strips = [slice(c * ATTN_STRIP, (c + 1) * ATTN_STRIP)
              for c in range(2 * T // ATTN_STRIP)]

    def n_keys(which, c):
        return strips[c].start % T + ATTN_STRIP if which == 1 else T

    def put_scores(buf, j, which, c, slot=0):
        nk = n_keys(which, c)
        start = j * T if isinstance(j, int) else pl.multiple_of(j * T, T)
        kj = k_ref[0, pl.ds(start, nk), :]
        mp, q0 = divmod(strips[c].start, T)
        qt = q_refs[slot][0, 0, 0, mp, :, q0:q0 + ATTN_STRIP]
        st = jnp.dot(kj, qt, preferred_element_type=F32)
        if which is not None:
            st = biased(st, which, mp, q0)
        s_ref[buf, :nk, strips[c]] = st
        smax_ref[buf, :, strips[c]] = jnp.max(st, axis=0, keepdims=True)

    def biased(st, which, mp, q0):
        rows = []
        for a in range(st.shape[0] // BIAS_T):
            cols = []
            for b in range(q0 // BIAS_T, (q0 + ATTN_STRIP) // BIAS_T):
                lo = (b - q0 // BIAS_T) * BIAS_T
                piece = st[a * BIAS_T:(a + 1) * BIAS_T, lo:lo + BIAS_T]
                dist = b - a + (T // BIAS_T if which == 0 else 0)
                if dist < 0:
                    piece = jnp.full_like(piece, NEG_BIG)
                elif dist <= 1:
                    piece = piece + bias_ref[0, mp, 1 - dist]
                cols.append(piece)
            rows.append(jnp.concatenate(cols, axis=1))
        return jnp.concatenate(rows, axis=0)

    def step(buf, j, which, nxt, hold=0):
        early = len(strips) - hold
        for c in range(ATTN_LOOKAHEAD):
            put_scores(*nxt[:3], c, nxt[3])
        for c, sl in enumerate(strips):
            if c + ATTN_LOOKAHEAD < early:
                put_scores(*nxt[:3], c + ATTN_LOOKAHEAD, nxt[3])
            nk = n_keys(which, c)
            m_old = m_ref[:, sl]
            m_new = jnp.maximum(m_old, smax_ref[buf, :, sl])
            alpha = jnp.exp2(m_old - m_new)
            l_new = alpha * l_ref[:, sl]
            acc_new = alpha * acc_ref[:, sl]
            for k0 in range(0, nk, ATTN_KCHUNK):
                keys = slice(k0, k0 + ATTN_KCHUNK)
                p = jnp.exp2(s_ref[buf, keys, sl] - m_new)
                l_new = l_new + jnp.sum(p, axis=0, keepdims=True)
                acc_new = acc_new + jnp.dot(vt_ref[0, j, :, keys], p.astype(BF16),
                                            preferred_element_type=F32)
            l_ref[:, sl] = l_new
            acc_ref[:, sl] = acc_new
            m_ref[:, sl] = m_new
        for c in range(early, len(strips)):
            put_scores(*nxt[:3], c, nxt[3])

    def last_step(buf, next_which=None):
        step(buf, qi, 1, (0, 0, next_which, 1), hold=ATTN_HOLD)
        lam = (jnp.exp(jnp.sum(lq1_ref[...] * lk1_ref[...], axis=-1, keepdims=True))
               - jnp.exp(jnp.sum(lq2_ref[...] * lk2_ref[...], axis=-1, keepdims=True))
               + lambda_init)
        o_all = acc_ref[...] / l_ref[...]
        ot = o_all[:, :T] - lam * o_all[:, T:]
        ot = ot * lax.rsqrt(jnp.mean(ot * ot, axis=0, keepdims=True) + NORM_EPS)
        o_ref[0] = (ot.T * sg_ref[...] * (1.0 - lambda_init)).astype(o_ref.dtype)

    @pl.when(qi == 0)
    def _():
        for c in range(len(strips)):
            put_scores(2, 0, 1, c)
        last_step(2, next_which=0)

    @pl.when(qi >= 1)
    def _():
        n_far = qi - 1

        def pair(i, carry):
            j = 2 * i
            step(0, j, None, (1, j + 1, None, 0))
            step(1, j + 1, None, (0, j + 2, None, 0))
            return carry

        lax.fori_loop(0, jnp.maximum((n_far - 1) // 2, 0), pair, 0)

        @pl.when(n_far == 0)
        def _():
            step(0, qi - 1, 0, (1, qi, 1, 0))
            last_step(1)

        @pl.when((n_far >= 2) & (n_far % 2 == 0))
        def _():
            step(0, qi - 3, None, (1, qi - 2, None, 0))
            step(1, qi - 2, None, (0, qi - 1, 0, 0))
            step(0, qi - 1, 0, (1, qi, 1, 0))
            last_step(1)

        @pl.when(n_far % 2 == 1)
        def _():
            step(0, qi - 2, None, (1, qi - 1, 0, 0))
            step(1, qi - 1, 0, (2, qi, 1, 0))
            last_step(2)


def _attention(q, k, vt, bias, lq1, lk1, lq2, lk2, subln_g, lambda_init):
    B, S, _ = k.shape
    T = ATTN_T
    H, dv, dqk = ATTN_HEADS, ATTN_V_DIM, ATTN_QK_DIM
    bias = bias.reshape(H, 2, 2, BIAS_T, BIAS_T)
    small = lambda n: _const_spec((1, n))
    return pl.pallas_call(
        functools.partial(_attn_kernel, lambda_init=lambda_init),
        grid=(B, H, S // T),
        in_specs=[pl.BlockSpec((1, 1, 1, 2, 2 * dqk, T), lambda b, h, i: (b, h, i, 0, 0, 0)),
                  pl.BlockSpec((1, 1, 1, 2, 2 * dqk, T),
                               lambda b, h, i: (b, h, jnp.minimum(i + 1, S // T - 1), 0, 0, 0)),
                  pl.BlockSpec((1, S, 2 * dqk), lambda b, h, i: (b, 0, h)),
                  pl.BlockSpec((1, S // T, dv, T), lambda b, h, i: (b, 0, h, 0)),
                  pl.BlockSpec((1, 2, 2, BIAS_T, BIAS_T), lambda b, h, i: (h, 0, 0, 0, 0)),
                  small(dqk), small(dqk), small(dqk), small(dqk), small(dv)],
        out_specs=pl.BlockSpec((1, T, dv), lambda b, h, i: (b, i, h)),
        out_shape=jax.ShapeDtypeStruct((B, S, H * dv), BF16),
        scratch_shapes=[pltpu.VMEM((1, 2 * T), F32),
                        pltpu.VMEM((1, 2 * T), F32),
                        pltpu.VMEM((dv, 2 * T), F32),
                        pltpu.VMEM((3, T, 2 * T), F32),
                        pltpu.VMEM((3, 1, 2 * T), F32)],
        compiler_params=pltpu.CompilerParams(
            dimension_semantics=("parallel", "parallel", "arbitrary"),
            vmem_limit_bytes=VMEM_LIMIT),
        name="diff_attn",
    )(q, q, k, vt, bias, lq1.reshape(1, dqk), lk1.reshape(1, dqk), lq2.reshape(1, dqk),
      lk2.reshape(1, dqk), subln_g.reshape(1, dv))


def _rglru_kernel(xr_ref, gr_ref, cw_ref, cb_ref, wg_ref, ba_ref, bi_ref, L_ref,
                  y_ref, xpad_ref, h_ref, a_buf, b_buf):
    Tc, C = xr_ref.shape[1], xr_ref.shape[2]
    G = Tc // SUBLANES

    @pl.when(pl.program_id(1) == 0)
    def _():
        xpad_ref[:SUBLANES] = jnp.zeros((SUBLANES, C), F32)
        h_ref[...] = jnp.zeros_like(h_ref)

    x = xr_ref[0]
    xpad_ref[SUBLANES:] = x
    xc = x * cw_ref[CONV_WIDTH - 1:CONV_WIDTH, :] + cb_ref[...]
    for k in range(1, CONV_WIDTH):
        xc = xc + (xpad_ref[pl.ds(SUBLANES - k, Tc), :]
                   * cw_ref[CONV_WIDTH - 1 - k:CONV_WIDTH - k, :])
    xpad_ref[:SUBLANES] = x[Tc - SUBLANES:]

    gates = jnp.dot(xc.astype(BF16), wg_ref[...], preferred_element_type=F32)
    r = jax.nn.sigmoid(gates[:, :C] + ba_ref[...])
    ig = jax.nn.sigmoid(gates[:, C:] + bi_ref[...])
    L = L_ref[...]
    log_sig = jnp.minimum(L, 0.0) - jnp.log1p(jnp.exp(-jnp.abs(L)))
    log_a = r * (RG_LRU_C * log_sig)
    a = jnp.exp(log_a)
    v = -jnp.tanh(log_a) * (a * a + 1.0)
    root = jnp.where(v > 0.0, v * lax.rsqrt(v), 0.0)
    u = root * (ig * xc)

    a3 = a.reshape(G, SUBLANES, C)
    u3 = u.reshape(G, SUBLANES, C)
    sub3 = lax.broadcasted_iota(jnp.int32, (G, SUBLANES, C), 1)
    for k in (1, 2, 4):
        a_sh = pltpu.roll(a3, k, 1)
        u_sh = pltpu.roll(u3, k, 1)
        ok = sub3 >= k
        u3 = jnp.where(ok, a3 * u_sh + u3, u3)
        a3 = jnp.where(ok, a3 * a_sh, a3)
    a_buf[...] = a3
    b_buf[...] = u3

    def body(g, hprev):
        hg = a_buf[g] * hprev + b_buf[g]
        b_buf[g] = hg
        return hg[SUBLANES - 1:SUBLANES, :]

    h_last = lax.fori_loop(0, G, body, h_ref[...], unroll=8)
    h_ref[...] = h_last

    h = b_buf[...].reshape(Tc, C)
    gr = gr_ref[0]
    c0 = math.sqrt(2.0 / math.pi)
    t = jnp.tanh(gr * (c0 + (c0 * 0.044715) * (gr * gr)))
    y_ref[0] = (h * (gr * (0.5 + 0.5 * t))).astype(y_ref.dtype)


def _rglru(xr, gr, conv_w, conv_b, w_gates, b_a, b_i, lru_L):
    B, S, C = xr.shape
    Tc = RNN_TC
    tok = pl.BlockSpec((1, Tc, C), lambda b, t: (b, t, 0))
    row = lambda: _const_spec((1, C))
    return pl.pallas_call(
        _rglru_kernel,
        grid=(B, S // Tc),
        in_specs=[tok, tok, _const_spec((CONV_WIDTH, C)), row(),
                  _const_spec(w_gates.shape), row(), row(), row()],
        out_specs=tok,
        out_shape=jax.ShapeDtypeStruct((B, S, C), BF16),
        scratch_shapes=[pltpu.VMEM((Tc + SUBLANES, C), F32),
                        pltpu.VMEM((1, C), F32),
                        pltpu.VMEM((Tc // SUBLANES, SUBLANES, C), F32),
                        pltpu.VMEM((Tc // SUBLANES, SUBLANES, C), F32)],
        compiler_params=pltpu.CompilerParams(
            dimension_semantics=("parallel", "arbitrary"), vmem_limit_bytes=VMEM_LIMIT),
        name="rg_lru",
    )(xr, gr, conv_w, conv_b.reshape(1, C), w_gates, b_a.reshape(1, C),
      b_i.reshape(1, C), lru_L.reshape(1, C))


def _to_bf16(w, l):
    return w[l].astype(BF16)


def _block_diag(w):
    n, bw, _ = w.shape
    eye = jnp.eye(n, dtype=w.dtype)
    return (eye[:, None, :, None] * w[:, :, None, :]).reshape(n * bw, n * bw)


def kernel(x, c, rel_bias, ada_w, ada_b, norm_g, ffn1_w1, ffn1_w3, ffn1_w2, w_in, lam_q1, lam_k1, lam_q2, lam_k2, subln_g, conv_w, conv_b, gate_a_w, gate_a_b, gate_i_w, gate_i_b, lru_L, w_out, ffn2_w1, ffn2_w3, ffn2_w2, final_g):
    B, S, D = x.shape
    depth = ada_w.shape[0]
    c_pad = jnp.zeros((SUBLANES, D), F32).at[:B].set(c)
    bias = _bias_tiles(rel_bias, BIAS_T)
    for l in range(depth):
        mod = _modulation(c_pad, ada_w[l], ada_b[l])[:, :B]
        sh1, sc1, g1, sh2, sc2, g2, sh3, sc3, g3 = [m.reshape(B, 1, D) for m in mod]

        w_in_b = _to_bf16(w_in, l)
        w_qvt = jnp.concatenate([w_in_b[:, :ATTN_WIDTH],
                                 w_in_b[:, 2 * ATTN_WIDTH:3 * ATTN_WIDTH]], axis=1).T
        x, q, k, vt, xr, gr = _ffn_in(
            x, norm_g[l, 0], sh1, sc1, g1,
            _to_bf16(ffn1_w1, l), _to_bf16(ffn1_w3, l), _to_bf16(ffn1_w2, l),
            norm_g[l, 1], sh2, sc2, w_in_b, w_qvt)

        lambda_init = 0.8 - 0.6 * math.exp(-0.3 * l)
        o = _attention(q, k, vt, bias, lam_q1[l], lam_k1[l], lam_q2[l], lam_k2[l],
                       subln_g[l], lambda_init)

        w_gates = jnp.concatenate(
            [_block_diag(gate_a_w[l]), _block_diag(gate_i_w[l])], axis=1).astype(BF16)
        yr = _rglru(xr, gr, conv_w[l], conv_b[l], w_gates, gate_a_b[l], gate_i_b[l],
                    lru_L[l])

        w_out_c = _to_bf16(w_out, l).reshape(2, ATTN_WIDTH, D)
        last = l == depth - 1
        x = _out_ffn(x, o, yr, g2, w_out_c, norm_g[l, 2], sh3, sc3, g3,
                     _to_bf16(ffn2_w1, l), _to_bf16(ffn2_w3, l), _to_bf16(ffn2_w2, l),
                     final_g=final_g if last else None)
    return x
```

```python
import functools
import math

import jax
import jax.numpy as jnp
from jax import lax
from jax.experimental import pallas as pl
from jax.experimental.pallas import tpu as pltpu

F32 = jnp.float32
BF16 = jnp.bfloat16

ATTN_HEADS = 4
ATTN_QK_DIM = 64
ATTN_V_DIM = 2 * ATTN_QK_DIM
ATTN_WIDTH = ATTN_HEADS * ATTN_V_DIM
RNN_BLOCKS = 8
CONV_WIDTH = 4
RG_LRU_C = 8.0
N_BUCKETS = 32
MAX_DISTANCE = 128
NORM_EPS = 1e-6
N_MOD = 9

SUBLANES = 8
VMEM_LIMIT = 56 * 1024 * 1024

FFN_TM = 512
FFN_FC = 256
ATTN_T = 1024
ATTN_STRIP = 256
ATTN_KCHUNK = 256
ATTN_LOOKAHEAD = 1
ATTN_HOLD = 3
BIAS_T = 128
RNN_TC = 1024
NEG_BIG = -1e30
LOG2E = math.log2(math.e)


def _rms(x):
    return x * lax.rsqrt(jnp.mean(x * x, axis=-1, keepdims=True) + NORM_EPS)


def _const_spec(shape):
    nd = len(shape)
    return pl.BlockSpec(shape, lambda *_: (0,) * nd, pipeline_mode=pl.Buffered(1))


def _mod_kernel(c_ref, w_ref, b_ref, o_ref):
    c = c_ref[...]
    ca = c * jax.nn.sigmoid(c)
    o_ref[0] = jnp.dot(ca, w_ref[...], preferred_element_type=F32,
                       precision=lax.Precision.HIGHEST) + b_ref[0]


def _modulation(c_pad, ada_w, ada_b):
    rows, d = c_pad.shape
    n = ada_w.shape[1] // d
    return pl.pallas_call(
        _mod_kernel,
        grid=(n,),
        in_specs=[pl.BlockSpec((rows, d), lambda j: (0, 0)),
                  pl.BlockSpec((d, d), lambda j: (0, j)),
                  pl.BlockSpec((1, 1, d), lambda j: (j, 0, 0))],
        out_specs=pl.BlockSpec((1, rows, d), lambda j: (j, 0, 0)),
        out_shape=jax.ShapeDtypeStruct((n, rows, d), F32),
        compiler_params=pltpu.CompilerParams(vmem_limit_bytes=VMEM_LIMIT),
        name="adaln_mod",
    )(c_pad, ada_w, ada_b.reshape(n, 1, d))


def _modulated(x, ng_ref, sh_ref, sc_ref):
    return (_rms(x) * ng_ref[...] * (1.0 + sc_ref[0]) + sh_ref[0]).astype(BF16)


def _swiglu_residual(x, hb, g_ref, w1_ref, w3_ref, w2_ref, acc_ref):
    n_chunks = w1_ref.shape[1] // FFN_FC
    cols = lambda ci: slice(ci * FFN_FC, (ci + 1) * FFN_FC)
    up = lambda ci: (jnp.dot(hb, w1_ref[:, cols(ci)], preferred_element_type=F32),
                     jnp.dot(hb, w3_ref[:, cols(ci)], preferred_element_type=F32))
    a, b = up(0)
    for ci in range(n_chunks):
        if ci + 1 < n_chunks:
            a_next, b_next = up(ci + 1)
        u = (a * jax.nn.sigmoid(a) * b).astype(BF16)
        down = jnp.dot(u, w2_ref[cols(ci), :], preferred_element_type=F32)
        if ci == 0:
            acc_ref[...] = down
        else:
            acc_ref[...] += down
        a, b = a_next, b_next
    return x + 0.5 * g_ref[0] * acc_ref[...]


def _ffn_in_kernel(x_ref, ng1_ref, sh1_ref, sc1_ref, g1_ref, w1_ref, w3_ref, w2_ref,
                   ng2_ref, sh2_ref, sc2_ref, win_ref, wqvt_ref,
                   x1_ref, q_ref, k_ref, vt_ref, xr_ref, gr_ref, acc_ref):
    x = x_ref[0]
    y = _swiglu_residual(x, _modulated(x, ng1_ref, sh1_ref, sc1_ref),
                         g1_ref, w1_ref, w3_ref, w2_ref, acc_ref)
    x1_ref[0] = y
    hb = _modulated(y, ng2_ref, sh2_ref, sc2_ref)
    scale = ATTN_QK_DIM ** -0.5 * LOG2E
    W = k_ref.shape[-1]
    proj = lambda n: jnp.dot(hb, win_ref[:, n * W:(n + 1) * W], preferred_element_type=F32)
    k_ref[0] = proj(1).astype(BF16)
    qvt = lax.dot_general(wqvt_ref[...], hb, (((1,), (1,)), ((), ())),
                          preferred_element_type=F32)
    vt_ref[0, 0] = qvt[W:].astype(BF16)
    qt = (qvt[:W] * scale).astype(BF16)
    dqk = ATTN_QK_DIM
    zeros = jnp.zeros((dqk, qt.shape[1]), BF16)
    for h in range(ATTN_HEADS):
        top = qt[2 * dqk * h:2 * dqk * h + dqk]
        bot = qt[2 * dqk * h + dqk:2 * dqk * (h + 1)]
        q_ref[0, h, 0, 0] = jnp.concatenate([top, zeros], axis=0)
        q_ref[0, h, 0, 1] = jnp.concatenate([zeros, bot], axis=0)
    xr_ref[0] = proj(3)
    gr_ref[0] = proj(4)


def _ffn_in(x, ng1, sh1, sc1, g1, w1, w3, w2, ng2, sh2, sc2, w_in_b, w_qvt):
    B, S, D = x.shape
    tm, T = FFN_TM, ATTN_T
    W = w_qvt.shape[0] // 2
    H, dqk = ATTN_HEADS, ATTN_QK_DIM
    per_t = T // tm
    vec = pl.BlockSpec((1, 1, D), lambda b, i: (b, 0, 0))
    row = _const_spec((1, D))
    full = pl.BlockSpec((1, tm, D), lambda b, i: (b, i, 0))
    tok = pl.BlockSpec((1, tm, W), lambda b, i: (b, i, 0))
    vt_spec = pl.BlockSpec((1, 1, W, tm), lambda b, i: (b, i // per_t, 0, i % per_t))
    q_spec = pl.BlockSpec((1, H, 1, 2, 2 * dqk, tm),
                          lambda b, i: (b, 0, i // per_t, 0, 0, i % per_t))
    return pl.pallas_call(
        _ffn_in_kernel,
        grid=(B, S // tm),
        in_specs=[full, row, vec, vec, vec,
                  _const_spec(w1.shape), _const_spec(w3.shape), _const_spec(w2.shape),
                  row, vec, vec, _const_spec(w_in_b.shape), _const_spec(w_qvt.shape)],
        out_specs=[full, q_spec, tok, vt_spec, tok, tok],
        out_shape=[jax.ShapeDtypeStruct((B, S, D), F32),
                   jax.ShapeDtypeStruct((B, H, S // T, 2, 2 * dqk, T), BF16),
                   jax.ShapeDtypeStruct((B, S, W), BF16),
                   jax.ShapeDtypeStruct((B, S // T, W, T), BF16)]
                  + [jax.ShapeDtypeStruct((B, S, W), F32)] * 2,
        scratch_shapes=[pltpu.VMEM((tm, D), F32)],
        compiler_params=pltpu.CompilerParams(
            dimension_semantics=("parallel", "parallel"), vmem_limit_bytes=VMEM_LIMIT),
        name="ffn_in_proj",
    )(x, ng1.reshape(1, D), sh1, sc1, g1, w1, w3, w2, ng2.reshape(1, D), sh2, sc2,
      w_in_b, w_qvt)


def _out_ffn_kernel(x_ref, o_ref, yr_ref, g2_ref, wout_ref, ng_ref, sh_ref, sc_ref, g_ref,
                    w1_ref, w3_ref, w2_ref, *rest, final_norm):
    if final_norm:
        fg_ref, out_ref, acc_ref = rest
    else:
        out_ref, acc_ref = rest
    mix = (jnp.dot(o_ref[0], wout_ref[0], preferred_element_type=F32)
           + jnp.dot(yr_ref[0], wout_ref[1], preferred_element_type=F32))
    x = x_ref[0] + g2_ref[0] * mix
    y = _swiglu_residual(x, _modulated(x, ng_ref, sh_ref, sc_ref),
                         g_ref, w1_ref, w3_ref, w2_ref, acc_ref)
    if final_norm:
        y = _rms(y) * fg_ref[...]
    out_ref[0] = y


def _out_ffn(x, o, yr, g2, w_out_c, ng, sh, sc, g, w1, w3, w2, final_g=None):
    B, S, D = x.shape
    tm = FFN_TM
    W = o.shape[-1]
    final_norm = final_g is not None
    vec = pl.BlockSpec((1, 1, D), lambda b, i: (b, 0, 0))
    row = _const_spec((1, D))
    full = pl.BlockSpec((1, tm, D), lambda b, i: (b, i, 0))
    half = pl.BlockSpec((1, tm, W), lambda b, i: (b, i, 0))
    in_specs = [full, half, half, vec, _const_spec(w_out_c.shape), row, vec, vec, vec,
                _const_spec(w1.shape), _const_spec(w3.shape), _const_spec(w2.shape)]
    args = [x, o, yr, g2, w_out_c, ng.reshape(1, D), sh, sc, g, w1, w3, w2]
    if final_norm:
        in_specs.append(row)
        args.append(final_g.reshape(1, D))
    return pl.pallas_call(
        functools.partial(_out_ffn_kernel, final_norm=final_norm),
        grid=(B, S // tm),
        in_specs=in_specs,
        out_specs=full,
        out_shape=jax.ShapeDtypeStruct((B, S, D), F32),
        scratch_shapes=[pltpu.VMEM((tm, D), F32)],
        compiler_params=pltpu.CompilerParams(
            dimension_semantics=("parallel", "parallel"), vmem_limit_bytes=VMEM_LIMIT),
        name="out_proj_ffn_final" if final_norm else "out_proj_ffn",
    )(*args)


def _bias_kernel(rb_ref, o_ref):
    mp = pl.program_id(0)
    T = o_ref.shape[-1]
    key = lax.broadcasted_iota(jnp.int32, (T, T), 0)
    qry = lax.broadcasted_iota(jnp.int32, (T, T), 1)
    max_exact = N_BUCKETS // 2
    last = rb_ref[N_BUCKETS - 1, mp]
    for which in range(2):
        rel = qry - key + (T if which == 0 else 0)
        n = jnp.maximum(rel, 0)
        nf = jnp.maximum(n, 1).astype(F32)
        large = max_exact + (jnp.log(nf / max_exact) / math.log(MAX_DISTANCE / max_exact)
                             * (N_BUCKETS - max_exact)).astype(jnp.int32)
        large = jnp.minimum(large, N_BUCKETS - 1)
        bucket = jnp.where(n < max_exact, n, large)
        val = jnp.zeros((T, T), F32)
        for bk in range(N_BUCKETS - 1):
            val = jnp.where(bucket == bk, (rb_ref[bk, mp] - last) * LOG2E, val)
        if which == 1:
            val = jnp.where(rel >= 0, val, NEG_BIG)
        o_ref[0, which] = val


def _bias_tiles(rel_bias, T):
    n_maps = rel_bias.shape[1]
    return pl.pallas_call(
        _bias_kernel,
        grid=(n_maps,),
        in_specs=[pl.BlockSpec(memory_space=pltpu.SMEM)],
        out_specs=pl.BlockSpec((1, 2, T, T), lambda m: (m, 0, 0, 0)),
        out_shape=jax.ShapeDtypeStruct((n_maps, 2, T, T), F32),
        name="rel_bias_tiles",
    )(rel_bias)


def _attn_kernel(q_ref, qn_ref, k_ref, vt_ref, bias_ref, lq1_ref, lk1_ref, lq2_ref, lk2_ref,
                 sg_ref, o_ref, m_ref, l_ref, acc_ref, s_ref, smax_ref, *, lambda_init):
    T = q_ref.shape[-1]
    qi = pl.program_id(2)
    q_refs = (q_ref, qn_ref)

    m_ref[...] = jnp.full_like(m_ref, NEG_BIG)
    l_ref[...] = jnp.zeros_like(l_ref)
    acc_ref[...] = jnp.zeros_like(acc_ref)

    strips = [slice(c * ATTN_STRIP, (c + 1) * ATTN_STRIP)
              for c in range(2 * T // ATTN_STRIP)]

    def n_keys(which, c):
        return strips[c].start % T + ATTN_STRIP if which == 1 else T

    def put_scores(buf, j, which, c, slot=0):
        nk = n_keys(which, c)
        start = j * T if isinstance(j, int) else pl.multiple_of(j * T, T)
        kj = k_ref[0, pl.ds(start, nk), :]
        mp, q0 = divmod(strips[c].start, T)
        qt = q_refs[slot][0, 0, 0, mp, :, q0:q0 + ATTN_STRIP]
        st = jnp.dot(kj, qt, preferred_element_type=F32)
        if which is not None:
            st = biased(st, which, mp, q0)
        s_ref[buf, c, :nk, :] = st
        smax_ref[buf, :, strips[c]] = jnp.max(st, axis=0, keepdims=True)

    def biased(st, which, mp, q0):
        rows = []
        for a in range(st.shape[0] // BIAS_T):
            cols = []
            for b in range(q0 // BIAS_T, (q0 + ATTN_STRIP) // BIAS_T):
                lo = (b - q0 // BIAS_T) * BIAS_T
                piece = st[a * BIAS_T:(a + 1) * BIAS_T, lo:lo + BIAS_T]
                dist = b - a + (T // BIAS_T if which == 0 else 0)
                if dist < 0:
                    piece = jnp.full_like(piece, NEG_BIG)
                elif dist <= 1:
                    piece = piece + bias_ref[0, mp, 1 - dist]
                cols.append(piece)
            rows.append(jnp.concatenate(cols, axis=1))
        return jnp.concatenate(rows, axis=0)

    def step(buf, j, which, nxt, hold=0):
        early = len(strips) - hold
        for c in range(ATTN_LOOKAHEAD):
            put_scores(*nxt[:3], c, nxt[3])
        for c, sl in enumerate(strips):
            if c + ATTN_LOOKAHEAD < early:
                put_scores(*nxt[:3], c + ATTN_LOOKAHEAD, nxt[3])
            nk = n_keys(which, c)
            m_old = m_ref[:, sl]
            m_new = jnp.maximum(m_old, smax_ref[buf, :, sl])
            alpha = jnp.exp2(m_old - m_new)
            l_new = alpha * l_ref[:, sl]
            acc_new = alpha * acc_ref[c]
            for k0 in range(0, nk, ATTN_KCHUNK):
                keys = slice(k0, k0 + ATTN_KCHUNK)
                p = jnp.exp2(s_ref[buf, c, keys, :] - m_new)
                l_new = l_new + jnp.sum(p, axis=0, keepdims=True)
                acc_new = acc_new + jnp.dot(vt_ref[0, j, :, keys], p.astype(BF16),
                                            preferred_element_type=F32)
            l_ref[:, sl] = l_new
            acc_ref[c] = acc_new
            m_ref[:, sl] = m_new
        for c in range(early, len(strips)):
            put_scores(*nxt[:3], c, nxt[3])

    def last_step(buf, next_which=None):
        step(buf, qi, 1, (0, 0, next_which, 1), hold=ATTN_HOLD)
        lam = (jnp.exp(jnp.sum(lq1_ref[...] * lk1_ref[...], axis=-1, keepdims=True))
               - jnp.exp(jnp.sum(lq2_ref[...] * lk2_ref[...], axis=-1, keepdims=True))
               + lambda_init)
        o_all = jnp.concatenate([acc_ref[c] for c in range(len(strips))],
                                axis=1) / l_ref[...]
        ot = o_all[:, :T] - lam * o_all[:, T:]
        ot = ot * lax.rsqrt(jnp.mean(ot * ot, axis=0, keepdims=True) + NORM_EPS)
        o_ref[0] = (ot.T * sg_ref[...] * (1.0 - lambda_init)).astype(o_ref.dtype)

    @pl.when(qi == 0)
    def _():
        for c in range(len(strips)):
            put_scores(2, 0, 1, c)
        last_step(2, next_which=0)

    @pl.when(qi >= 1)
    def _():
        n_far = qi - 1

        def pair(i, carry):
            j = 2 * i
            step(0, j, None, (1, j + 1, None, 0))
            step(1, j + 1, None, (0, j + 2, None, 0))
            return carry

        lax.fori_loop(0, jnp.maximum((n_far - 1) // 2, 0), pair, 0)

        @pl.when(n_far == 0)
        def _():
            step(0, qi - 1, 0, (1, qi, 1, 0))
            last_step(1)

        @pl.when((n_far >= 2) & (n_far % 2 == 0))
        def _():
            step(0, qi - 3, None, (1, qi - 2, None, 0))
            step(1, qi - 2, None, (0, qi - 1, 0, 0))
            step(0, qi - 1, 0, (1, qi, 1, 0))
            last_step(1)

        @pl.when(n_far % 2 == 1)
        def _():
            step(0, qi - 2, None, (1, qi - 1, 0, 0))
            step(1, qi - 1, 0, (2, qi, 1, 0))
            last_step(2)


def _attention(q, k, vt, bias, lq1, lk1, lq2, lk2, subln_g, lambda_init):
    B, S, _ = k.shape
    T = ATTN_T
    H, dv, dqk = ATTN_HEADS, ATTN_V_DIM, ATTN_QK_DIM
    bias = bias.reshape(H, 2, 2, BIAS_T, BIAS_T)
    small = lambda n: _const_spec((1, n))
    return pl.pallas_call(
        functools.partial(_attn_kernel, lambda_init=lambda_init),
        grid=(B, H, S // T),
        in_specs=[pl.BlockSpec((1, 1, 1, 2, 2 * dqk, T), lambda b, h, i: (b, h, i, 0, 0, 0)),
                  pl.BlockSpec((1, 1, 1, 2, 2 * dqk, T),
                               lambda b, h, i: (b, h, jnp.minimum(i + 1, S // T - 1), 0, 0, 0)),
                  pl.BlockSpec((1, S, 2 * dqk), lambda b, h, i: (b, 0, h)),
                  pl.BlockSpec((1, S // T, dv, T), lambda b, h, i: (b, 0, h, 0)),
                  pl.BlockSpec((1, 2, 2, BIAS_T, BIAS_T), lambda b, h, i: (h, 0, 0, 0, 0)),
                  small(dqk), small(dqk), small(dqk), small(dqk), small(dv)],
        out_specs=pl.BlockSpec((1, T, dv), lambda b, h, i: (b, i, h)),
        out_shape=jax.ShapeDtypeStruct((B, S, H * dv), BF16),
        scratch_shapes=[pltpu.VMEM((1, 2 * T), F32),
                        pltpu.VMEM((1, 2 * T), F32),
                        pltpu.VMEM((2 * T // ATTN_STRIP, dv, ATTN_STRIP), F32),
                        pltpu.VMEM((3, 2 * T // ATTN_STRIP, T, ATTN_STRIP), F32),
                        pltpu.VMEM((3, 1, 2 * T), F32)],
        compiler_params=pltpu.CompilerParams(
            dimension_semantics=("parallel", "parallel", "arbitrary"),
            vmem_limit_bytes=VMEM_LIMIT),
        name="diff_attn",
    )(q, q, k, vt, bias, lq1.reshape(1, dqk), lk1.reshape(1, dqk), lq2.reshape(1, dqk),
      lk2.reshape(1, dqk), subln_g.reshape(1, dv))


def _rglru_kernel(xr_ref, gr_ref, cw_ref, cb_ref, wg_ref, ba_ref, bi_ref, L_ref,
                  y_ref, xpad_ref, h_ref, a_buf, b_buf):
    Tc, C = xr_ref.shape[1], xr_ref.shape[2]
    G = Tc // SUBLANES

    @pl.when(pl.program_id(1) == 0)
    def _():
        xpad_ref[:SUBLANES] = jnp.zeros((SUBLANES, C), F32)
        h_ref[...] = jnp.zeros_like(h_ref)

    x = xr_ref[0]
    xpad_ref[SUBLANES:] = x
    xc = x * cw_ref[CONV_WIDTH - 1:CONV_WIDTH, :] + cb_ref[...]
    for k in range(1, CONV_WIDTH):
        xc = xc + (xpad_ref[pl.ds(SUBLANES - k, Tc), :]
                   * cw_ref[CONV_WIDTH - 1 - k:CONV_WIDTH - k, :])
    xpad_ref[:SUBLANES] = x[Tc - SUBLANES:]

    gates = jnp.dot(xc.astype(BF16), wg_ref[...], preferred_element_type=F32)
    r = jax.nn.sigmoid(gates[:, :C] + ba_ref[...])
    ig = jax.nn.sigmoid(gates[:, C:] + bi_ref[...])
    L = L_ref[...]
    log_sig = jnp.minimum(L, 0.0) - jnp.log1p(jnp.exp(-jnp.abs(L)))
    log_a = r * (RG_LRU_C * log_sig)
    a = jnp.exp(log_a)
    v = -jnp.tanh(log_a) * (a * a + 1.0)
    root = jnp.where(v > 0.0, v * lax.rsqrt(v), 0.0)
    u = root * (ig * xc)

    a3 = a.reshape(G, SUBLANES, C)
    u3 = u.reshape(G, SUBLANES, C)
    sub3 = lax.broadcasted_iota(jnp.int32, (G, SUBLANES, C), 1)
    for k in (1, 2, 4):
        a_sh = pltpu.roll(a3, k, 1)
        u_sh = pltpu.roll(u3, k, 1)
        ok = sub3 >= k
        u3 = jnp.where(ok, a3 * u_sh + u3, u3)
        a3 = jnp.where(ok, a3 * a_sh, a3)
    a_buf[...] = a3
    b_buf[...] = u3

    def body(g, hprev):
        hg = a_buf[g] * hprev + b_buf[g]
        b_buf[g] = hg
        return hg[SUBLANES - 1:SUBLANES, :]

    h_last = lax.fori_loop(0, G, body, h_ref[...], unroll=8)
    h_ref[...] = h_last

    h = b_buf[...].reshape(Tc, C)
    gr = gr_ref[0]
    c0 = math.sqrt(2.0 / math.pi)
    t = jnp.tanh(gr * (c0 + (c0 * 0.044715) * (gr * gr)))
    y_ref[0] = (h * (gr * (0.5 + 0.5 * t))).astype(y_ref.dtype)


def _rglru(xr, gr, conv_w, conv_b, w_gates, b_a, b_i, lru_L):
    B, S, C = xr.shape
    Tc = RNN_TC
    tok = pl.BlockSpec((1, Tc, C), lambda b, t: (b, t, 0))
    row = lambda: _const_spec((1, C))
    return pl.pallas_call(
        _rglru_kernel,
        grid=(B, S // Tc),
        in_specs=[tok, tok, _const_spec((CONV_WIDTH, C)), row(),
                  _const_spec(w_gates.shape), row(), row(), row()],
        out_specs=tok,
        out_shape=jax.ShapeDtypeStruct((B, S, C), BF16),
        scratch_shapes=[pltpu.VMEM((Tc + SUBLANES, C), F32),
                        pltpu.VMEM((1, C), F32),
                        pltpu.VMEM((Tc // SUBLANES, SUBLANES, C), F32),
                        pltpu.VMEM((Tc // SUBLANES, SUBLANES, C), F32)],
        compiler_params=pltpu.CompilerParams(
            dimension_semantics=("parallel", "arbitrary"), vmem_limit_bytes=VMEM_LIMIT),
        name="rg_lru",
    )(xr, gr, conv_w, conv_b.reshape(1, C), w_gates, b_a.reshape(1, C),
      b_i.reshape(1, C), lru_L.reshape(1, C))


def _to_bf16(w, l):
    return w[l].astype(BF16)


def _block_diag(w):
    n, bw, _ = w.shape
    eye = jnp.eye(n, dtype=w.dtype)
    return (eye[:, None, :, None] * w[:, :, None, :]).reshape(n * bw, n * bw)


def kernel(x, c, rel_bias, ada_w, ada_b, norm_g, ffn1_w1, ffn1_w3, ffn1_w2, w_in, lam_q1, lam_k1, lam_q2, lam_k2, subln_g, conv_w, conv_b, gate_a_w, gate_a_b, gate_i_w, gate_i_b, lru_L, w_out, ffn2_w1, ffn2_w3, ffn2_w2, final_g):
    B, S, D = x.shape
    depth = ada_w.shape[0]
    c_pad = jnp.zeros((SUBLANES, D), F32).at[:B].set(c)
    bias = _bias_tiles(rel_bias, BIAS_T)
    for l in range(depth):
        mod = _modulation(c_pad, ada_w[l], ada_b[l])[:, :B]
        sh1, sc1, g1, sh2, sc2, g2, sh3, sc3, g3 = [m.reshape(B, 1, D) for m in mod]

        w_in_b = _to_bf16(w_in, l)
        w_qvt = jnp.concatenate([w_in_b[:, :ATTN_WIDTH],
                                 w_in_b[:, 2 * ATTN_WIDTH:3 * ATTN_WIDTH]], axis=1).T
        x, q, k, vt, xr, gr = _ffn_in(
            x, norm_g[l, 0], sh1, sc1, g1,
            _to_bf16(ffn1_w1, l), _to_bf16(ffn1_w3, l), _to_bf16(ffn1_w2, l),
            norm_g[l, 1], sh2, sc2, w_in_b, w_qvt)

        lambda_init = 0.8 - 0.6 * math.exp(-0.3 * l)
        o = _attention(q, k, vt, bias, lam_q1[l], lam_k1[l], lam_q2[l], lam_k2[l],
                       subln_g[l], lambda_init)

        w_gates = jnp.concatenate(
            [_block_diag(gate_a_w[l]), _block_diag(gate_i_w[l])], axis=1).astype(BF16)
        yr = _rglru(xr, gr, conv_w[l], conv_b[l], w_gates, gate_a_b[l], gate_i_b[l],
                    lru_L[l])

        w_out_c = _to_bf16(w_out, l).reshape(2, ATTN_WIDTH, D)
        last = l == depth - 1
        x = _out_ffn(x, o, yr, g2, w_out_c, norm_g[l, 2], sh3, sc3, g3,
                     _to_bf16(ffn2_w1, l), _to_bf16(ffn2_w3, l), _to_bf16(ffn2_w2, l),
                     final_g=final_g if last else None)
    return x
```

```python
import functools
import math

import jax
import jax.numpy as jnp
from jax import lax
from jax.experimental import pallas as pl
from jax.experimental.pallas import tpu as pltpu

F32 = jnp.float32
BF16 = jnp.bfloat16

ATTN_HEADS = 4
ATTN_QK_DIM = 64
ATTN_V_DIM = 2 * ATTN_QK_DIM
ATTN_WIDTH = ATTN_HEADS * ATTN_V_DIM
RNN_BLOCKS = 8
CONV_WIDTH = 4
RG_LRU_C = 8.0
N_BUCKETS = 32
MAX_DISTANCE = 128
NORM_EPS = 1e-6
N_MOD = 9

SUBLANES = 8
VMEM_LIMIT = 56 * 1024 * 1024

FFN_TM = 512
FFN_FC = 256
ATTN_T = 1024
ATTN_STRIP = 256
ATTN_KCHUNK = 256
ATTN_LOOKAHEAD = 1
ATTN_HOLD = 3
BIAS_T = 128
RNN_TC = 1024
NEG_BIG = -1e30
LOG2E = math.log2(math.e)


def _rms(x):
    return x * lax.rsqrt(jnp.mean(x * x, axis=-1, keepdims=True) + NORM_EPS)


def _const_spec(shape):
    nd = len(shape)
    return pl.BlockSpec(shape, lambda *_: (0,) * nd, pipeline_mode=pl.Buffered(1))


def _mod_kernel(c_ref, w_ref, b_ref, o_ref):
    c = c_ref[...]
    ca = c * jax.nn.sigmoid(c)
    o_ref[0] = jnp.dot(ca, w_ref[...], preferred_element_type=F32,
                       precision=lax.Precision.HIGHEST) + b_ref[0]


def _modulation(c_pad, ada_w, ada_b):
    rows, d = c_pad.shape
    n = ada_w.shape[1] // d
    return pl.pallas_call(
        _mod_kernel,
        grid=(n,),
        in_specs=[pl.BlockSpec((rows, d), lambda j: (0, 0)),
                  pl.BlockSpec((d, d), lambda j: (0, j)),
                  pl.BlockSpec((1, 1, d), lambda j: (j, 0, 0))],
        out_specs=pl.BlockSpec((1, rows, d), lambda j: (j, 0, 0)),
        out_shape=jax.ShapeDtypeStruct((n, rows, d), F32),
        compiler_params=pltpu.CompilerParams(vmem_limit_bytes=VMEM_LIMIT),
        name="adaln_mod",
    )(c_pad, ada_w, ada_b.reshape(n, 1, d))


def _modulated(x, ng_ref, sh_ref, sc_ref):
    return (_rms(x) * ng_ref[...] * (1.0 + sc_ref[0]) + sh_ref[0]).astype(BF16)


def _swiglu_residual(x, hb, g_ref, w1_ref, w3_ref, w2_ref, acc_ref):
    n_chunks = w1_ref.shape[1] // FFN_FC
    cols = lambda ci: slice(ci * FFN_FC, (ci + 1) * FFN_FC)
    up = lambda ci: (jnp.dot(hb, w1_ref[:, cols(ci)], preferred_element_type=F32),
                     jnp.dot(hb, w3_ref[:, cols(ci)], preferred_element_type=F32))
    a, b = up(0)
    for ci in range(n_chunks):
        if ci + 1 < n_chunks:
            a_next, b_next = up(ci + 1)
        u = (a * jax.nn.sigmoid(a) * b).astype(BF16)
        down = jnp.dot(u, w2_ref[cols(ci), :], preferred_element_type=F32)
        if ci == 0:
            acc_ref[...] = down
        else:
            acc_ref[...] += down
        a, b = a_next, b_next
    return x + 0.5 * g_ref[0] * acc_ref[...]


def _ffn_in_kernel(x_ref, ng1_ref, sh1_ref, sc1_ref, g1_ref, w1_ref, w3_ref, w2_ref,
                   ng2_ref, sh2_ref, sc2_ref, win_ref, wqvt_ref,
                   x1_ref, q_ref, k_ref, vt_ref, xr_ref, gr_ref, acc_ref):
    x = x_ref[0]
    y = _swiglu_residual(x, _modulated(x, ng1_ref, sh1_ref, sc1_ref),
                         g1_ref, w1_ref, w3_ref, w2_ref, acc_ref)
    x1_ref[0] = y
    hb = _modulated(y, ng2_ref, sh2_ref, sc2_ref)
    scale = ATTN_QK_DIM ** -0.5 * LOG2E
    W = k_ref.shape[-1]
    proj = lambda n: jnp.dot(hb, win_ref[:, n * W:(n + 1) * W], preferred_element_type=F32)
    k_ref[0] = proj(1).astype(BF16)
    qvt = lax.dot_general(wqvt_ref[...], hb, (((1,), (1,)), ((), ())),
                          preferred_element_type=F32)
    vt = qvt[W:].astype(BF16)
    for ci in range(vt_ref.shape[1]):
        vt_ref[0, ci] = vt[:, ci * ATTN_KCHUNK:(ci + 1) * ATTN_KCHUNK]
    qt = (qvt[:W] * scale).astype(BF16)
    dqk = ATTN_QK_DIM
    zeros = jnp.zeros((dqk, qt.shape[1]), BF16)
    for h in range(ATTN_HEADS):
        top = qt[2 * dqk * h:2 * dqk * h + dqk]
        bot = qt[2 * dqk * h + dqk:2 * dqk * (h + 1)]
        q_ref[0, h, 0, 0] = jnp.concatenate([top, zeros], axis=0)
        q_ref[0, h, 0, 1] = jnp.concatenate([zeros, bot], axis=0)
    xr_ref[0] = proj(3)
    gr_ref[0] = proj(4)


def _ffn_in(x, ng1, sh1, sc1, g1, w1, w3, w2, ng2, sh2, sc2, w_in_b, w_qvt):
    B, S, D = x.shape
    tm, T = FFN_TM, ATTN_T
    W = w_qvt.shape[0] // 2
    H, dqk = ATTN_HEADS, ATTN_QK_DIM
    per_t = T // tm
    vec = pl.BlockSpec((1, 1, D), lambda b, i: (b, 0, 0))
    row = _const_spec((1, D))
    full = pl.BlockSpec((1, tm, D), lambda b, i: (b, i, 0))
    tok = pl.BlockSpec((1, tm, W), lambda b, i: (b, i, 0))
    vt_spec = pl.BlockSpec((1, tm // ATTN_KCHUNK, W, ATTN_KCHUNK), lambda b, i: (b, i, 0, 0))
    q_spec = pl.BlockSpec((1, H, 1, 2, 2 * dqk, tm),
                          lambda b, i: (b, 0, i // per_t, 0, 0, i % per_t))
    return pl.pallas_call(
        _ffn_in_kernel,
        grid=(B, S // tm),
        in_specs=[full, row, vec, vec, vec,
                  _const_spec(w1.shape), _const_spec(w3.shape), _const_spec(w2.shape),
                  row, vec, vec, _const_spec(w_in_b.shape), _const_spec(w_qvt.shape)],
        out_specs=[full, q_spec, tok, vt_spec, tok, tok],
        out_shape=[jax.ShapeDtypeStruct((B, S, D), F32),
                   jax.ShapeDtypeStruct((B, H, S // T, 2, 2 * dqk, T), BF16),
                   jax.ShapeDtypeStruct((B, S, W), BF16),
                   jax.ShapeDtypeStruct((B, S // ATTN_KCHUNK, W, ATTN_KCHUNK), BF16)]
                  + [jax.ShapeDtypeStruct((B, S, W), F32)] * 2,
        scratch_shapes=[pltpu.VMEM((tm, D), F32)],
        compiler_params=pltpu.CompilerParams(
            dimension_semantics=("parallel", "parallel"), vmem_limit_bytes=VMEM_LIMIT),
        name="ffn_in_proj",
    )(x, ng1.reshape(1, D), sh1, sc1, g1, w1, w3, w2, ng2.reshape(1, D), sh2, sc2,
      w_in_b, w_qvt)


def _out_ffn_kernel(x_ref, o_ref, yr_ref, g2_ref, wout_ref, ng_ref, sh_ref, sc_ref, g_ref,
                    w1_ref, w3_ref, w2_ref, *rest, final_norm):
    if final_norm:
        fg_ref, out_ref, acc_ref = rest
    else:
        out_ref, acc_ref = rest
    mix = (jnp.dot(o_ref[0], wout_ref[0], preferred_element_type=F32)
           + jnp.dot(yr_ref[0], wout_ref[1], preferred_element_type=F32))
    x = x_ref[0] + g2_ref[0] * mix
    y = _swiglu_residual(x, _modulated(x, ng_ref, sh_ref, sc_ref),
                         g_ref, w1_ref, w3_ref, w2_ref, acc_ref)
    if final_norm:
        y = _rms(y) * fg_ref[...]
    out_ref[0] = y


def _out_ffn(x, o, yr, g2, w_out_c, ng, sh, sc, g, w1, w3, w2, final_g=None):
    B, S, D = x.shape
    tm = FFN_TM
    W = o.shape[-1]
    final_norm = final_g is not None
    vec = pl.BlockSpec((1, 1, D), lambda b, i: (b, 0, 0))
    row = _const_spec((1, D))
    full = pl.BlockSpec((1, tm, D), lambda b, i: (b, i, 0))
    half = pl.BlockSpec((1, tm, W), lambda b, i: (b, i, 0))
    in_specs = [full, half, half, vec, _const_spec(w_out_c.shape), row, vec, vec, vec,
                _const_spec(w1.shape), _const_spec(w3.shape), _const_spec(w2.shape)]
    args = [x, o, yr, g2, w_out_c, ng.reshape(1, D), sh, sc, g, w1, w3, w2]
    if final_norm:
        in_specs.append(row)
        args.append(final_g.reshape(1, D))
    return pl.pallas_call(
        functools.partial(_out_ffn_kernel, final_norm=final_norm),
        grid=(B, S // tm),
        in_specs=in_specs,
        out_specs=full,
        out_shape=jax.ShapeDtypeStruct((B, S, D), F32),
        scratch_shapes=[pltpu.VMEM((tm, D), F32)],
        compiler_params=pltpu.CompilerParams(
            dimension_semantics=("parallel", "parallel"), vmem_limit_bytes=VMEM_LIMIT),
        name="out_proj_ffn_final" if final_norm else "out_proj_ffn",
    )(*args)


def _bias_kernel(rb_ref, o_ref):
    mp = pl.program_id(0)
    T = o_ref.shape[-1]
    key = lax.broadcasted_iota(jnp.int32, (T, T), 0)
    qry = lax.broadcasted_iota(jnp.int32, (T, T), 1)
    max_exact = N_BUCKETS // 2
    last = rb_ref[N_BUCKETS - 1, mp]
    for which in range(2):
        rel = qry - key + (T if which == 0 else 0)
        n = jnp.maximum(rel, 0)
        nf = jnp.maximum(n, 1).astype(F32)
        large = max_exact + (jnp.log(nf / max_exact) / math.log(MAX_DISTANCE / max_exact)
                             * (N_BUCKETS - max_exact)).astype(jnp.int32)
        large = jnp.minimum(large, N_BUCKETS - 1)
        bucket = jnp.where(n < max_exact, n, large)
        val = jnp.zeros((T, T), F32)
        for bk in range(N_BUCKETS - 1):
            val = jnp.where(bucket == bk, (rb_ref[bk, mp] - last) * LOG2E, val)
        if which == 1:
            val = jnp.where(rel >= 0, val, NEG_BIG)
        o_ref[0, which] = val


def _bias_tiles(rel_bias, T):
    n_maps = rel_bias.shape[1]
    return pl.pallas_call(
        _bias_kernel,
        grid=(n_maps,),
        in_specs=[pl.BlockSpec(memory_space=pltpu.SMEM)],
        out_specs=pl.BlockSpec((1, 2, T, T), lambda m: (m, 0, 0, 0)),
        out_shape=jax.ShapeDtypeStruct((n_maps, 2, T, T), F32),
        name="rel_bias_tiles",
    )(rel_bias)


def _attn_kernel(q_ref, qn_ref, k_ref, vt_ref, bias_ref, lq1_ref, lk1_ref, lq2_ref, lk2_ref,
                 sg_ref, o_ref, m_ref, l_ref, acc_ref, s_ref, smax_ref, *, lambda_init):
    T = q_ref.shape[-1]
    qi = pl.program_id(2)
    q_refs = (q_ref, qn_ref)

    m_ref[...] = jnp.full_like(m_ref, NEG_BIG)
    l_ref[...] = jnp.zeros_like(l_ref)
    acc_ref[...] = jnp.zeros_like(acc_ref)

    strips = [slice(c * ATTN_STRIP, (c + 1) * ATTN_STRIP)
              for c in range(2 * T // ATTN_STRIP)]

    def n_keys(which, c):
        return strips[c].start % T + ATTN_STRIP if which == 1 else T

    def put_scores(buf, j, which, c, slot=0):
        nk = n_keys(which, c)
        start = j * T if isinstance(j, int) else pl.multiple_of(j * T, T)
        kj = k_ref[0, pl.ds(start, nk), :]
        mp, q0 = divmod(strips[c].start, T)
        qt = q_refs[slot][0, 0, 0, mp, :, q0:q0 + ATTN_STRIP]
        st = jnp.dot(kj, qt, preferred_element_type=F32)
        if which is not None:
            st = biased(st, which, mp, q0)
        s_ref[buf, c, :nk, :] = st
        smax_ref[buf, :, strips[c]] = jnp.max(st, axis=0, keepdims=True)

    def biased(st, which, mp, q0):
        rows = []
        for a in range(st.shape[0] // BIAS_T):
            cols = []
            for b in range(q0 // BIAS_T, (q0 + ATTN_STRIP) // BIAS_T):
                lo = (b - q0 // BIAS_T) * BIAS_T
                piece = st[a * BIAS_T:(a + 1) * BIAS_T, lo:lo + BIAS_T]
                dist = b - a + (T // BIAS_T if which == 0 else 0)
                if dist < 0:
                    piece = jnp.full_like(piece, NEG_BIG)
                elif dist <= 1:
                    piece = piece + bias_ref[0, mp, 1 - dist]
                cols.append(piece)
            rows.append(jnp.concatenate(cols, axis=1))
        return jnp.concatenate(rows, axis=0)

    def step(buf, j, which, nxt, hold=0):
        early = len(strips) - hold
        for c in range(ATTN_LOOKAHEAD):
            put_scores(*nxt[:3], c, nxt[3])
        for c, sl in enumerate(strips):
            if c + ATTN_LOOKAHEAD < early:
                put_scores(*nxt[:3], c + ATTN_LOOKAHEAD, nxt[3])
            nk = n_keys(which, c)
            m_old = m_ref[:, sl]
            m_new = jnp.maximum(m_old, smax_ref[buf, :, sl])
            alpha = jnp.exp2(m_old - m_new)
            l_new = alpha * l_ref[:, sl]
            acc_new = alpha * acc_ref[c]
            for k0 in range(0, nk, ATTN_KCHUNK):
                keys = slice(k0, k0 + ATTN_KCHUNK)
                p = jnp.exp2(s_ref[buf, c, keys, :] - m_new)
                l_new = l_new + jnp.sum(p, axis=0, keepdims=True)
                vt = vt_ref[0, j * (T // ATTN_KCHUNK) + k0 // ATTN_KCHUNK]
                acc_new = acc_new + jnp.dot(vt, p.astype(BF16), preferred_element_type=F32)
            l_ref[:, sl] = l_new
            acc_ref[c] = acc_new
            m_ref[:, sl] = m_new
        for c in range(early, len(strips)):
            put_scores(*nxt[:3], c, nxt[3])

    def last_step(buf, next_which=None):
        step(buf, qi, 1, (0, 0, next_which, 1), hold=ATTN_HOLD)
        lam = (jnp.exp(jnp.sum(lq1_ref[...] * lk1_ref[...], axis=-1, keepdims=True))
               - jnp.exp(jnp.sum(lq2_ref[...] * lk2_ref[...], axis=-1, keepdims=True))
               + lambda_init)
        o_all = jnp.concatenate([acc_ref[c] for c in range(len(strips))],
                                axis=1) / l_ref[...]
        ot = o_all[:, :T] - lam * o_all[:, T:]
        ot = ot * lax.rsqrt(jnp.mean(ot * ot, axis=0, keepdims=True) + NORM_EPS)
        o_ref[0] = (ot.T * sg_ref[...] * (1.0 - lambda_init)).astype(o_ref.dtype)

    @pl.when(qi == 0)
    def _():
        for c in range(len(strips)):
            put_scores(2, 0, 1, c)
        last_step(2, next_which=0)

    @pl.when(qi >= 1)
    def _():
        n_far = qi - 1

        def pair(i, carry):
            j = 2 * i
            step(0, j, None, (1, j + 1, None, 0))
            step(1, j + 1, None, (0, j + 2, None, 0))
            return carry

        lax.fori_loop(0, jnp.maximum((n_far - 1) // 2, 0), pair, 0)

        @pl.when(n_far == 0)
        def _():
            step(0, qi - 1, 0, (1, qi, 1, 0))
            last_step(1)

        @pl.when((n_far >= 2) & (n_far % 2 == 0))
        def _():
            step(0, qi - 3, None, (1, qi - 2, None, 0))
            step(1, qi - 2, None, (0, qi - 1, 0, 0))
            step(0, qi - 1, 0, (1, qi, 1, 0))
            last_step(1)

        @pl.when(n_far % 2 == 1)
        def _():
            step(0, qi - 2, None, (1, qi - 1, 0, 0))
            step(1, qi - 1, 0, (2, qi, 1, 0))
            last_step(2)


def _attention(q, k, vt, bias, lq1, lk1, lq2, lk2, subln_g, lambda_init):
    B, S, _ = k.shape
    T = ATTN_T
    H, dv, dqk = ATTN_HEADS, ATTN_V_DIM, ATTN_QK_DIM
    bias = bias.reshape(H, 2, 2, BIAS_T, BIAS_T)
    small = lambda n: _const_spec((1, n))
    return pl.pallas_call(
        functools.partial(_attn_kernel, lambda_init=lambda_init),
        grid=(B, H, S // T),
        in_specs=[pl.BlockSpec((1, 1, 1, 2, 2 * dqk, T), lambda b, h, i: (b, h, i, 0, 0, 0)),
                  pl.BlockSpec((1, 1, 1, 2, 2 * dqk, T),
                               lambda b, h, i: (b, h, jnp.minimum(i + 1, S // T - 1), 0, 0, 0)),
                  pl.BlockSpec((1, S, 2 * dqk), lambda b, h, i: (b, 0, h)),
                  pl.BlockSpec((1, S // ATTN_KCHUNK, dv, ATTN_KCHUNK),
                               lambda b, h, i: (b, 0, h, 0)),
                  pl.BlockSpec((1, 2, 2, BIAS_T, BIAS_T), lambda b, h, i: (h, 0, 0, 0, 0)),
                  small(dqk), small(dqk), small(dqk), small(dqk), small(dv)],
        out_specs=pl.BlockSpec((1, T, dv), lambda b, h, i: (b, i, h)),
        out_shape=jax.ShapeDtypeStruct((B, S, H * dv), BF16),
        scratch_shapes=[pltpu.VMEM((1, 2 * T), F32),
                        pltpu.VMEM((1, 2 * T), F32),
                        pltpu.VMEM((2 * T // ATTN_STRIP, dv, ATTN_STRIP), F32),
                        pltpu.VMEM((3, 2 * T // ATTN_STRIP, T, ATTN_STRIP), F32),
                        pltpu.VMEM((3, 1, 2 * T), F32)],
        compiler_params=pltpu.CompilerParams(
            dimension_semantics=("parallel", "parallel", "arbitrary"),
            vmem_limit_bytes=VMEM_LIMIT),
        name="diff_attn",
    )(q, q, k, vt, bias, lq1.reshape(1, dqk), lk1.reshape(1, dqk), lq2.reshape(1, dqk),
      lk2.reshape(1, dqk), subln_g.reshape(1, dv))


def _rglru_kernel(xr_ref, gr_ref, cw_ref, cb_ref, wg_ref, ba_ref, bi_ref, L_ref,
                  y_ref, xpad_ref, h_ref, a_buf, b_buf):
    Tc, C = xr_ref.shape[1], xr_ref.shape[2]
    G = Tc // SUBLANES

    @pl.when(pl.program_id(1) == 0)
    def _():
        xpad_ref[:SUBLANES] = jnp.zeros((SUBLANES, C), F32)
        h_ref[...] = jnp.zeros_like(h_ref)

    x = xr_ref[0]
    xpad_ref[SUBLANES:] = x
    xc = x * cw_ref[CONV_WIDTH - 1:CONV_WIDTH, :] + cb_ref[...]
    for k in range(1, CONV_WIDTH):
        xc = xc + (xpad_ref[pl.ds(SUBLANES - k, Tc), :]
                   * cw_ref[CONV_WIDTH - 1 - k:CONV_WIDTH - k, :])
    xpad_ref[:SUBLANES] = x[Tc - SUBLANES:]

    gates = jnp.dot(xc.astype(BF16), wg_ref[...], preferred_element_type=F32)
    r = jax.nn.sigmoid(gates[:, :C] + ba_ref[...])
    ig = jax.nn.sigmoid(gates[:, C:] + bi_ref[...])
    L = L_ref[...]
    log_sig = jnp.minimum(L, 0.0) - jnp.log1p(jnp.exp(-jnp.abs(L)))
    log_a = r * (RG_LRU_C * log_sig)
    a = jnp.exp(log_a)
    v = -jnp.tanh(log_a) * (a * a + 1.0)
    root = jnp.where(v > 0.0, v * lax.rsqrt(v), 0.0)
    u = root * (ig * xc)

    a3 = a.reshape(G, SUBLANES, C)
    u3 = u.reshape(G, SUBLANES, C)
    sub3 = lax.broadcasted_iota(jnp.int32, (G, SUBLANES, C), 1)
    for k in (1, 2, 4):
        a_sh = pltpu.roll(a3, k, 1)
        u_sh = pltpu.roll(u3, k, 1)
        ok = sub3 >= k
        u3 = jnp.where(ok, a3 * u_sh + u3, u3)
        a3 = jnp.where(ok, a3 * a_sh, a3)
    a_buf[...] = a3
    b_buf[...] = u3

    def body(g, hprev):
        hg = a_buf[g] * hprev + b_buf[g]
        b_buf[g] = hg
        return hg[SUBLANES - 1:SUBLANES, :]

    h_last = lax.fori_loop(0, G, body, h_ref[...], unroll=8)
    h_ref[...] = h_last

    h = b_buf[...].reshape(Tc, C)
    gr = gr_ref[0]
    c0 = math.sqrt(2.0 / math.pi)
    t = jnp.tanh(gr * (c0 + (c0 * 0.044715) * (gr * gr)))
    y_ref[0] = (h * (gr * (0.5 + 0.5 * t))).astype(y_ref.dtype)


def _rglru(xr, gr, conv_w, conv_b, w_gates, b_a, b_i, lru_L):
    B, S, C = xr.shape
    Tc = RNN_TC
    tok = pl.BlockSpec((1, Tc, C), lambda b, t: (b, t, 0))
    row = lambda: _const_spec((1, C))
    return pl.pallas_call(
        _rglru_kernel,
        grid=(B, S // Tc),
        in_specs=[tok, tok, _const_spec((CONV_WIDTH, C)), row(),
                  _const_spec(w_gates.shape), row(), row(), row()],
        out_specs=tok,
        out_shape=jax.ShapeDtypeStruct((B, S, C), BF16),
        scratch_shapes=[pltpu.VMEM((Tc + SUBLANES, C), F32),
                        pltpu.VMEM((1, C), F32),
                        pltpu.VMEM((Tc // SUBLANES, SUBLANES, C), F32),
                        pltpu.VMEM((Tc // SUBLANES, SUBLANES, C), F32)],
        compiler_params=pltpu.CompilerParams(
            dimension_semantics=("parallel", "arbitrary"), vmem_limit_bytes=VMEM_LIMIT),
        name="rg_lru",
    )(xr, gr, conv_w, conv_b.reshape(1, C), w_gates, b_a.reshape(1, C),
      b_i.reshape(1, C), lru_L.reshape(1, C))


def _to_bf16(w, l):
    return w[l].astype(BF16)


def _block_diag(w):
    n, bw, _ = w.shape
    eye = jnp.eye(n, dtype=w.dtype)
    return (eye[:, None, :, None] * w[:, :, None, :]).reshape(n * bw, n * bw)


def kernel(x, c, rel_bias, ada_w, ada_b, norm_g, ffn1_w1, ffn1_w3, ffn1_w2, w_in, lam_q1, lam_k1, lam_q2, lam_k2, subln_g, conv_w, conv_b, gate_a_w, gate_a_b, gate_i_w, gate_i_b, lru_L, w_out, ffn2_w1, ffn2_w3, ffn2_w2, final_g):
    B, S, D = x.shape
    depth = ada_w.shape[0]
    c_pad = jnp.zeros((SUBLANES, D), F32).at[:B].set(c)
    bias = _bias_tiles(rel_bias, BIAS_T)
    for l in range(depth):
        mod = _modulation(c_pad, ada_w[l], ada_b[l])[:, :B]
        sh1, sc1, g1, sh2, sc2, g2, sh3, sc3, g3 = [m.reshape(B, 1, D) for m in mod]

        w_in_b = _to_bf16(w_in, l)
        w_qvt = jnp.concatenate([w_in_b[:, :ATTN_WIDTH],
                                 w_in_b[:, 2 * ATTN_WIDTH:3 * ATTN_WIDTH]], axis=1).T
        x, q, k, vt, xr, gr = _ffn_in(
            x, norm_g[l, 0], sh1, sc1, g1,
            _to_bf16(ffn1_w1, l), _to_bf16(ffn1_w3, l), _to_bf16(ffn1_w2, l),
            norm_g[l, 1], sh2, sc2, w_in_b, w_qvt)

        lambda_init = 0.8 - 0.6 * math.exp(-0.3 * l)
        o = _attention(q, k, vt, bias, lam_q1[l], lam_k1[l], lam_q2[l], lam_k2[l],
                       subln_g[l], lambda_init)

        w_gates = jnp.concatenate(
            [_block_diag(gate_a_w[l]), _block_diag(gate_i_w[l])], axis=1).astype(BF16)
        yr = _rglru(xr, gr, conv_w[l], conv_b[l], w_gates, gate_a_b[l], gate_i_b[l],
                    lru_L[l])

        w_out_c = _to_bf16(w_out, l).reshape(2, ATTN_WIDTH, D)
        last = l == depth - 1
        x = _out_ffn(x, o, yr, g2, w_out_c, norm_g[l, 2], sh3, sc3, g3,
                     _to_bf16(ffn2_w1, l), _to_bf16(ffn2_w3, l), _to_bf16(ffn2_w2, l),
                     final_g=final_g if last else None)
    return x
```

```python
import functools
import math

import jax
import jax.numpy as jnp
from jax import lax
from jax.experimental import pallas as pl
from jax.experimental.pallas import tpu as pltpu

F32 = jnp.float32
BF16 = jnp.bfloat16

ATTN_HEADS = 4
ATTN_QK_DIM = 64
ATTN_V_DIM = 2 * ATTN_QK_DIM
ATTN_WIDTH = ATTN_HEADS * ATTN_V_DIM
CONV_WIDTH = 4
RG_LRU_C = 8.0
N_BUCKETS = 32
MAX_DISTANCE = 128
NORM_EPS = 1e-6

SUBLANES = 8
VMEM_LIMIT = 56 * 1024 * 1024

FFN_TM = 512
FFN_FC = 256
ATTN_T = 1024
ATTN_STRIP = 256
ATTN_KCHUNK = 256
ATTN_LOOKAHEAD = 1
ATTN_HOLD = 3
BIAS_T = 128
RNN_TC = 1024
_MAX_EXACT = N_BUCKETS // 2
_LAST_BUCKET_START = math.ceil(_MAX_EXACT * (MAX_DISTANCE / _MAX_EXACT) ** (
    (N_BUCKETS - 1 - _MAX_EXACT) / (N_BUCKETS - _MAX_EXACT)))
assert _LAST_BUCKET_START <= BIAS_T <= ATTN_STRIP and ATTN_T % ATTN_STRIP == 0
NEG_BIG = -1e30
LOG2E = math.log2(math.e)


def _rms(x):
    return x * lax.rsqrt(jnp.mean(x * x, axis=-1, keepdims=True) + NORM_EPS)


def _const_spec(shape):
    nd = len(shape)
    return pl.BlockSpec(shape, lambda *_: (0,) * nd, pipeline_mode=pl.Buffered(1))


def _mod_kernel(c_ref, w_ref, b_ref, o_ref):
    c = c_ref[...]
    ca = c * jax.nn.sigmoid(c)
    o_ref[0] = jnp.dot(ca, w_ref[...], preferred_element_type=F32,
                       precision=lax.Precision.HIGHEST) + b_ref[0]


def _modulation(c_pad, ada_w, ada_b):
    rows, d = c_pad.shape
    n = ada_w.shape[1] // d
    return pl.pallas_call(
        _mod_kernel,
        grid=(n,),
        in_specs=[pl.BlockSpec((rows, d), lambda j: (0, 0)),
                  pl.BlockSpec((d, d), lambda j: (0, j)),
                  pl.BlockSpec((1, 1, d), lambda j: (j, 0, 0))],
        out_specs=pl.BlockSpec((1, rows, d), lambda j: (j, 0, 0)),
        out_shape=jax.ShapeDtypeStruct((n, rows, d), F32),
        compiler_params=pltpu.CompilerParams(vmem_limit_bytes=VMEM_LIMIT),
        name="adaln_mod",
    )(c_pad, ada_w, ada_b.reshape(n, 1, d))


def _modulated(x, ng_ref, sh_ref, sc_ref):
    return (_rms(x) * ng_ref[...] * (1.0 + sc_ref[0]) + sh_ref[0]).astype(BF16)


def _swiglu_residual(x, hb, g_ref, w1_ref, w3_ref, w2_ref, acc_ref):
    n_chunks = w1_ref.shape[1] // FFN_FC
    cols = lambda ci: slice(ci * FFN_FC, (ci + 1) * FFN_FC)
    up = lambda ci: (jnp.dot(hb, w1_ref[:, cols(ci)], preferred_element_type=F32),
                     jnp.dot(hb, w3_ref[:, cols(ci)], preferred_element_type=F32))
    a, b = up(0)
    for ci in range(n_chunks):
        if ci + 1 < n_chunks:
            a_next, b_next = up(ci + 1)
        u = (a * jax.nn.sigmoid(a) * b).astype(BF16)
        down = jnp.dot(u, w2_ref[cols(ci), :], preferred_element_type=F32)
        if ci == 0:
            acc_ref[...] = down
        else:
            acc_ref[...] += down
        a, b = a_next, b_next
    return x + 0.5 * g_ref[0] * acc_ref[...]


def _ffn_in_kernel(x_ref, ng1_ref, sh1_ref, sc1_ref, g1_ref, w1_ref, w3_ref, w2_ref,
                   ng2_ref, sh2_ref, sc2_ref, win_ref, wqvt_ref,
                   x1_ref, q_ref, k_ref, vt_ref, xr_ref, gr_ref, acc_ref):
    x = x_ref[0]
    y = _swiglu_residual(x, _modulated(x, ng1_ref, sh1_ref, sc1_ref),
                         g1_ref, w1_ref, w3_ref, w2_ref, acc_ref)
    x1_ref[0] = y
    hb = _modulated(y, ng2_ref, sh2_ref, sc2_ref)
    scale = ATTN_QK_DIM ** -0.5 * LOG2E
    W = k_ref.shape[-1]
    proj = lambda n: jnp.dot(hb, win_ref[:, n * W:(n + 1) * W], preferred_element_type=F32)
    k_ref[0] = proj(1).astype(BF16)
    qvt = lax.dot_general(wqvt_ref[...], hb, (((1,), (1,)), ((), ())),
                          preferred_element_type=F32)
    vt_ref[0, 0] = qvt[W:].astype(BF16)
    qt = (qvt[:W] * scale).astype(BF16)
    dqk = ATTN_QK_DIM
    zeros = jnp.zeros((dqk, qt.shape[1]), BF16)
    for h in range(ATTN_HEADS):
        top = qt[2 * dqk * h:2 * dqk * h + dqk]
        bot = qt[2 * dqk * h + dqk:2 * dqk * (h + 1)]
        q_ref[0, h, 0, 0] = jnp.concatenate([top, zeros], axis=0)
        q_ref[0, h, 0, 1] = jnp.concatenate([zeros, bot], axis=0)
    xr_ref[0] = proj(3)
    gr_ref[0] = proj(4)


def _ffn_in(x, ng1, sh1, sc1, g1, w1, w3, w2, ng2, sh2, sc2, w_in_b, w_qvt):
    B, S, D = x.shape
    tm, T = FFN_TM, ATTN_T
    W = w_qvt.shape[0] // 2
    H, dqk = ATTN_HEADS, ATTN_QK_DIM
    per_t = T // tm
    vec = pl.BlockSpec((1, 1, D), lambda b, i: (b, 0, 0))
    row = _const_spec((1, D))
    full = pl.BlockSpec((1, tm, D), lambda b, i: (b, i, 0))
    tok = pl.BlockSpec((1, tm, W), lambda b, i: (b, i, 0))
    vt_spec = pl.BlockSpec((1, 1, W, tm), lambda b, i: (b, i // per_t, 0, i % per_t))
    q_spec = pl.BlockSpec((1, H, 1, 2, 2 * dqk, tm),
                          lambda b, i: (b, 0, i // per_t, 0, 0, i % per_t))
    return pl.pallas_call(
        _ffn_in_kernel,
        grid=(B, S // tm),
        in_specs=[full, row, vec, vec, vec,
                  _const_spec(w1.shape), _const_spec(w3.shape), _const_spec(w2.shape),
                  row, vec, vec, _const_spec(w_in_b.shape), _const_spec(w_qvt.shape)],
        out_specs=[full, q_spec, tok, vt_spec, tok, tok],
        out_shape=[jax.ShapeDtypeStruct((B, S, D), F32),
                   jax.ShapeDtypeStruct((B, H, S // T, 2, 2 * dqk, T), BF16),
                   jax.ShapeDtypeStruct((B, S, W), BF16),
                   jax.ShapeDtypeStruct((B, S // T, W, T), BF16)]
                  + [jax.ShapeDtypeStruct((B, S, W), F32)] * 2,
        scratch_shapes=[pltpu.VMEM((tm, D), F32)],
        compiler_params=pltpu.CompilerParams(
            dimension_semantics=("parallel", "parallel"), vmem_limit_bytes=VMEM_LIMIT),
        name="ffn_in_proj",
    )(x, ng1.reshape(1, D), sh1, sc1, g1, w1, w3, w2, ng2.reshape(1, D), sh2, sc2,
      w_in_b, w_qvt)


def _out_ffn_kernel(x_ref, o_ref, yr_ref, g2_ref, wout_ref, ng_ref, sh_ref, sc_ref, g_ref,
                    w1_ref, w3_ref, w2_ref, *rest, final_norm):
    if final_norm:
        fg_ref, out_ref, acc_ref = rest
    else:
        out_ref, acc_ref = rest
    mix = (jnp.dot(o_ref[0], wout_ref[0], preferred_element_type=F32)
           + jnp.dot(yr_ref[0], wout_ref[1], preferred_element_type=F32))
    x = x_ref[0] + g2_ref[0] * mix
    y = _swiglu_residual(x, _modulated(x, ng_ref, sh_ref, sc_ref),
                         g_ref, w1_ref, w3_ref, w2_ref, acc_ref)
    if final_norm:
        y = _rms(y) * fg_ref[...]
    out_ref[0] = y


def _out_ffn(x, o, yr, g2, w_out_c, ng, sh, sc, g, w1, w3, w2, final_g=None):
    B, S, D = x.shape
    tm = FFN_TM
    W = o.shape[-1]
    final_norm = final_g is not None
    vec = pl.BlockSpec((1, 1, D), lambda b, i: (b, 0, 0))
    row = _const_spec((1, D))
    full = pl.BlockSpec((1, tm, D), lambda b, i: (b, i, 0))
    half = pl.BlockSpec((1, tm, W), lambda b, i: (b, i, 0))
    in_specs = [full, half, half, vec, _const_spec(w_out_c.shape), row, vec, vec, vec,
                _const_spec(w1.shape), _const_spec(w3.shape), _const_spec(w2.shape)]
    args = [x, o, yr, g2, w_out_c, ng.reshape(1, D), sh, sc, g, w1, w3, w2]
    if final_norm:
        in_specs.append(row)
        args.append(final_g.reshape(1, D))
    return pl.pallas_call(
        functools.partial(_out_ffn_kernel, final_norm=final_norm),
        grid=(B, S // tm),
        in_specs=in_specs,
        out_specs=full,
        out_shape=jax.ShapeDtypeStruct((B, S, D), F32),
        scratch_shapes=[pltpu.VMEM((tm, D), F32)],
        compiler_params=pltpu.CompilerParams(
            dimension_semantics=("parallel", "parallel"), vmem_limit_bytes=VMEM_LIMIT),
        name="out_proj_ffn_final" if final_norm else "out_proj_ffn",
    )(*args)


def _bias_kernel(rb_ref, o_ref):
    mp = pl.program_id(0)
    T = o_ref.shape[-1]
    key = lax.broadcasted_iota(jnp.int32, (T, T), 0)
    qry = lax.broadcasted_iota(jnp.int32, (T, T), 1)
    max_exact = N_BUCKETS // 2
    last = rb_ref[N_BUCKETS - 1, mp]
    for which in range(2):
        rel = qry - key + (T if which == 0 else 0)
        n = jnp.maximum(rel, 0)
        nf = jnp.maximum(n, 1).astype(F32)
        large = max_exact + (jnp.log(nf / max_exact) / math.log(MAX_DISTANCE / max_exact)
                             * (N_BUCKETS - max_exact)).astype(jnp.int32)
        large = jnp.minimum(large, N_BUCKETS - 1)
        bucket = jnp.where(n < max_exact, n, large)
        val = jnp.zeros((T, T), F32)
        for bk in range(N_BUCKETS - 1):
            val = jnp.where(bucket == bk, (rb_ref[bk, mp] - last) * LOG2E, val)
        if which == 1:
            val = jnp.where(rel >= 0, val, NEG_BIG)
        o_ref[0, which] = val


def _bias_tiles(rel_bias, T):
    n_maps = rel_bias.shape[1]
    return pl.pallas_call(
        _bias_kernel,
        grid=(n_maps,),
        in_specs=[pl.BlockSpec(memory_space=pltpu.SMEM)],
        out_specs=pl.BlockSpec((1, 2, T, T), lambda m: (m, 0, 0, 0)),
        out_shape=jax.ShapeDtypeStruct((n_maps, 2, T, T), F32),
        name="rel_bias_tiles",
    )(rel_bias)


def _attn_kernel(q_ref, qn_ref, k_ref, vt_ref, bias_ref, lq1_ref, lk1_ref, lq2_ref, lk2_ref,
                 sg_ref, o_ref, m_ref, l_ref, acc_ref, s_ref, smax_ref, *, lambda_init):
    T = q_ref.shape[-1]
    qi = pl.program_id(2)
    q_refs = (q_ref, qn_ref)

    m_ref[...] = jnp.full_like(m_ref, NEG_BIG)
    l_ref[...] = jnp.zeros_like(l_ref)
    acc_ref[...] = jnp.zeros_like(acc_ref)

    strips = [slice(c * ATTN_STRIP, (c + 1) * ATTN_STRIP)
              for c in range(2 * T // ATTN_STRIP)]

    def n_keys(which, c):
        return strips[c].start % T + ATTN_STRIP if which == 1 else T

    def put_scores(buf, j, which, c, slot=0):
        nk = n_keys(which, c)
        start = j * T if isinstance(j, int) else pl.multiple_of(j * T, T)
        kj = k_ref[0, pl.ds(start, nk), :]
        mp, q0 = divmod(strips[c].start, T)
        qt = q_refs[slot][0, 0, 0, mp, :, q0:q0 + ATTN_STRIP]
        st = jnp.dot(kj, qt, preferred_element_type=F32)
        if which is not None:
            st = biased(st, which, mp, q0)
        s_ref[buf, c, :nk, :] = st
        smax_ref[buf, :, strips[c]] = jnp.max(st, axis=0, keepdims=True)

    def biased(st, which, mp, q0):
        rows = []
        for a in range(st.shape[0] // BIAS_T):
            cols = []
            for b in range(q0 // BIAS_T, (q0 + ATTN_STRIP) // BIAS_T):
                lo = (b - q0 // BIAS_T) * BIAS_T
                piece = st[a * BIAS_T:(a + 1) * BIAS_T, lo:lo + BIAS_T]
                dist = b - a + (T // BIAS_T if which == 0 else 0)
                if dist < 0:
                    piece = jnp.full_like(piece, NEG_BIG)
                elif dist <= 1:
                    piece = piece + bias_ref[0, mp, 1 - dist]
                cols.append(piece)
            rows.append(jnp.concatenate(cols, axis=1))
        return jnp.concatenate(rows, axis=0)

    def step(buf, j, which, nxt, hold=0):
        early = len(strips) - hold
        for c in range(ATTN_LOOKAHEAD):
            put_scores(*nxt[:3], c, nxt[3])
        for c, sl in enumerate(strips):
            if c + ATTN_LOOKAHEAD < early:
                put_scores(*nxt[:3], c + ATTN_LOOKAHEAD, nxt[3])
            nk = n_keys(which, c)
            m_old = m_ref[:, sl]
            m_new = jnp.maximum(m_old, smax_ref[buf, :, sl])
            alpha = jnp.exp2(m_old - m_new)
            l_new = alpha * l_ref[:, sl]
            acc_new = alpha * acc_ref[c]
            for k0 in range(0, nk, ATTN_KCHUNK):
                keys = slice(k0, k0 + ATTN_KCHUNK)
                p = jnp.exp2(s_ref[buf, c, keys, :] - m_new)
                l_new = l_new + jnp.sum(p, axis=0, keepdims=True)
                acc_new = acc_new + jnp.dot(vt_ref[0, j, :, keys], p.astype(BF16),
                                            preferred_element_type=F32)
            l_ref[:, sl] = l_new
            acc_ref[c] = acc_new
            m_ref[:, sl] = m_new
        for c in range(early, len(strips)):
            put_scores(*nxt[:3], c, nxt[3])

    def last_step(buf, next_which=None):
        step(buf, qi, 1, (0, 0, next_which, 1), hold=ATTN_HOLD)
        lam = (jnp.exp(jnp.sum(lq1_ref[...] * lk1_ref[...], axis=-1, keepdims=True))
               - jnp.exp(jnp.sum(lq2_ref[...] * lk2_ref[...], axis=-1, keepdims=True))
               + lambda_init)
        o_all = jnp.concatenate([acc_ref[c] for c in range(len(strips))],
                                axis=1) / l_ref[...]
        ot = o_all[:, :T] - lam * o_all[:, T:]
        ot = ot * lax.rsqrt(jnp.mean(ot * ot, axis=0, keepdims=True) + NORM_EPS)
        o_ref[0] = (ot.T * sg_ref[...] * (1.0 - lambda_init)).astype(o_ref.dtype)

    @pl.when(qi == 0)
    def _():
        for c in range(len(strips)):
            put_scores(2, 0, 1, c)
        last_step(2, next_which=0)

    @pl.when(qi >= 1)
    def _():
        n_far = qi - 1

        def pair(i, carry):
            j = 2 * i
            step(0, j, None, (1, j + 1, None, 0))
            step(1, j + 1, None, (0, j + 2, None, 0))
            return carry

        lax.fori_loop(0, jnp.maximum((n_far - 1) // 2, 0), pair, 0)

        @pl.when(n_far == 0)
        def _():
            step(0, qi - 1, 0, (1, qi, 1, 0))
            last_step(1)

        @pl.when((n_far >= 2) & (n_far % 2 == 0))
        def _():
            step(0, qi - 3, None, (1, qi - 2, None, 0))
            step(1, qi - 2, None, (0, qi - 1, 0, 0))
            step(0, qi - 1, 0, (1, qi, 1, 0))
            last_step(1)

        @pl.when(n_far % 2 == 1)
        def _():
            step(0, qi - 2, None, (1, qi - 1, 0, 0))
            step(1, qi - 1, 0, (2, qi, 1, 0))
            last_step(2)


def _attention(q, k, vt, bias, lq1, lk1, lq2, lk2, subln_g, lambda_init):
    B, S, _ = k.shape
    T = ATTN_T
    H, dv, dqk = ATTN_HEADS, ATTN_V_DIM, ATTN_QK_DIM
    bias = bias.reshape(H, 2, 2, BIAS_T, BIAS_T)
    small = lambda n: _const_spec((1, n))
    return pl.pallas_call(
        functools.partial(_attn_kernel, lambda_init=lambda_init),
        grid=(B, H, S // T),
        in_specs=[pl.BlockSpec((1, 1, 1, 2, 2 * dqk, T), lambda b, h, i: (b, h, i, 0, 0, 0)),
                  pl.BlockSpec((1, 1, 1, 2, 2 * dqk, T),
                               lambda b, h, i: (b, h, jnp.minimum(i + 1, S // T - 1), 0, 0, 0)),
                  pl.BlockSpec((1, S, 2 * dqk), lambda b, h, i: (b, 0, h)),
                  pl.BlockSpec((1, S // T, dv, T), lambda b, h, i: (b, 0, h, 0)),
                  pl.BlockSpec((1, 2, 2, BIAS_T, BIAS_T), lambda b, h, i: (h, 0, 0, 0, 0)),
                  small(dqk), small(dqk), small(dqk), small(dqk), small(dv)],
        out_specs=pl.BlockSpec((1, T, dv), lambda b, h, i: (b, i, h)),
        out_shape=jax.ShapeDtypeStruct((B, S, H * dv), BF16),
        scratch_shapes=[pltpu.VMEM((1, 2 * T), F32),
                        pltpu.VMEM((1, 2 * T), F32),
                        pltpu.VMEM((2 * T // ATTN_STRIP, dv, ATTN_STRIP), F32),
                        pltpu.VMEM((3, 2 * T // ATTN_STRIP, T, ATTN_STRIP), F32),
                        pltpu.VMEM((3, 1, 2 * T), F32)],
        compiler_params=pltpu.CompilerParams(
            dimension_semantics=("parallel", "parallel", "arbitrary"),
            vmem_limit_bytes=VMEM_LIMIT),
        name="diff_attn",
    )(q, q, k, vt, bias, lq1.reshape(1, dqk), lk1.reshape(1, dqk), lq2.reshape(1, dqk),
      lk2.reshape(1, dqk), subln_g.reshape(1, dv))


def _rglru_kernel(xr_ref, gr_ref, cw_ref, cb_ref, wg_ref, ba_ref, bi_ref, L_ref,
                  y_ref, xpad_ref, h_ref, a_buf, b_buf):
    Tc, C = xr_ref.shape[1], xr_ref.shape[2]
    G = Tc // SUBLANES

    @pl.when(pl.program_id(1) == 0)
    def _():
        xpad_ref[:SUBLANES] = jnp.zeros((SUBLANES, C), F32)
        h_ref[...] = jnp.zeros_like(h_ref)

    x = xr_ref[0]
    xpad_ref[SUBLANES:] = x
    xc = x * cw_ref[CONV_WIDTH - 1:CONV_WIDTH, :] + cb_ref[...]
    for k in range(1, CONV_WIDTH):
        xc = xc + (xpad_ref[pl.ds(SUBLANES - k, Tc), :]
                   * cw_ref[CONV_WIDTH - 1 - k:CONV_WIDTH - k, :])
    xpad_ref[:SUBLANES] = x[Tc - SUBLANES:]

    gates = jnp.dot(xc.astype(BF16), wg_ref[...], preferred_element_type=F32)
    r = jax.nn.sigmoid(gates[:, :C] + ba_ref[...])
    ig = jax.nn.sigmoid(gates[:, C:] + bi_ref[...])
    L = L_ref[...]
    log_sig = jnp.minimum(L, 0.0) - jnp.log1p(jnp.exp(-jnp.abs(L)))
    log_a = r * (RG_LRU_C * log_sig)
    a = jnp.exp(log_a)
    v = -jnp.tanh(log_a) * (a * a + 1.0)
    root = jnp.where(v > 0.0, v * lax.rsqrt(v), 0.0)
    u = root * (ig * xc)

    a3 = a.reshape(G, SUBLANES, C)
    u3 = u.reshape(G, SUBLANES, C)
    sub3 = lax.broadcasted_iota(jnp.int32, (G, SUBLANES, C), 1)
    for k in (1, 2, 4):
        a_sh = pltpu.roll(a3, k, 1)
        u_sh = pltpu.roll(u3, k, 1)
        ok = sub3 >= k
        u3 = jnp.where(ok, a3 * u_sh + u3, u3)
        a3 = jnp.where(ok, a3 * a_sh, a3)
    a_buf[...] = a3
    b_buf[...] = u3

    def body(g, hprev):
        hg = a_buf[g] * hprev + b_buf[g]
        b_buf[g] = hg
        return hg[SUBLANES - 1:SUBLANES, :]

    h_last = lax.fori_loop(0, G, body, h_ref[...], unroll=8)
    h_ref[...] = h_last

    h = b_buf[...].reshape(Tc, C)
    gr = gr_ref[0]
    c0 = math.sqrt(2.0 / math.pi)
    t = jnp.tanh(gr * (c0 + (c0 * 0.044715) * (gr * gr)))
    y_ref[0] = (h * (gr * (0.5 + 0.5 * t))).astype(y_ref.dtype)


def _rglru(xr, gr, conv_w, conv_b, w_gates, b_a, b_i, lru_L):
    B, S, C = xr.shape
    Tc = RNN_TC
    tok = pl.BlockSpec((1, Tc, C), lambda b, t: (b, t, 0))
    row = lambda: _const_spec((1, C))
    return pl.pallas_call(
        _rglru_kernel,
        grid=(B, S // Tc),
        in_specs=[tok, tok, _const_spec((CONV_WIDTH, C)), row(),
                  _const_spec(w_gates.shape), row(), row(), row()],
        out_specs=tok,
        out_shape=jax.ShapeDtypeStruct((B, S, C), BF16),
        scratch_shapes=[pltpu.VMEM((Tc + SUBLANES, C), F32),
                        pltpu.VMEM((1, C), F32),
                        pltpu.VMEM((Tc // SUBLANES, SUBLANES, C), F32),
                        pltpu.VMEM((Tc // SUBLANES, SUBLANES, C), F32)],
        compiler_params=pltpu.CompilerParams(
            dimension_semantics=("parallel", "arbitrary"), vmem_limit_bytes=VMEM_LIMIT),
        name="rg_lru",
    )(xr, gr, conv_w, conv_b.reshape(1, C), w_gates, b_a.reshape(1, C),
      b_i.reshape(1, C), lru_L.reshape(1, C))


def _to_bf16(w, l):
    return w[l].astype(BF16)


def _block_diag(w):
    n, bw, _ = w.shape
    eye = jnp.eye(n, dtype=w.dtype)
    return (eye[:, None, :, None] * w[:, :, None, :]).reshape(n * bw, n * bw)


def kernel(x, c, rel_bias, ada_w, ada_b, norm_g, ffn1_w1, ffn1_w3, ffn1_w2, w_in, lam_q1, lam_k1, lam_q2, lam_k2, subln_g, conv_w, conv_b, gate_a_w, gate_a_b, gate_i_w, gate_i_b, lru_L, w_out, ffn2_w1, ffn2_w3, ffn2_w2, final_g):
    B, S, D = x.shape
    depth = ada_w.shape[0]
    c_pad = jnp.zeros((SUBLANES, D), F32).at[:B].set(c)
    bias = _bias_tiles(rel_bias, BIAS_T)
    for l in range(depth):
        mod = _modulation(c_pad, ada_w[l], ada_b[l])[:, :B]
        sh1, sc1, g1, sh2, sc2, g2, sh3, sc3, g3 = [m.reshape(B, 1, D) for m in mod]

        w_in_b = _to_bf16(w_in, l)
        w_qvt = jnp.concatenate([w_in_b[:, :ATTN_WIDTH],
                                 w_in_b[:, 2 * ATTN_WIDTH:3 * ATTN_WIDTH]], axis=1).T
        x, q, k, vt, xr, gr = _ffn_in(
            x, norm_g[l, 0], sh1, sc1, g1,
            _to_bf16(ffn1_w1, l), _to_bf16(ffn1_w3, l), _to_bf16(ffn1_w2, l),
            norm_g[l, 1], sh2, sc2, w_in_b, w_qvt)

        lambda_init = 0.8 - 0.6 * math.exp(-0.3 * l)
        o = _attention(q, k, vt, bias, lam_q1[l], lam_k1[l], lam_q2[l], lam_k2[l],
                       subln_g[l], lambda_init)

        w_gates = jnp.concatenate(
            [_block_diag(gate_a_w[l]), _block_diag(gate_i_w[l])], axis=1).astype(BF16)
        yr = _rglru(xr, gr, conv_w[l], conv_b[l], w_gates, gate_a_b[l], gate_i_b[l],
                    lru_L[l])

        w_out_c = _to_bf16(w_out, l).reshape(2, ATTN_WIDTH, D)
        last = l == depth - 1
        x = _out_ffn(x, o, yr, g2, w_out_c, norm_g[l, 2], sh3, sc3, g3,
                     _to_bf16(ffn2_w1, l), _to_bf16(ffn2_w3, l), _to_bf16(ffn2_w2, l),
                     final_g=final_g if last else None)
    return x
```

```python
import functools
import math

import jax
import jax.numpy as jnp
from jax import lax
from jax.experimental import pallas as pl
from jax.experimental.pallas import tpu as pltpu

F32 = jnp.float32
BF16 = jnp.bfloat16

ATTN_HEADS = 4
ATTN_QK_DIM = 64
ATTN_V_DIM = 2 * ATTN_QK_DIM
ATTN_WIDTH = ATTN_HEADS * ATTN_V_DIM
CONV_WIDTH = 4
RG_LRU_C = 8.0
N_BUCKETS = 32
MAX_DISTANCE = 128
NORM_EPS = 1e-6

SUBLANES = 8
CAST_ROW_TILE = 16
VMEM_LIMIT = 56 * 1024 * 1024

FFN_TM = 512
FFN_FC = 256
ATTN_T = 1024
ATTN_STRIP = 256
ATTN_KCHUNK = 256
ATTN_LOOKAHEAD = 1
ATTN_HOLD = 3
BIAS_T = 128
RNN_TC = 1024
_MAX_EXACT = N_BUCKETS // 2
_LAST_BUCKET_START = math.ceil(_MAX_EXACT * (MAX_DISTANCE / _MAX_EXACT) ** (
    (N_BUCKETS - 1 - _MAX_EXACT) / (N_BUCKETS - _MAX_EXACT)))
assert _LAST_BUCKET_START <= BIAS_T <= ATTN_STRIP and ATTN_T % ATTN_STRIP == 0
NEG_BIG = -1e30
LOG2E = math.log2(math.e)


def _rms(x):
    return x * lax.rsqrt(jnp.mean(x * x, axis=-1, keepdims=True) + NORM_EPS)


def _const_spec(shape):
    nd = len(shape)
    return pl.BlockSpec(shape, lambda *_: (0,) * nd, pipeline_mode=pl.Buffered(1))


def _mod_kernel(c_ref, w_ref, b_ref, o_ref):
    c = c_ref[...]
    ca = c * jax.nn.sigmoid(c)
    o_ref[0] = jnp.dot(ca, w_ref[...], preferred_element_type=F32,
                       precision=lax.Precision.HIGHEST) + b_ref[0]


def _modulation(c_pad, ada_w, ada_b):
    rows, d = c_pad.shape
    n = ada_w.shape[1] // d
    return pl.pallas_call(
        _mod_kernel,
        grid=(n,),
        in_specs=[pl.BlockSpec((rows, d), lambda j: (0, 0)),
                  pl.BlockSpec((d, d), lambda j: (0, j)),
                  pl.BlockSpec((1, 1, d), lambda j: (j, 0, 0))],
        out_specs=pl.BlockSpec((1, rows, d), lambda j: (j, 0, 0)),
        out_shape=jax.ShapeDtypeStruct((n, rows, d), F32),
        compiler_params=pltpu.CompilerParams(vmem_limit_bytes=VMEM_LIMIT),
        name="adaln_mod",
    )(c_pad, ada_w, ada_b.reshape(n, 1, d))


def _modulated(x, ng_ref, sh_ref, sc_ref):
    return (_rms(x) * ng_ref[...] * (1.0 + sc_ref[0]) + sh_ref[0]).astype(BF16)


def _swiglu_residual(x, hb, g_ref, w1_ref, w3_ref, w2_ref, acc_ref):
    n_chunks = w1_ref.shape[1] // FFN_FC
    cols = lambda ci: slice(ci * FFN_FC, (ci + 1) * FFN_FC)
    up = lambda ci: (jnp.dot(hb, w1_ref[:, cols(ci)], preferred_element_type=F32),
                     jnp.dot(hb, w3_ref[:, cols(ci)], preferred_element_type=F32))
    a, b = up(0)
    for ci in range(n_chunks):
        if ci + 1 < n_chunks:
            a_next, b_next = up(ci + 1)
        u = (a * jax.nn.sigmoid(a) * b).astype(BF16)
        down = jnp.dot(u, w2_ref[cols(ci), :], preferred_element_type=F32)
        if ci == 0:
            acc_ref[...] = down
        else:
            acc_ref[...] += down
        a, b = a_next, b_next
    return x + 0.5 * g_ref[0] * acc_ref[...]


def _ffn_in_kernel(*refs, n_cast):
    (x_ref, ng1_ref, sh1_ref, sc1_ref, g1_ref, w1_ref, w3_ref, w2_ref,
     ng2_ref, sh2_ref, sc2_ref, win_ref, wqvt_ref) = refs[:13]
    cast_in = refs[13:13 + n_cast]
    x1_ref, q_ref, k_ref, vt_ref, xr_ref, gr_ref = refs[13 + n_cast:19 + n_cast]
    cast_out = refs[19 + n_cast:19 + 2 * n_cast]
    acc_ref = refs[-1]
    for src, dst in zip(cast_in, cast_out):
        dst[...] = src[...].astype(BF16)
    x = x_ref[0]
    y = _swiglu_residual(x, _modulated(x, ng1_ref, sh1_ref, sc1_ref),
                         g1_ref, w1_ref, w3_ref, w2_ref, acc_ref)
    x1_ref[0] = y
    hb = _modulated(y, ng2_ref, sh2_ref, sc2_ref)
    scale = ATTN_QK_DIM ** -0.5 * LOG2E
    W = k_ref.shape[-1]
    proj = lambda n: jnp.dot(hb, win_ref[:, n * W:(n + 1) * W], preferred_element_type=F32)
    k_ref[0] = proj(1).astype(BF16)
    qvt = lax.dot_general(wqvt_ref[...], hb, (((1,), (1,)), ((), ())),
                          preferred_element_type=F32)
    vt_ref[0, 0] = qvt[W:].astype(BF16)
    qt = (qvt[:W] * scale).astype(BF16)
    dqk = ATTN_QK_DIM
    zeros = jnp.zeros((dqk, qt.shape[1]), BF16)
    for h in range(ATTN_HEADS):
        top = qt[2 * dqk * h:2 * dqk * h + dqk]
        bot = qt[2 * dqk * h + dqk:2 * dqk * (h + 1)]
        q_ref[0, h, 0, 0] = jnp.concatenate([top, zeros], axis=0)
        q_ref[0, h, 0, 1] = jnp.concatenate([zeros, bot], axis=0)
    xr_ref[0] = proj(3)
    gr_ref[0] = proj(4)


def _ffn_in(x, ng1, sh1, sc1, g1, w1, w3, w2, ng2, sh2, sc2, w_in_b, w_qvt, to_cast):
    B, S, D = x.shape
    tm, T = FFN_TM, ATTN_T
    n_steps = B * (S // tm)

    def cast_spec(w):
        rows = CAST_ROW_TILE
        while w.shape[0] % rows or w.shape[0] // rows > n_steps:
            rows += CAST_ROW_TILE
        last = w.shape[0] // rows - 1
        return pl.BlockSpec((rows, w.shape[1]),
                            lambda b, i: (jnp.minimum(b * (S // tm) + i, last), 0))

    cast_specs = [cast_spec(w) for w in to_cast]
    W = w_qvt.shape[0] // 2
    H, dqk = ATTN_HEADS, ATTN_QK_DIM
    per_t = T // tm
    vec = pl.BlockSpec((1, 1, D), lambda b, i: (b, 0, 0))
    row = _const_spec((1, D))
    full = pl.BlockSpec((1, tm, D), lambda b, i: (b, i, 0))
    tok = pl.BlockSpec((1, tm, W), lambda b, i: (b, i, 0))
    vt_spec = pl.BlockSpec((1, 1, W, tm), lambda b, i: (b, i // per_t, 0, i % per_t))
    q_spec = pl.BlockSpec((1, H, 1, 2, 2 * dqk, tm),
                          lambda b, i: (b, 0, i // per_t, 0, 0, i % per_t))
    outs = pl.pallas_call(
        functools.partial(_ffn_in_kernel, n_cast=len(to_cast)),
        grid=(B, S // tm),
        in_specs=[full, row, vec, vec, vec,
                  _const_spec(w1.shape), _const_spec(w3.shape), _const_spec(w2.shape),
                  row, vec, vec, _const_spec(w_in_b.shape), _const_spec(w_qvt.shape)]
                 + cast_specs,
        out_specs=[full, q_spec, tok, vt_spec, tok, tok] + cast_specs,
        out_shape=[jax.ShapeDtypeStruct((B, S, D), F32),
                   jax.ShapeDtypeStruct((B, H, S // T, 2, 2 * dqk, T), BF16),
                   jax.ShapeDtypeStruct((B, S, W), BF16),
                   jax.ShapeDtypeStruct((B, S // T, W, T), BF16)]
                  + [jax.ShapeDtypeStruct((B, S, W), F32)] * 2
                  + [jax.ShapeDtypeStruct(w.shape, BF16) for w in to_cast],
        scratch_shapes=[pltpu.VMEM((tm, D), F32)],
        compiler_params=pltpu.CompilerParams(
            dimension_semantics=("parallel", "parallel"), vmem_limit_bytes=VMEM_LIMIT),
        name="ffn_in_proj",
    )(x, ng1.reshape(1, D), sh1, sc1, g1, w1, w3, w2, ng2.reshape(1, D), sh2, sc2,
      w_in_b, w_qvt, *to_cast)
    return outs[:6], outs[6:]


def _out_ffn_kernel(x_ref, o_ref, yr_ref, g2_ref, wout_ref, ng_ref, sh_ref, sc_ref, g_ref,
                    w1_ref, w3_ref, w2_ref, *rest, final_norm):
    if final_norm:
        fg_ref, out_ref, acc_ref = rest
    else:
        out_ref, acc_ref = rest
    mix = (jnp.dot(o_ref[0], wout_ref[0], preferred_element_type=F32)
           + jnp.dot(yr_ref[0], wout_ref[1], preferred_element_type=F32))
    x = x_ref[0] + g2_ref[0] * mix
    y = _swiglu_residual(x, _modulated(x, ng_ref, sh_ref, sc_ref),
                         g_ref, w1_ref, w3_ref, w2_ref, acc_ref)
    if final_norm:
        y = _rms(y) * fg_ref[...]
    out_ref[0] = y


def _out_ffn(x, o, yr, g2, w_out_c, ng, sh, sc, g, w1, w3, w2, final_g=None):
    B, S, D = x.shape
    tm = FFN_TM
    W = o.shape[-1]
    final_norm = final_g is not None
    vec = pl.BlockSpec((1, 1, D), lambda b, i: (b, 0, 0))
    row = _const_spec((1, D))
    full = pl.BlockSpec((1, tm, D), lambda b, i: (b, i, 0))
    half = pl.BlockSpec((1, tm, W), lambda b, i: (b, i, 0))
    in_specs = [full, half, half, vec, _const_spec(w_out_c.shape), row, vec, vec, vec,
                _const_spec(w1.shape), _const_spec(w3.shape), _const_spec(w2.shape)]
    args = [x, o, yr, g2, w_out_c, ng.reshape(1, D), sh, sc, g, w1, w3, w2]
    if final_norm:
        in_specs.append(row)
        args.append(final_g.reshape(1, D))
    return pl.pallas_call(
        functools.partial(_out_ffn_kernel, final_norm=final_norm),
        grid=(B, S // tm),
        in_specs=in_specs,
        out_specs=full,
        out_shape=jax.ShapeDtypeStruct((B, S, D), F32),
        scratch_shapes=[pltpu.VMEM((tm, D), F32)],
        compiler_params=pltpu.CompilerParams(
            dimension_semantics=("parallel", "parallel"), vmem_limit_bytes=VMEM_LIMIT),
        name="out_proj_ffn_final" if final_norm else "out_proj_ffn",
    )(*args)


def _bias_kernel(rb_ref, o_ref):
    mp = pl.program_id(0)
    T = o_ref.shape[-1]
    key = lax.broadcasted_iota(jnp.int32, (T, T), 0)
    qry = lax.broadcasted_iota(jnp.int32, (T, T), 1)
    max_exact = N_BUCKETS // 2
    last = rb_ref[N_BUCKETS - 1, mp]
    for which in range(2):
        rel = qry - key + (T if which == 0 else 0)
        n = jnp.maximum(rel, 0)
        nf = jnp.maximum(n, 1).astype(F32)
        large = max_exact + (jnp.log(nf / max_exact) / math.log(MAX_DISTANCE / max_exact)
                             * (N_BUCKETS - max_exact)).astype(jnp.int32)
        large = jnp.minimum(large, N_BUCKETS - 1)
        bucket = jnp.where(n < max_exact, n, large)
        val = jnp.zeros((T, T), F32)
        for bk in range(N_BUCKETS - 1):
            val = jnp.where(bucket == bk, (rb_ref[bk, mp] - last) * LOG2E, val)
        if which == 1:
            val = jnp.where(rel >= 0, val, NEG_BIG)
        o_ref[0, which] = val


def _bias_tiles(rel_bias, T):
    n_maps = rel_bias.shape[1]
    return pl.pallas_call(
        _bias_kernel,
        grid=(n_maps,),
        in_specs=[pl.BlockSpec(memory_space=pltpu.SMEM)],
        out_specs=pl.BlockSpec((1, 2, T, T), lambda m: (m, 0, 0, 0)),
        out_shape=jax.ShapeDtypeStruct((n_maps, 2, T, T), F32),
        name="rel_bias_tiles",
    )(rel_bias)


def _attn_kernel(q_ref, qn_ref, k_ref, vt_ref, bias_ref, lq1_ref, lk1_ref, lq2_ref, lk2_ref,
                 sg_ref, o_ref, m_ref, l_ref, acc_ref, s_ref, smax_ref, *, lambda_init):
    T = q_ref.shape[-1]
    qi = pl.program_id(2)
    q_refs = (q_ref, qn_ref)

    m_ref[...] = jnp.full_like(m_ref, NEG_BIG)
    l_ref[...] = jnp.zeros_like(l_ref)
    acc_ref[...] = jnp.zeros_like(acc_ref)

    strips = [slice(c * ATTN_STRIP, (c + 1) * ATTN_STRIP)
              for c in range(2 * T // ATTN_STRIP)]

    def n_keys(which, c):
        return strips[c].start % T + ATTN_STRIP if which == 1 else T

    def put_scores(buf, j, which, c, slot=0):
        nk = n_keys(which, c)
        start = j * T if isinstance(j, int) else pl.multiple_of(j * T, T)
        kj = k_ref[0, pl.ds(start, nk), :]
        mp, q0 = divmod(strips[c].start, T)
        qt = q_refs[slot][0, 0, 0, mp, :, q0:q0 + ATTN_STRIP]
        st = jnp.dot(kj, qt, preferred_element_type=F32)
        if which is not None:
            st = biased(st, which, mp, q0)
        s_ref[buf, c, :nk, :] = st
        smax_ref[buf, :, strips[c]] = jnp.max(st, axis=0, keepdims=True)

    def biased(st, which, mp, q0):
        rows = []
        for a in range(st.shape[0] // BIAS_T):
            cols = []
            for b in range(q0 // BIAS_T, (q0 + ATTN_STRIP) // BIAS_T):
                lo = (b - q0 // BIAS_T) * BIAS_T
                piece = st[a * BIAS_T:(a + 1) * BIAS_T, lo:lo + BIAS_T]
                dist = b - a + (T // BIAS_T if which == 0 else 0)
                if dist < 0:
                    piece = jnp.full_like(piece, NEG_BIG)
                elif dist <= 1:
                    piece = piece + bias_ref[0, mp, 1 - dist]
                cols.append(piece)
            rows.append(jnp.concatenate(cols, axis=1))
        return jnp.concatenate(rows, axis=0)

    def step(buf, j, which, nxt, hold=0):
        early = len(strips) - hold
        for c in range(ATTN_LOOKAHEAD):
            put_scores(*nxt[:3], c, nxt[3])
        for c, sl in enumerate(strips):
            if c + ATTN_LOOKAHEAD < early:
                put_scores(*nxt[:3], c + ATTN_LOOKAHEAD, nxt[3])
            nk = n_keys(which, c)
            m_old = m_ref[:, sl]
            m_new = jnp.maximum(m_old, smax_ref[buf, :, sl])
            alpha = jnp.exp2(m_old - m_new)
            l_new = alpha * l_ref[:, sl]
            acc_new = alpha * acc_ref[c]
            for k0 in range(0, nk, ATTN_KCHUNK):
                keys = slice(k0, k0 + ATTN_KCHUNK)
                p = jnp.exp2(s_ref[buf, c, keys, :] - m_new)
                l_new = l_new + jnp.sum(p, axis=0, keepdims=True)
                acc_new = acc_new + jnp.dot(vt_ref[0, j, :, keys], p.astype(BF16),
                                            preferred_element_type=F32)
            l_ref[:, sl] = l_new
            acc_ref[c] = acc_new
            m_ref[:, sl] = m_new
        for c in range(early, len(strips)):
            put_scores(*nxt[:3], c, nxt[3])

    def last_step(buf, next_which=None):
        step(buf, qi, 1, (0, 0, next_which, 1), hold=ATTN_HOLD)
        lam = (jnp.exp(jnp.sum(lq1_ref[...] * lk1_ref[...], axis=-1, keepdims=True))
               - jnp.exp(jnp.sum(lq2_ref[...] * lk2_ref[...], axis=-1, keepdims=True))
               + lambda_init)
        o_all = jnp.concatenate([acc_ref[c] for c in range(len(strips))],
                                axis=1) / l_ref[...]
        ot = o_all[:, :T] - lam * o_all[:, T:]
        ot = ot * lax.rsqrt(jnp.mean(ot * ot, axis=0, keepdims=True) + NORM_EPS)
        o_ref[0] = (ot.T * sg_ref[...] * (1.0 - lambda_init)).astype(o_ref.dtype)

    @pl.when(qi == 0)
    def _():
        for c in range(len(strips)):
            put_scores(2, 0, 1, c)
        last_step(2, next_which=0)

    @pl.when(qi >= 1)
    def _():
        n_far = qi - 1

        def pair(i, carry):
            j = 2 * i
            step(0, j, None, (1, j + 1, None, 0))
            step(1, j + 1, None, (0, j + 2, None, 0))
            return carry

        lax.fori_loop(0, jnp.maximum((n_far - 1) // 2, 0), pair, 0)

        @pl.when(n_far == 0)
        def _():
            step(0, qi - 1, 0, (1, qi, 1, 0))
            last_step(1)

        @pl.when((n_far >= 2) & (n_far % 2 == 0))
        def _():
            step(0, qi - 3, None, (1, qi - 2, None, 0))
            step(1, qi - 2, None, (0, qi - 1, 0, 0))
            step(0, qi - 1, 0, (1, qi, 1, 0))
            last_step(1)

        @pl.when(n_far % 2 == 1)
        def _():
            step(0, qi - 2, None, (1, qi - 1, 0, 0))
            step(1, qi - 1, 0, (2, qi, 1, 0))
            last_step(2)


def _attention(q, k, vt, bias, lq1, lk1, lq2, lk2, subln_g, lambda_init):
    B, S, _ = k.shape
    T = ATTN_T
    H, dv, dqk = ATTN_HEADS, ATTN_V_DIM, ATTN_QK_DIM
    bias = bias.reshape(H, 2, 2, BIAS_T, BIAS_T)
    small = lambda n: _const_spec((1, n))
    return pl.pallas_call(
        functools.partial(_attn_kernel, lambda_init=lambda_init),
        grid=(B, H, S // T),
        in_specs=[pl.BlockSpec((1, 1, 1, 2, 2 * dqk, T), lambda b, h, i: (b, h, i, 0, 0, 0)),
                  pl.BlockSpec((1, 1, 1, 2, 2 * dqk, T),
                               lambda b, h, i: (b, h, jnp.minimum(i + 1, S // T - 1), 0, 0, 0)),
                  pl.BlockSpec((1, S, 2 * dqk), lambda b, h, i: (b, 0, h)),
                  pl.BlockSpec((1, S // T, dv, T), lambda b, h, i: (b, 0, h, 0)),
                  pl.BlockSpec((1, 2, 2, BIAS_T, BIAS_T), lambda b, h, i: (h, 0, 0, 0, 0)),
                  small(dqk), small(dqk), small(dqk), small(dqk), small(dv)],
        out_specs=pl.BlockSpec((1, T, dv), lambda b, h, i: (b, i, h)),
        out_shape=jax.ShapeDtypeStruct((B, S, H * dv), BF16),
        scratch_shapes=[pltpu.VMEM((1, 2 * T), F32),
                        pltpu.VMEM((1, 2 * T), F32),
                        pltpu.VMEM((2 * T // ATTN_STRIP, dv, ATTN_STRIP), F32),
                        pltpu.VMEM((3, 2 * T // ATTN_STRIP, T, ATTN_STRIP), F32),
                        pltpu.VMEM((3, 1, 2 * T), F32)],
        compiler_params=pltpu.CompilerParams(
            dimension_semantics=("parallel", "parallel", "arbitrary"),
            vmem_limit_bytes=VMEM_LIMIT),
        name="diff_attn",
    )(q, q, k, vt, bias, lq1.reshape(1, dqk), lk1.reshape(1, dqk), lq2.reshape(1, dqk),
      lk2.reshape(1, dqk), subln_g.reshape(1, dv))


def _rglru_kernel(xr_ref, gr_ref, cw_ref, cb_ref, wg_ref, ba_ref, bi_ref, L_ref,
                  y_ref, xpad_ref, h_ref, a_buf, b_buf):
    Tc, C = xr_ref.shape[1], xr_ref.shape[2]
    G = Tc // SUBLANES

    @pl.when(pl.program_id(1) == 0)
    def _():
        xpad_ref[:SUBLANES] = jnp.zeros((SUBLANES, C), F32)
        h_ref[...] = jnp.zeros_like(h_ref)

    x = xr_ref[0]
    xpad_ref[SUBLANES:] = x
    xc = x * cw_ref[CONV_WIDTH - 1:CONV_WIDTH, :] + cb_ref[...]
    for k in range(1, CONV_WIDTH):
        xc = xc + (xpad_ref[pl.ds(SUBLANES - k, Tc), :]
                   * cw_ref[CONV_WIDTH - 1 - k:CONV_WIDTH - k, :])
    xpad_ref[:SUBLANES] = x[Tc - SUBLANES:]

    gates = jnp.dot(xc.astype(BF16), wg_ref[...], preferred_element_type=F32)
    r = jax.nn.sigmoid(gates[:, :C] + ba_ref[...])
    ig = jax.nn.sigmoid(gates[:, C:] + bi_ref[...])
    L = L_ref[...]
    log_sig = jnp.minimum(L, 0.0) - jnp.log1p(jnp.exp(-jnp.abs(L)))
    log_a = r * (RG_LRU_C * log_sig)
    a = jnp.exp(log_a)
    v = -jnp.tanh(log_a) * (a * a + 1.0)
    root = jnp.where(v > 0.0, v * lax.rsqrt(v), 0.0)
    u = root * (ig * xc)

    a3 = a.reshape(G, SUBLANES, C)
    u3 = u.reshape(G, SUBLANES, C)
    sub3 = lax.broadcasted_iota(jnp.int32, (G, SUBLANES, C), 1)
    for k in (1, 2, 4):
        a_sh = pltpu.roll(a3, k, 1)
        u_sh = pltpu.roll(u3, k, 1)
        ok = sub3 >= k
        u3 = jnp.where(ok, a3 * u_sh + u3, u3)
        a3 = jnp.where(ok, a3 * a_sh, a3)
    a_buf[...] = a3
    b_buf[...] = u3

    def body(g, hprev):
        hg = a_buf[g] * hprev + b_buf[g]
        b_buf[g] = hg
        return hg[SUBLANES - 1:SUBLANES, :]

    h_last = lax.fori_loop(0, G, body, h_ref[...], unroll=8)
    h_ref[...] = h_last

    h = b_buf[...].reshape(Tc, C)
    gr = gr_ref[0]
    c0 = math.sqrt(2.0 / math.pi)
    t = jnp.tanh(gr * (c0 + (c0 * 0.044715) * (gr * gr)))
    y_ref[0] = (h * (gr * (0.5 + 0.5 * t))).astype(y_ref.dtype)


def _rglru(xr, gr, conv_w, conv_b, w_gates, b_a, b_i, lru_L):
    B, S, C = xr.shape
    Tc = RNN_TC
    tok = pl.BlockSpec((1, Tc, C), lambda b, t: (b, t, 0))
    row = lambda: _const_spec((1, C))
    return pl.pallas_call(
        _rglru_kernel,
        grid=(B, S // Tc),
        in_specs=[tok, tok, _const_spec((CONV_WIDTH, C)), row(),
                  _const_spec(w_gates.shape), row(), row(), row()],
        out_specs=tok,
        out_shape=jax.ShapeDtypeStruct((B, S, C), BF16),
        scratch_shapes=[pltpu.VMEM((Tc + SUBLANES, C), F32),
                        pltpu.VMEM((1, C), F32),
                        pltpu.VMEM((Tc // SUBLANES, SUBLANES, C), F32),
                        pltpu.VMEM((Tc // SUBLANES, SUBLANES, C), F32)],
        compiler_params=pltpu.CompilerParams(
            dimension_semantics=("parallel", "arbitrary"), vmem_limit_bytes=VMEM_LIMIT),
        name="rg_lru",
    )(xr, gr, conv_w, conv_b.reshape(1, C), w_gates, b_a.reshape(1, C),
      b_i.reshape(1, C), lru_L.reshape(1, C))


def _to_bf16(w, l):
    return w[l].astype(BF16)


def _block_diag(w):
    n, bw, _ = w.shape
    eye = jnp.eye(n, dtype=w.dtype)
    return (eye[:, None, :, None] * w[:, :, None, :]).reshape(n * bw, n * bw)


def kernel(x, c, rel_bias, ada_w, ada_b, norm_g, ffn1_w1, ffn1_w3, ffn1_w2, w_in, lam_q1, lam_k1, lam_q2, lam_k2, subln_g, conv_w, conv_b, gate_a_w, gate_a_b, gate_i_w, gate_i_b, lru_L, w_out, ffn2_w1, ffn2_w3, ffn2_w2, final_g):
    B, S, D = x.shape
    depth = ada_w.shape[0]
    c_pad = jnp.zeros((SUBLANES, D), F32).at[:B].set(c)
    bias = _bias_tiles(rel_bias, BIAS_T)
    for l in range(depth):
        mod = _modulation(c_pad, ada_w[l], ada_b[l])[:, :B]
        sh1, sc1, g1, sh2, sc2, g2, sh3, sc3, g3 = [m.reshape(B, 1, D) for m in mod]

        w_in_b = _to_bf16(w_in, l)
        w_qvt = jnp.concatenate([w_in_b[:, :ATTN_WIDTH],
                                 w_in_b[:, 2 * ATTN_WIDTH:3 * ATTN_WIDTH]], axis=1).T
        (x, q, k, vt, xr, gr), (w1_b, w3_b, w2_b, w_out_b) = _ffn_in(
            x, norm_g[l, 0], sh1, sc1, g1,
            _to_bf16(ffn1_w1, l), _to_bf16(ffn1_w3, l), _to_bf16(ffn1_w2, l),
            norm_g[l, 1], sh2, sc2, w_in_b, w_qvt,
            [ffn2_w1[l], ffn2_w3[l], ffn2_w2[l], w_out[l]])

        lambda_init = 0.8 - 0.6 * math.exp(-0.3 * l)
        o = _attention(q, k, vt, bias, lam_q1[l], lam_k1[l], lam_q2[l], lam_k2[l],
                       subln_g[l], lambda_init)

        w_gates = jnp.concatenate(
            [_block_diag(gate_a_w[l]), _block_diag(gate_i_w[l])], axis=1).astype(BF16)
        yr = _rglru(xr, gr, conv_w[l], conv_b[l], w_gates, gate_a_b[l], gate_i_b[l],
                    lru_L[l])

        w_out_c = w_out_b.reshape(2, ATTN_WIDTH, D)
        last = l == depth - 1
        x = _out_ffn(x, o, yr, g2, w_out_c, norm_g[l, 2], sh3, sc3, g3,
                     w1_b, w3_b, w2_b,
                     final_g=final_g if last else None)
    return x
```

```python
import functools
import math

import jax
import jax.numpy as jnp
from jax import lax
from jax.experimental import pallas as pl
from jax.experimental.pallas import tpu as pltpu

F32 = jnp.float32
BF16 = jnp.bfloat16

ATTN_HEADS = 4
ATTN_QK_DIM = 64
ATTN_V_DIM = 2 * ATTN_QK_DIM
ATTN_WIDTH = ATTN_HEADS * ATTN_V_DIM
CONV_WIDTH = 4
RG_LRU_C = 8.0
N_BUCKETS = 32
MAX_DISTANCE = 128
NORM_EPS = 1e-6

SUBLANES = 8
VMEM_LIMIT = 56 * 1024 * 1024

FFN_TM = 512
FFN_FC = 256
ATTN_T = 1024
ATTN_STRIP = 256
ATTN_KCHUNK = 512
ATTN_LOOKAHEAD = 1
ATTN_HOLD = 3
BIAS_T = 128
RNN_TC = 1024
_MAX_EXACT = N_BUCKETS // 2
_LAST_BUCKET_START = math.ceil(_MAX_EXACT * (MAX_DISTANCE / _MAX_EXACT) ** (
    (N_BUCKETS - 1 - _MAX_EXACT) / (N_BUCKETS - _MAX_EXACT)))
assert _LAST_BUCKET_START <= BIAS_T <= ATTN_STRIP and ATTN_T % ATTN_STRIP == 0
NEG_BIG = -1e30
LOG2E = math.log2(math.e)


def _rms(x):
    return x * lax.rsqrt(jnp.mean(x * x, axis=-1, keepdims=True) + NORM_EPS)


def _const_spec(shape):
    nd = len(shape)
    return pl.BlockSpec(shape, lambda *_: (0,) * nd, pipeline_mode=pl.Buffered(1))


def _mod_kernel(c_ref, w_ref, b_ref, o_ref):
    c = c_ref[...]
    ca = c * jax.nn.sigmoid(c)
    o_ref[0] = jnp.dot(ca, w_ref[...], preferred_element_type=F32,
                       precision=lax.Precision.HIGHEST) + b_ref[0]


def _modulation(c_pad, ada_w, ada_b):
    rows, d = c_pad.shape
    n = ada_w.shape[1] // d
    return pl.pallas_call(
        _mod_kernel,
        grid=(n,),
        in_specs=[pl.BlockSpec((rows, d), lambda j: (0, 0)),
                  pl.BlockSpec((d, d), lambda j: (0, j)),
                  pl.BlockSpec((1, 1, d), lambda j: (j, 0, 0))],
        out_specs=pl.BlockSpec((1, rows, d), lambda j: (j, 0, 0)),
        out_shape=jax.ShapeDtypeStruct((n, rows, d), F32),
        compiler_params=pltpu.CompilerParams(vmem_limit_bytes=VMEM_LIMIT),
        name="adaln_mod",
    )(c_pad, ada_w, ada_b.reshape(n, 1, d))


def _modulated(x, ng_ref, sh_ref, sc_ref):
    return (_rms(x) * ng_ref[...] * (1.0 + sc_ref[0]) + sh_ref[0]).astype(BF16)


def _swiglu_residual(x, hb, g_ref, w1_ref, w3_ref, w2_ref, acc_ref):
    n_chunks = w1_ref.shape[1] // FFN_FC
    cols = lambda ci: slice(ci * FFN_FC, (ci + 1) * FFN_FC)
    up = lambda ci: (jnp.dot(hb, w1_ref[:, cols(ci)], preferred_element_type=F32),
                     jnp.dot(hb, w3_ref[:, cols(ci)], preferred_element_type=F32))
    a, b = up(0)
    for ci in range(n_chunks):
        if ci + 1 < n_chunks:
            a_next, b_next = up(ci + 1)
        u = (a * jax.nn.sigmoid(a) * b).astype(BF16)
        down = jnp.dot(u, w2_ref[cols(ci), :], preferred_element_type=F32)
        if ci == 0:
            acc_ref[...] = down
        else:
            acc_ref[...] += down
        a, b = a_next, b_next
    return x + 0.5 * g_ref[0] * acc_ref[...]


def _ffn_in_kernel(x_ref, ng1_ref, sh1_ref, sc1_ref, g1_ref, w1_ref, w3_ref, w2_ref,
                   ng2_ref, sh2_ref, sc2_ref, win_ref, wqvt_ref,
                   x1_ref, q_ref, k_ref, vt_ref, xr_ref, gr_ref, acc_ref):
    x = x_ref[0]
    y = _swiglu_residual(x, _modulated(x, ng1_ref, sh1_ref, sc1_ref),
                         g1_ref, w1_ref, w3_ref, w2_ref, acc_ref)
    x1_ref[0] = y
    hb = _modulated(y, ng2_ref, sh2_ref, sc2_ref)
    scale = ATTN_QK_DIM ** -0.5 * LOG2E
    W = k_ref.shape[-1]
    proj = lambda n: jnp.dot(hb, win_ref[:, n * W:(n + 1) * W], preferred_element_type=F32)
    k_ref[0] = proj(1).astype(BF16)
    qvt = lax.dot_general(wqvt_ref[...], hb, (((1,), (1,)), ((), ())),
                          preferred_element_type=F32)
    vt_ref[0, 0] = qvt[W:].astype(BF16)
    qt = (qvt[:W] * scale).astype(BF16)
    dqk = ATTN_QK_DIM
    zeros = jnp.zeros((dqk, qt.shape[1]), BF16)
    for h in range(ATTN_HEADS):
        top = qt[2 * dqk * h:2 * dqk * h + dqk]
        bot = qt[2 * dqk * h + dqk:2 * dqk * (h + 1)]
        q_ref[0, h, 0, 0] = jnp.concatenate([top, zeros], axis=0)
        q_ref[0, h, 0, 1] = jnp.concatenate([zeros, bot], axis=0)
    xr_ref[0] = proj(3)
    gr_ref[0] = proj(4)


def _ffn_in(x, ng1, sh1, sc1, g1, w1, w3, w2, ng2, sh2, sc2, w_in_b, w_qvt):
    B, S, D = x.shape
    tm, T = FFN_TM, ATTN_T
    W = w_qvt.shape[0] // 2
    H, dqk = ATTN_HEADS, ATTN_QK_DIM
    per_t = T // tm
    vec = pl.BlockSpec((1, 1, D), lambda b, i: (b, 0, 0))
    row = _const_spec((1, D))
    full = pl.BlockSpec((1, tm, D), lambda b, i: (b, i, 0))
    tok = pl.BlockSpec((1, tm, W), lambda b, i: (b, i, 0))
    vt_spec = pl.BlockSpec((1, 1, W, tm), lambda b, i: (b, i // per_t, 0, i % per_t))
    q_spec = pl.BlockSpec((1, H, 1, 2, 2 * dqk, tm),
                          lambda b, i: (b, 0, i // per_t, 0, 0, i % per_t))
    return pl.pallas_call(
        _ffn_in_kernel,
        grid=(B, S // tm),
        in_specs=[full, row, vec, vec, vec,
                  _const_spec(w1.shape), _const_spec(w3.shape), _const_spec(w2.shape),
                  row, vec, vec, _const_spec(w_in_b.shape), _const_spec(w_qvt.shape)],
        out_specs=[full, q_spec, tok, vt_spec, tok, tok],
        out_shape=[jax.ShapeDtypeStruct((B, S, D), F32),
                   jax.ShapeDtypeStruct((B, H, S // T, 2, 2 * dqk, T), BF16),
                   jax.ShapeDtypeStruct((B, S, W), BF16),
                   jax.ShapeDtypeStruct((B, S // T, W, T), BF16)]
                  + [jax.ShapeDtypeStruct((B, S, W), F32)] * 2,
        scratch_shapes=[pltpu.VMEM((tm, D), F32)],
        compiler_params=pltpu.CompilerParams(
            dimension_semantics=("parallel", "parallel"), vmem_limit_bytes=VMEM_LIMIT),
        name="ffn_in_proj",
    )(x, ng1.reshape(1, D), sh1, sc1, g1, w1, w3, w2, ng2.reshape(1, D), sh2, sc2,
      w_in_b, w_qvt)


def _out_ffn_kernel(x_ref, o_ref, yr_ref, g2_ref, wout_ref, ng_ref, sh_ref, sc_ref, g_ref,
                    w1_ref, w3_ref, w2_ref, *rest, final_norm):
    if final_norm:
        fg_ref, out_ref, acc_ref = rest
    else:
        out_ref, acc_ref = rest
    mix = (jnp.dot(o_ref[0], wout_ref[0], preferred_element_type=F32)
           + jnp.dot(yr_ref[0], wout_ref[1], preferred_element_type=F32))
    x = x_ref[0] + g2_ref[0] * mix
    y = _swiglu_residual(x, _modulated(x, ng_ref, sh_ref, sc_ref),
                         g_ref, w1_ref, w3_ref, w2_ref, acc_ref)
    if final_norm:
        y = _rms(y) * fg_ref[...]
    out_ref[0] = y


def _out_ffn(x, o, yr, g2, w_out_c, ng, sh, sc, g, w1, w3, w2, final_g=None):
    B, S, D = x.shape
    tm = FFN_TM
    W = o.shape[-1]
    final_norm = final_g is not None
    vec = pl.BlockSpec((1, 1, D), lambda b, i: (b, 0, 0))
    row = _const_spec((1, D))
    full = pl.BlockSpec((1, tm, D), lambda b, i: (b, i, 0))
    half = pl.BlockSpec((1, tm, W), lambda b, i: (b, i, 0))
    in_specs = [full, half, half, vec, _const_spec(w_out_c.shape), row, vec, vec, vec,
                _const_spec(w1.shape), _const_spec(w3.shape), _const_spec(w2.shape)]
    args = [x, o, yr, g2, w_out_c, ng.reshape(1, D), sh, sc, g, w1, w3, w2]
    if final_norm:
        in_specs.append(row)
        args.append(final_g.reshape(1, D))
    return pl.pallas_call(
        functools.partial(_out_ffn_kernel, final_norm=final_norm),
        grid=(B, S // tm),
        in_specs=in_specs,
        out_specs=full,
        out_shape=jax.ShapeDtypeStruct((B, S, D), F32),
        scratch_shapes=[pltpu.VMEM((tm, D), F32)],
        compiler_params=pltpu.CompilerParams(
            dimension_semantics=("parallel", "parallel"), vmem_limit_bytes=VMEM_LIMIT),
        name="out_proj_ffn_final" if final_norm else "out_proj_ffn",
    )(*args)


def _bias_kernel(rb_ref, o_ref):
    mp = pl.program_id(0)
    T = o_ref.shape[-1]
    key = lax.broadcasted_iota(jnp.int32, (T, T), 0)
    qry = lax.broadcasted_iota(jnp.int32, (T, T), 1)
    max_exact = N_BUCKETS // 2
    last = rb_ref[N_BUCKETS - 1, mp]
    for which in range(2):
        rel = qry - key + (T if which == 0 else 0)
        n = jnp.maximum(rel, 0)
        nf = jnp.maximum(n, 1).astype(F32)
        large = max_exact + (jnp.log(nf / max_exact) / math.log(MAX_DISTANCE / max_exact)
                             * (N_BUCKETS - max_exact)).astype(jnp.int32)
        large = jnp.minimum(large, N_BUCKETS - 1)
        bucket = jnp.where(n < max_exact, n, large)
        val = jnp.zeros((T, T), F32)
        for bk in range(N_BUCKETS - 1):
            val = jnp.where(bucket == bk, (rb_ref[bk, mp] - last) * LOG2E, val)
        if which == 1:
            val = jnp.where(rel >= 0, val, NEG_BIG)
        o_ref[0, which] = val


def _bias_tiles(rel_bias, T):
    n_maps = rel_bias.shape[1]
    return pl.pallas_call(
        _bias_kernel,
        grid=(n_maps,),
        in_specs=[pl.BlockSpec(memory_space=pltpu.SMEM)],
        out_specs=pl.BlockSpec((1, 2, T, T), lambda m: (m, 0, 0, 0)),
        out_shape=jax.ShapeDtypeStruct((n_maps, 2, T, T), F32),
        name="rel_bias_tiles",
    )(rel_bias)


def _attn_kernel(q_ref, qn_ref, k_ref, vt_ref, bias_ref, lq1_ref, lk1_ref, lq2_ref, lk2_ref,
                 sg_ref, o_ref, m_ref, l_ref, acc_ref, s_ref, smax_ref, *, lambda_init):
    T = q_ref.shape[-1]
    qi = pl.program_id(2)
    q_refs = (q_ref, qn_ref)

    m_ref[...] = jnp.full_like(m_ref, NEG_BIG)
    l_ref[...] = jnp.zeros_like(l_ref)
    acc_ref[...] = jnp.zeros_like(acc_ref)

    strips = [slice(c * ATTN_STRIP, (c + 1) * ATTN_STRIP)
              for c in range(2 * T // ATTN_STRIP)]

    def n_keys(which, c):
        return strips[c].start % T + ATTN_STRIP if which == 1 else T

    def put_scores(buf, j, which, c, slot=0):
        nk = n_keys(which, c)
        start = j * T if isinstance(j, int) else pl.multiple_of(j * T, T)
        kj = k_ref[0, pl.ds(start, nk), :]
        mp, q0 = divmod(strips[c].start, T)
        qt = q_refs[slot][0, 0, 0, mp, :, q0:q0 + ATTN_STRIP]
        st = jnp.dot(kj, qt, preferred_element_type=F32)
        if which is not None:
            st = biased(st, which, mp, q0)
        s_ref[buf, c, :nk, :] = st
        smax_ref[buf, :, strips[c]] = jnp.max(st, axis=0, keepdims=True)

    def biased(st, which, mp, q0):
        rows = []
        for a in range(st.shape[0] // BIAS_T):
            cols = []
            for b in range(q0 // BIAS_T, (q0 + ATTN_STRIP) // BIAS_T):
                lo = (b - q0 // BIAS_T) * BIAS_T
                piece = st[a * BIAS_T:(a + 1) * BIAS_T, lo:lo + BIAS_T]
                dist = b - a + (T // BIAS_T if which == 0 else 0)
                if dist < 0:
                    piece = jnp.full_like(piece, NEG_BIG)
                elif dist <= 1:
                    piece = piece + bias_ref[0, mp, 1 - dist]
                cols.append(piece)
            rows.append(jnp.concatenate(cols, axis=1))
        return jnp.concatenate(rows, axis=0)

    def step(buf, j, which, nxt, hold=0):
        early = len(strips) - hold
        for c in range(ATTN_LOOKAHEAD):
            put_scores(*nxt[:3], c, nxt[3])
        for c, sl in enumerate(strips):
            if c + ATTN_LOOKAHEAD < early:
                put_scores(*nxt[:3], c + ATTN_LOOKAHEAD, nxt[3])
            nk = n_keys(which, c)
            m_old = m_ref[:, sl]
            m_new = jnp.maximum(m_old, smax_ref[buf, :, sl])
            alpha = jnp.exp2(m_old - m_new)
            l_new = alpha * l_ref[:, sl]
            acc_new = alpha * acc_ref[c]
            for k0 in range(0, nk, ATTN_KCHUNK):
                keys = slice(k0, min(k0 + ATTN_KCHUNK, nk))
                p = jnp.exp2(s_ref[buf, c, keys, :] - m_new)
                l_new = l_new + jnp.sum(p, axis=0, keepdims=True)
                acc_new = acc_new + jnp.dot(vt_ref[0, j, :, keys], p.astype(BF16),
                                            preferred_element_type=F32)
            l_ref[:, sl] = l_new
            acc_ref[c] = acc_new
            m_ref[:, sl] = m_new
        for c in range(early, len(strips)):
            put_scores(*nxt[:3], c, nxt[3])

    def last_step(buf, next_which=None):
        step(buf, qi, 1, (0, 0, next_which, 1), hold=ATTN_HOLD)
        lam = (jnp.exp(jnp.sum(lq1_ref[...] * lk1_ref[...], axis=-1, keepdims=True))
               - jnp.exp(jnp.sum(lq2_ref[...] * lk2_ref[...], axis=-1, keepdims=True))
               + lambda_init)
        o_all = jnp.concatenate([acc_ref[c] for c in range(len(strips))],
                                axis=1) / l_ref[...]
        ot = o_all[:, :T] - lam * o_all[:, T:]
        ot = ot * lax.rsqrt(jnp.mean(ot * ot, axis=0, keepdims=True) + NORM_EPS)
        o_ref[0] = (ot.T * sg_ref[...] * (1.0 - lambda_init)).astype(o_ref.dtype)

    @pl.when(qi == 0)
    def _():
        for c in range(len(strips)):
            put_scores(2, 0, 1, c)
        last_step(2, next_which=0)

    @pl.when(qi >= 1)
    def _():
        n_far = qi - 1

        def pair(i, carry):
            j = 2 * i
            step(0, j, None, (1, j + 1, None, 0))
            step(1, j + 1, None, (0, j + 2, None, 0))
            return carry

        lax.fori_loop(0, jnp.maximum((n_far - 1) // 2, 0), pair, 0)

        @pl.when(n_far == 0)
        def _():
            step(0, qi - 1, 0, (1, qi, 1, 0))
            last_step(1)

        @pl.when((n_far >= 2) & (n_far % 2 == 0))
        def _():
            step(0, qi - 3, None, (1, qi - 2, None, 0))
            step(1, qi - 2, None, (0, qi - 1, 0, 0))
            step(0, qi - 1, 0, (1, qi, 1, 0))
            last_step(1)

        @pl.when(n_far % 2 == 1)
        def _():
            step(0, qi - 2, None, (1, qi - 1, 0, 0))
            step(1, qi - 1, 0, (2, qi, 1, 0))
            last_step(2)


def _attention(q, k, vt, bias, lq1, lk1, lq2, lk2, subln_g, lambda_init):
    B, S, _ = k.shape
    T = ATTN_T
    H, dv, dqk = ATTN_HEADS, ATTN_V_DIM, ATTN_QK_DIM
    bias = bias.reshape(H, 2, 2, BIAS_T, BIAS_T)
    small = lambda n: _const_spec((1, n))
    return pl.pallas_call(
        functools.partial(_attn_kernel, lambda_init=lambda_init),
        grid=(B, H, S // T),
        in_specs=[pl.BlockSpec((1, 1, 1, 2, 2 * dqk, T), lambda b, h, i: (b, h, i, 0, 0, 0)),
                  pl.BlockSpec((1, 1, 1, 2, 2 * dqk, T),
                               lambda b, h, i: (b, h, jnp.minimum(i + 1, S // T - 1), 0, 0, 0)),
                  pl.BlockSpec((1, S, 2 * dqk), lambda b, h, i: (b, 0, h)),
                  pl.BlockSpec((1, S // T, dv, T), lambda b, h, i: (b, 0, h, 0)),
                  pl.BlockSpec((1, 2, 2, BIAS_T, BIAS_T), lambda b, h, i: (h, 0, 0, 0, 0)),
                  small(dqk), small(dqk), small(dqk), small(dqk), small(dv)],
        out_specs=pl.BlockSpec((1, T, dv), lambda b, h, i: (b, i, h)),
        out_shape=jax.ShapeDtypeStruct((B, S, H * dv), BF16),
        scratch_shapes=[pltpu.VMEM((1, 2 * T), F32),
                        pltpu.VMEM((1, 2 * T), F32),
                        pltpu.VMEM((2 * T // ATTN_STRIP, dv, ATTN_STRIP), F32),
                        pltpu.VMEM((3, 2 * T // ATTN_STRIP, T, ATTN_STRIP), F32),
                        pltpu.VMEM((3, 1, 2 * T), F32)],
        compiler_params=pltpu.CompilerParams(
            dimension_semantics=("parallel", "parallel", "arbitrary"),
            vmem_limit_bytes=VMEM_LIMIT),
        name="diff_attn",
    )(q, q, k, vt, bias, lq1.reshape(1, dqk), lk1.reshape(1, dqk), lq2.reshape(1, dqk),
      lk2.reshape(1, dqk), subln_g.reshape(1, dv))


def _rglru_kernel(xr_ref, gr_ref, cw_ref, cb_ref, wg_ref, ba_ref, bi_ref, L_ref,
                  y_ref, xpad_ref, h_ref, a_buf, b_buf):
    Tc, C = xr_ref.shape[1], xr_ref.shape[2]
    G = Tc // SUBLANES

    @pl.when(pl.program_id(1) == 0)
    def _():
        xpad_ref[:SUBLANES] = jnp.zeros((SUBLANES, C), F32)
        h_ref[...] = jnp.zeros_like(h_ref)

    x = xr_ref[0]
    xpad_ref[SUBLANES:] = x
    xc = x * cw_ref[CONV_WIDTH - 1:CONV_WIDTH, :] + cb_ref[...]
    for k in range(1, CONV_WIDTH):
        xc = xc + (xpad_ref[pl.ds(SUBLANES - k, Tc), :]
                   * cw_ref[CONV_WIDTH - 1 - k:CONV_WIDTH - k, :])
    xpad_ref[:SUBLANES] = x[Tc - SUBLANES:]

    gates = jnp.dot(xc.astype(BF16), wg_ref[...], preferred_element_type=F32)
    r = jax.nn.sigmoid(gates[:, :C] + ba_ref[...])
    ig = jax.nn.sigmoid(gates[:, C:] + bi_ref[...])
    L = L_ref[...]
    log_sig = jnp.minimum(L, 0.0) - jnp.log1p(jnp.exp(-jnp.abs(L)))
    log_a = r * (RG_LRU_C * log_sig)
    a = jnp.exp(log_a)
    v = -jnp.tanh(log_a) * (a * a + 1.0)
    root = jnp.where(v > 0.0, v * lax.rsqrt(v), 0.0)
    u = root * (ig * xc)

    a3 = a.reshape(G, SUBLANES, C)
    u3 = u.reshape(G, SUBLANES, C)
    sub3 = lax.broadcasted_iota(jnp.int32, (G, SUBLANES, C), 1)
    for k in (1, 2, 4):
        a_sh = pltpu.roll(a3, k, 1)
        u_sh = pltpu.roll(u3, k, 1)
        ok = sub3 >= k
        u3 = jnp.where(ok, a3 * u_sh + u3, u3)
        a3 = jnp.where(ok, a3 * a_sh, a3)
    a_buf[...] = a3
    b_buf[...] = u3

    def body(g, hprev):
        hg = a_buf[g] * hprev + b_buf[g]
        b_buf[g] = hg
        return hg[SUBLANES - 1:SUBLANES, :]

    h_last = lax.fori_loop(0, G, body, h_ref[...], unroll=8)
    h_ref[...] = h_last

    h = b_buf[...].reshape(Tc, C)
    gr = gr_ref[0]
    c0 = math.sqrt(2.0 / math.pi)
    t = jnp.tanh(gr * (c0 + (c0 * 0.044715) * (gr * gr)))
    y_ref[0] = (h * (gr * (0.5 + 0.5 * t))).astype(y_ref.dtype)


def _rglru(xr, gr, conv_w, conv_b, w_gates, b_a, b_i, lru_L):
    B, S, C = xr.shape
    Tc = RNN_TC
    tok = pl.BlockSpec((1, Tc, C), lambda b, t: (b, t, 0))
    row = lambda: _const_spec((1, C))
    return pl.pallas_call(
        _rglru_kernel,
        grid=(B, S // Tc),
        in_specs=[tok, tok, _const_spec((CONV_WIDTH, C)), row(),
                  _const_spec(w_gates.shape), row(), row(), row()],
        out_specs=tok,
        out_shape=jax.ShapeDtypeStruct((B, S, C), BF16),
        scratch_shapes=[pltpu.VMEM((Tc + SUBLANES, C), F32),
                        pltpu.VMEM((1, C), F32),
                        pltpu.VMEM((Tc // SUBLANES, SUBLANES, C), F32),
                        pltpu.VMEM((Tc // SUBLANES, SUBLANES, C), F32)],
        compiler_params=pltpu.CompilerParams(
            dimension_semantics=("parallel", "arbitrary"), vmem_limit_bytes=VMEM_LIMIT),
        name="rg_lru",
    )(xr, gr, conv_w, conv_b.reshape(1, C), w_gates, b_a.reshape(1, C),
      b_i.reshape(1, C), lru_L.reshape(1, C))


def _to_bf16(w, l):
    return w[l].astype(BF16)


def _block_diag(w):
    n, bw, _ = w.shape
    eye = jnp.eye(n, dtype=w.dtype)
    return (eye[:, None, :, None] * w[:, :, None, :]).reshape(n * bw, n * bw)


def kernel(x, c, rel_bias, ada_w, ada_b, norm_g, ffn1_w1, ffn1_w3, ffn1_w2, w_in, lam_q1, lam_k1, lam_q2, lam_k2, subln_g, conv_w, conv_b, gate_a_w, gate_a_b, gate_i_w, gate_i_b, lru_L, w_out, ffn2_w1, ffn2_w3, ffn2_w2, final_g):
    B, S, D = x.shape
    depth = ada_w.shape[0]
    c_pad = jnp.zeros((SUBLANES, D), F32).at[:B].set(c)
    bias = _bias_tiles(rel_bias, BIAS_T)
    for l in range(depth):
        mod = _modulation(c_pad, ada_w[l], ada_b[l])[:, :B]
        sh1, sc1, g1, sh2, sc2, g2, sh3, sc3, g3 = [m.reshape(B, 1, D) for m in mod]

        w_in_b = _to_bf16(w_in, l)
        w_qvt = jnp.concatenate([w_in_b[:, :ATTN_WIDTH],
                                 w_in_b[:, 2 * ATTN_WIDTH:3 * ATTN_WIDTH]], axis=1).T
        x, q, k, vt, xr, gr = _ffn_in(
            x, norm_g[l, 0], sh1, sc1, g1,
            _to_bf16(ffn1_w1, l), _to_bf16(ffn1_w3, l), _to_bf16(ffn1_w2, l),
            norm_g[l, 1], sh2, sc2, w_in_b, w_qvt)

        lambda_init = 0.8 - 0.6 * math.exp(-0.3 * l)
        o = _attention(q, k, vt, bias, lam_q1[l], lam_k1[l], lam_q2[l], lam_k2[l],
                       subln_g[l], lambda_init)

        w_gates = jnp.concatenate(
            [_block_diag(gate_a_w[l]), _block_diag(gate_i_w[l])], axis=1).astype(BF16)
        yr = _rglru(xr, gr, conv_w[l], conv_b[l], w_gates, gate_a_b[l], gate_i_b[l],
                    lru_L[l])

        w_out_c = _to_bf16(w_out, l).reshape(2, ATTN_WIDTH, D)
        last = l == depth - 1
        x = _out_ffn(x, o, yr, g2, w_out_c, norm_g[l, 2], sh3, sc3, g3,
                     _to_bf16(ffn2_w1, l), _to_bf16(ffn2_w3, l), _to_bf16(ffn2_w2, l),
                     final_g=final_g if last else None)
    return x
```
